```python
import math
import jax, jax.numpy as jnp
from jax import lax
import numpy as np

D_MODEL = 1024
BATCH = 4
SEQ = 4096
DEPTH = 1

D_MIX = D_MODEL
N_ATTN_HEADS = 8
ATTN_HEAD_DIM = 64
D_ATTN = N_ATTN_HEADS * ATTN_HEAD_DIM
D_POOL = D_MIX - D_ATTN
POOL_WINDOWS = (2, 4, 8, 16)
N_POOL_GROUPS = len(POOL_WINDOWS)
POOL_GROUP_DIM = D_POOL // N_POOL_GROUPS
D_IN = 3 * D_ATTN + D_POOL
MOBA_BLOCK = 256
MOBA_TOPK = 3
Q_CHUNK = 32
N_MEM = 256
XA_HEADS = 4
XA_HEAD_DIM = D_MODEL // XA_HEADS
D_FF = 2816
MACARON_WEIGHT = 0.5
RMS_EPS = 1e-6

kernel_name = "hybrid_moba_pool_macaron_block"


def rmsnorm(x, g):
    xf = x.astype(jnp.float32)
    r = lax.rsqrt(jnp.mean(xf * xf, axis=-1, keepdims=True) + RMS_EPS)
    return (xf * r).astype(x.dtype) * g


def swiglu(h, w_gate, w_up, w_down):
    return (jax.nn.silu(h @ w_gate) * (h @ w_up)) @ w_down


def alibi_slopes(n_heads):
    s = 2.0 ** (-8.0 * np.arange(1, n_heads + 1) / n_heads)
    return jnp.asarray(s, dtype=jnp.float32)


def moba_attention(q, k, v, slopes):
    B, H, S, dh = q.shape
    L = MOBA_BLOCK
    s_pad = ((S + L - 1) // L) * L
    pad = s_pad - S
    kp = jnp.pad(k, ((0, 0), (0, 0), (0, pad), (0, 0)))
    vp = jnp.pad(v, ((0, 0), (0, 0), (0, pad), (0, 0)))
    nb = s_pad // L
    n_sel = min(MOBA_TOPK, nb)
    kb = kp.reshape(B, H, nb, L, dh)
    vb = vp.reshape(B, H, nb, L, dh)
    k_mean = jnp.mean(kb, axis=3)
    qs = q * (dh ** -0.5)
    bi = jnp.arange(B)[:, None, None, None]
    hi = jnp.arange(H)[None, :, None, None]
    lpos = jnp.arange(L)
    slot = jnp.arange(n_sel)
    blk_ids = jnp.arange(nb)
    n_chunks = S // Q_CHUNK

    def chunk(c):
        start = c * Q_CHUNK
        qc = lax.dynamic_slice_in_dim(qs, start, Q_CHUNK, axis=2)
        qpos = start + jnp.arange(Q_CHUNK)
        qpos_f = qpos.astype(jnp.float32)
        own = start // L
        gate = jnp.einsum('bhqd,bhnd->bhqn', qc, k_mean).astype(jnp.float32)
        gate = jnp.where(blk_ids < own, gate, -jnp.inf)
        _, idx = lax.top_k(gate, n_sel)
        slot_ok = slot < own
        k_sel = kb[bi, hi, idx]
        v_sel = vb[bi, hi, idx]
        kpos_sel = (idx[..., None] * L + lpos).astype(jnp.float32)
        s_sel = (jnp.einsum('bhqd,bhqrld->bhqrl', qc, k_sel).astype(jnp.float32)
                 - slopes[:, None, None, None] * (qpos_f[:, None, None] - kpos_sel))
        s_sel = jnp.where(slot_ok[:, None], s_sel, -jnp.inf)
        k_own = lax.dynamic_index_in_dim(kb, own, axis=2, keepdims=False)
        v_own = lax.dynamic_index_in_dim(vb, own, axis=2, keepdims=False)
        kpos_own = own * L + lpos
        s_own = (jnp.einsum('bhqd,bhld->bhql', qc, k_own).astype(jnp.float32)
                 - slopes[:, None, None] * (qpos_f[:, None] - kpos_own.astype(jnp.float32)[None, :]))
        s_own = jnp.where(kpos_own[None, :] <= qpos[:, None], s_own, -jnp.inf)
        scores = jnp.concatenate([s_sel.reshape(B, H, Q_CHUNK, n_sel * L), s_own], axis=-1)
        p = jax.nn.softmax(scores, axis=-1)
        p_sel = p[..., :n_sel * L].reshape(B, H, Q_CHUNK, n_sel, L).astype(v.dtype)
        p_own = p[..., n_sel * L:].astype(v.dtype)
        return (jnp.einsum('bhqrl,bhqrld->bhqd', p_sel, v_sel)
                + jnp.einsum('bhql,bhld->bhqd', p_own, v_own))

    outs = lax.map(chunk, jnp.arange(n_chunks))
    return outs.transpose(1, 2, 0, 3, 4).reshape(B, H, S, dh)


def multiscale_pool(p, pool_w, pool_scale):
    B, S, C = p.shape
    pf = p.reshape(B, S, N_POOL_GROUPS, POOL_GROUP_DIM).astype(jnp.float32)
    cs = jnp.concatenate([jnp.zeros((B, 1, N_POOL_GROUPS, POOL_GROUP_DIM), jnp.float32),
                          lax.cumsum(pf, axis=1)], axis=1)
    t = jnp.arange(S)
    outs = []
    for g, w in enumerate(POOL_WINDOWS):
        lo = jnp.maximum(t + 1 - w, 0)
        cnt = (t + 1 - lo).astype(jnp.float32)
        mean = (cs[:, 1:, g] - cs[:, lo, g]) / cnt[None, :, None]
        outs.append(mean - pf[:, :, g])
    d = jnp.stack(outs, axis=2).astype(p.dtype)
    y = jnp.einsum('bsgc,gcd->bsgd', d, pool_w)
    return y.reshape(B, S, C) * pool_scale


def memory_cross_attention(h, mem_n, wq, wkv, wo):
    B, S, _ = h.shape
    M = mem_n.shape[1]
    q = (h @ wq).reshape(B, S, XA_HEADS, XA_HEAD_DIM)
    kv = mem_n @ wkv
    k = kv[..., :D_MODEL].reshape(B, M, XA_HEADS, XA_HEAD_DIM)
    v = kv[..., D_MODEL:].reshape(B, M, XA_HEADS, XA_HEAD_DIM)
    s = jnp.einsum('bshd,bmhd->bhsm', q, k).astype(jnp.float32) * (XA_HEAD_DIM ** -0.5)
    p = jax.nn.softmax(s, axis=-1).astype(v.dtype)
    o = jnp.einsum('bhsm,bmhd->bshd', p, v).reshape(B, S, XA_HEADS * XA_HEAD_DIM)
    return o @ wo


def setup_inputs(seed: int = 0) -> dict:
    key = jax.random.key(seed)
    ks = jax.random.split(key, 24)

    def nrm(k, shape, fan_in):
        return jax.random.normal(k, shape, jnp.float32) * fan_in ** -0.5

    def gain(k, shape):
        return 1.0 + 0.1 * jax.random.normal(k, shape, jnp.float32)

    L = DEPTH
    return {
        "x": jax.random.normal(ks[0], (BATCH, SEQ, D_MODEL), jnp.float32),
        "mem": jax.random.normal(ks[1], (BATCH, N_MEM, D_MODEL), jnp.float32),
        "ffn1_pre_g": gain(ks[2], (L, D_MODEL)),
        "ffn1_w_gate": nrm(ks[3], (L, D_MODEL, D_FF), D_MODEL),
        "ffn1_w_up": nrm(ks[4], (L, D_MODEL, D_FF), D_MODEL),
        "ffn1_w_down": nrm(ks[5], (L, D_FF, D_MODEL), D_FF),
        "ffn1_post_g": gain(ks[6], (L, D_MODEL)),
        "mix_pre_g": gain(ks[7], (L, D_MODEL)),
        "w_in": nrm(ks[8], (L, D_MODEL, D_IN), D_MODEL),
        "pool_w": nrm(ks[9], (L, N_POOL_GROUPS, POOL_GROUP_DIM, POOL_GROUP_DIM), POOL_GROUP_DIM),
        "pool_scale": gain(ks[10], (L, D_POOL)),
        "w_out": nrm(ks[11], (L, D_MIX, D_MODEL), D_MIX),
        "mix_post_g": gain(ks[12], (L, D_MODEL)),
        "xa_pre_g": gain(ks[13], (L, D_MODEL)),
        "mem_g": gain(ks[14], (L, D_MODEL)),
        "xa_wq": nrm(ks[15], (L, D_MODEL, XA_HEADS * XA_HEAD_DIM), D_MODEL),
        "xa_wkv": nrm(ks[16], (L, D_MODEL, 2 * XA_HEADS * XA_HEAD_DIM), D_MODEL),
        "xa_wo": nrm(ks[17], (L, XA_HEADS * XA_HEAD_DIM, D_MODEL), XA_HEADS * XA_HEAD_DIM),
        "xa_post_g": gain(ks[18], (L, D_MODEL)),
        "ffn2_pre_g": gain(ks[19], (L, D_MODEL)),
        "ffn2_w_gate": nrm(ks[20], (L, D_MODEL, D_FF), D_MODEL),
        "ffn2_w_up": nrm(ks[21], (L, D_MODEL, D_FF), D_MODEL),
        "ffn2_w_down": nrm(ks[22], (L, D_FF, D_MODEL), D_FF),
        "ffn2_post_g": gain(ks[23], (L, D_MODEL)),
    }


def reference(x, mem, ffn1_pre_g, ffn1_w_gate, ffn1_w_up, ffn1_w_down, ffn1_post_g,
              mix_pre_g, w_in, pool_w, pool_scale, w_out, mix_post_g,
              xa_pre_g, mem_g, xa_wq, xa_wkv, xa_wo, xa_post_g,
              ffn2_pre_g, ffn2_w_gate, ffn2_w_up, ffn2_w_down, ffn2_post_g):
    B, S, _ = x.shape
    slopes = alibi_slopes(N_ATTN_HEADS)
    for l in range(DEPTH):
        h = rmsnorm(x, ffn1_pre_g[l])
        f = swiglu(h, ffn1_w_gate[l], ffn1_w_up[l], ffn1_w_down[l])
        x = x + MACARON_WEIGHT * rmsnorm(f, ffn1_post_g[l])

        h = rmsnorm(x, mix_pre_g[l])
        proj = h @ w_in[l]
        q = proj[..., :D_ATTN]
        k = proj[..., D_ATTN:2 * D_ATTN]
        v = proj[..., 2 * D_ATTN:3 * D_ATTN]
        p_in = proj[..., 3 * D_ATTN:]
        to_heads = lambda t: t.reshape(B, S, N_ATTN_HEADS, ATTN_HEAD_DIM).transpose(0, 2, 1, 3)
        attn = moba_attention(to_heads(q), to_heads(k), to_heads(v), slopes)
        attn = attn.transpose(0, 2, 1, 3).reshape(B, S, D_ATTN)
        pool = multiscale_pool(p_in, pool_w[l], pool_scale[l])
        y = jnp.concatenate([attn, pool], axis=-1) @ w_out[l]
        x = x + rmsnorm(y, mix_post_g[l])

        h = rmsnorm(x, xa_pre_g[l])
        mem_n = rmsnorm(mem, mem_g[l])
        c = memory_cross_attention(h, mem_n, xa_wq[l], xa_wkv[l], xa_wo[l])
        x = x + rmsnorm(c, xa_post_g[l])

        h = rmsnorm(x, ffn2_pre_g[l])
        f = swiglu(h, ffn2_w_gate[l], ffn2_w_up[l], ffn2_w_down[l])
        x = x + MACARON_WEIGHT * rmsnorm(f, ffn2_post_g[l])
    return x
```

```python
import functools

import numpy as np
import jax
import jax.numpy as jnp
from jax import lax
from jax.experimental import pallas as pl
from jax.experimental.pallas import tpu as pltpu

N_ATTN_HEADS = 8
ATTN_HEAD_DIM = 64
D_ATTN = N_ATTN_HEADS * ATTN_HEAD_DIM
POOL_WINDOWS = (2, 4, 8, 16)
POOL_GROUP_DIM = 128
D_POOL = len(POOL_WINDOWS) * POOL_GROUP_DIM
MOBA_BLOCK = 256
MOBA_TOPK = 3
XA_HEADS = 4
MACARON_WEIGHT = 0.5
RMS_EPS = 1e-6

V7X_LANES = 128
V7X_VMEM_BYTES = 64 * 1024 * 1024
VMEM_LIMIT_BYTES = 56 * 1024 * 1024

TOKEN_TILE = 512
HEADS_PER_STEP = V7X_LANES // ATTN_HEAD_DIM
POOL_HALO = max(POOL_WINDOWS)

_NT = (((1,), (1,)), ((), ()))


def _rmsnorm(x, g):
    r = lax.rsqrt(jnp.mean(x * x, axis=-1, keepdims=True) + RMS_EPS)
    return (x * r) * g


def _dot(a, b):
    return jnp.dot(a, b, preferred_element_type=jnp.float32)


def _params(*semantics):
    return pltpu.CompilerParams(dimension_semantics=semantics,
                                vmem_limit_bytes=VMEM_LIMIT_BYTES)


def _const_spec(shape):
    return pl.BlockSpec(shape, lambda *_: (0,) * len(shape))


def _ffn_kernel(x_ref, pre_g_ref, wg_ref, wu_ref, wd_ref, post_g_ref, o_ref):
    x = x_ref[...]
    h = _rmsnorm(x, pre_g_ref[...]).astype(jnp.bfloat16)
    gate = _dot(h, wg_ref[...])
    up = _dot(h, wu_ref[...])
    act = (gate * jax.nn.sigmoid(gate) * up).astype(jnp.bfloat16)
    f = _dot(act, wd_ref[...])
    o_ref[...] = x + MACARON_WEIGHT * _rmsnorm(f, post_g_ref[...])


def _ffn(x, pre_g, wg, wu, wd, post_g):
    n, d = x.shape
    d_ff = wg.shape[1]
    tile = pl.BlockSpec((TOKEN_TILE, d), lambda i: (i, 0))
    return pl.pallas_call(
        _ffn_kernel,
        grid=(n // TOKEN_TILE,),
        in_specs=[tile, _const_spec((1, d)), _const_spec((d, d_ff)),
                  _const_spec((d, d_ff)), _const_spec((d_ff, d)),
                  _const_spec((1, d))],
        out_specs=tile,
        out_shape=jax.ShapeDtypeStruct((n, d), jnp.float32),
        compiler_params=_params("parallel"),
        name="ffn",
    )(x, pre_g, wg, wu, wd, post_g)


def _proj_kernel(x_ref, g_ref, wqt_ref, wk_ref, wvt_ref, wp_ref,
                 qt_ref, k_ref, kmean_ref, vt_ref, p_ref):
    h = _rmsnorm(x_ref[0], g_ref[...]).astype(jnp.bfloat16)
    qt = lax.dot_general(wqt_ref[...], h, _NT, preferred_element_type=jnp.float32)
    qt = (qt * (ATTN_HEAD_DIM ** -0.5)).astype(jnp.bfloat16)
    k = _dot(h, wk_ref[...])
    vt = lax.dot_general(wvt_ref[...], h, _NT,
                         preferred_element_type=jnp.float32).astype(jnp.bfloat16)
    p_ref[0] = _dot(h, wp_ref[...])
    for blk in range(TOKEN_TILE // MOBA_BLOCK):
        rows = slice(blk * MOBA_BLOCK, (blk + 1) * MOBA_BLOCK)
        qt_ref[0, blk] = qt[:, rows]
        vt_ref[0, blk] = vt[:, rows]
        k_blk = k[rows, :]
        k_ref[0, blk] = k_blk.astype(jnp.bfloat16)
        kmean_ref[0, blk] = jnp.sum(k_blk, axis=0, keepdims=True) * (1.0 / MOBA_BLOCK)


def _proj(x, g, wqt, wk, wvt, wp):
    b, s, d = x.shape
    nb = s // MOBA_BLOCK
    bpt = TOKEN_TILE // MOBA_BLOCK
    return pl.pallas_call(
        _proj_kernel,
        grid=(b, s // TOKEN_TILE),
        in_specs=[pl.BlockSpec((1, TOKEN_TILE, d), lambda bi, t: (bi, t, 0)),
                  _const_spec((1, d)), _const_spec((D_ATTN, d)),
                  _const_spec((d, D_ATTN)), _const_spec((D_ATTN, d)),
                  _const_spec((d, D_POOL))],
        out_specs=[
            pl.BlockSpec((1, bpt, D_ATTN, MOBA_BLOCK), lambda bi, t: (bi, t, 0, 0)),
            pl.BlockSpec((1, bpt, MOBA_BLOCK, D_ATTN), lambda bi, t: (bi, t, 0, 0)),
            pl.BlockSpec((1, bpt, 1, D_ATTN), lambda bi, t: (bi, t, 0, 0)),
            pl.BlockSpec((1, bpt, D_ATTN, MOBA_BLOCK), lambda bi, t: (bi, t, 0, 0)),
            pl.BlockSpec((1, TOKEN_TILE, D_POOL), lambda bi, t: (bi, t, 0)),
        ],
        out_shape=[
            jax.ShapeDtypeStruct((b, nb, D_ATTN, MOBA_BLOCK), jnp.bfloat16),
            jax.ShapeDtypeStruct((b, nb, MOBA_BLOCK, D_ATTN), jnp.bfloat16),
            jax.ShapeDtypeStruct((b, nb, 1, D_ATTN), jnp.float32),
            jax.ShapeDtypeStruct((b, nb, D_ATTN, MOBA_BLOCK), jnp.bfloat16),
            jax.ShapeDtypeStruct((b, s, D_POOL), jnp.float32),
        ],
        compiler_params=_params("parallel", "parallel"),
        name="mixer_proj",
    )(x, g, wqt, wk, wvt, wp)


def _moba_kernel(slopes_ref, qt_ref, k_ref, kmean_ref, vt_ref, o_ref, sel_ref):
    hp = pl.program_id(1)
    i = pl.program_id(2)
    L = MOBA_BLOCK
    nb = k_ref.shape[1]
    qt = qt_ref[0, 0]
    kmean = kmean_ref[0].astype(jnp.bfloat16)
    head_row = lax.broadcasted_iota(jnp.int32, qt.shape, 0) // ATTN_HEAD_DIM
    blk_id = lax.broadcasted_iota(jnp.int32, (nb, L), 0)
    k_minus_q = (lax.broadcasted_iota(jnp.int32, (L, L), 0)
                 - lax.broadcasted_iota(jnp.int32, (L, L), 1))
    q_minus_k = (-k_minus_q).astype(jnp.float32)
    causal = k_minus_q <= 0

    outs = []
    for hh in range(HEADS_PER_STEP):
        slope = slopes_ref[hp * HEADS_PER_STEP + hh]
        rows = slice(hh * ATTN_HEAD_DIM, (hh + 1) * ATTN_HEAD_DIM)
        qt_h = jnp.where(head_row == hh, qt, jnp.zeros_like(qt))

        gate = _dot(kmean, qt_h)
        rank = jnp.zeros((nb, L), jnp.int32)
        for jp in range(nb):
            row = gate[jp:jp + 1, :]
            ahead = (row > gate) | ((row == gate) & (jp < blk_id))
            rank = rank + jnp.where(ahead & (jp < i), 1, 0)
        sel = (rank < MOBA_TOPK) & (blk_id < i)
        sel_ref[hh] = sel.astype(jnp.float32)

        s = _dot(k_ref[0, i], qt_h) - slope * q_minus_k
        s = jnp.where(causal, s, -jnp.inf)
        m = jnp.max(s, axis=0, keepdims=True)
        p = jnp.exp(s - m)
        l = jnp.sum(p, axis=0, keepdims=True)
        acc = _dot(vt_ref[0, i, rows, :], p.astype(jnp.bfloat16))

        def past_block(j, carry, hh=hh, slope=slope, rows=rows, qt_h=qt_h):
            m, l, acc = carry
            dist = q_minus_k + ((i - j) * L).astype(jnp.float32)
            s = _dot(k_ref[0, j], qt_h) - slope * dist
            keep = sel_ref[hh, pl.ds(j, 1), :] > 0.5
            s = jnp.where(keep, s, -jnp.inf)
            m_new = jnp.maximum(m, jnp.max(s, axis=0, keepdims=True))
            alpha = jnp.exp(m - m_new)
            p = jnp.exp(s - m_new)
            l = alpha * l + jnp.sum(p, axis=0, keepdims=True)
            acc = alpha * acc + _dot(vt_ref[0, j, rows, :], p.astype(jnp.bfloat16))
            return m_new, l, acc

        m, l, acc = lax.fori_loop(0, i, past_block, (m, l, acc))
        outs.append(acc / l)

    o_t = jnp.concatenate(outs, axis=0)
    o_ref[0] = o_t.T.astype(o_ref.dtype)


def _moba(slopes, qt, k, kmean, vt):
    b, nb, _, L = qt.shape
    lanes = HEADS_PER_STEP * ATTN_HEAD_DIM
    return pl.pallas_call(
        _moba_kernel,
        grid=(b, N_ATTN_HEADS // HEADS_PER_STEP, nb),
        in_specs=[
            pl.BlockSpec(memory_space=pltpu.SMEM),
            pl.BlockSpec((1, 1, lanes, L), lambda bi, hp, i: (bi, i, hp, 0)),
            pl.BlockSpec((1, nb, L, lanes), lambda bi, hp, i: (bi, 0, 0, hp)),
            pl.BlockSpec((1, nb, lanes), lambda bi, hp, i: (bi, 0, hp)),
            pl.BlockSpec((1, nb, lanes, L), lambda bi, hp, i: (bi, 0, hp, 0)),
        ],
        out_specs=pl.BlockSpec((1, L, lanes), lambda bi, hp, i: (bi, i, hp)),
        out_shape=jax.ShapeDtypeStruct((b, nb * L, D_ATTN), jnp.bfloat16),
        scratch_shapes=[pltpu.VMEM((HEADS_PER_STEP, nb, L), jnp.float32)],
        compiler_params=_params("parallel", "parallel", "arbitrary"),
        name="moba_attention",
    )(slopes, qt, k, kmean, vt)


def _mixout_kernel(x_ref, attn_ref, p_ref, halo_ref, pool_w_ref, pool_scale_ref,
                   w_out_ref, g_ref, o_ref):
    t = pl.program_id(1)
    tm = p_ref.shape[1]
    p = p_ref[0]
    halo = jnp.where(t > 0, halo_ref[0], 0.0)
    ext = jnp.concatenate([halo, p], axis=0)
    pos = t * tm + lax.broadcasted_iota(jnp.int32, (tm, 1), 0)
    pooled = []
    for g, w in enumerate(POOL_WINDOWS):
        cols = slice(g * POOL_GROUP_DIM, (g + 1) * POOL_GROUP_DIM)
        acc = ext[:, cols]
        n_valid = acc.shape[0]
        span = 1
        while span < w:
            acc = acc[span:, :] + acc[:n_valid - span, :]
            n_valid -= span
            span *= 2
        win_sum = acc[n_valid - tm:, :]
        cnt = jnp.minimum(pos + 1, w).astype(jnp.float32)
        d = win_sum / cnt - p[:, cols]
        y = _dot(d.astype(jnp.bfloat16), pool_w_ref[g])
        pooled.append(y * pool_scale_ref[:, cols])
    pool = jnp.concatenate(pooled, axis=-1).astype(jnp.bfloat16)
    mixed = jnp.concatenate([attn_ref[0], pool], axis=-1)
    y = _dot(mixed, w_out_ref[...])
    o_ref[0] = x_ref[0] + _rmsnorm(y, g_ref[...])


def _mixout(x, attn, p_in, pool_w, pool_scale, w_out, g):
    b, s, d = x.shape
    halo_blocks = TOKEN_TILE // POOL_HALO
    tile = lambda width: pl.BlockSpec((1, TOKEN_TILE, width), lambda bi, t: (bi, t, 0))
    return pl.pallas_call(
        _mixout_kernel,
        grid=(b, s // TOKEN_TILE),
        in_specs=[
            tile(d), tile(D_ATTN), tile(D_POOL),
            pl.BlockSpec((1, POOL_HALO, D_POOL),
                         lambda bi, t: (bi, jnp.maximum(t * halo_blocks - 1, 0), 0)),
            _const_spec(pool_w.shape), _const_spec((1, D_POOL)),
            _const_spec(w_out.shape), _const_spec((1, d)),
        ],
        out_specs=tile(d),
        out_shape=jax.ShapeDtypeStruct((b, s, d), jnp.float32),
        compiler_params=_params("parallel", "parallel"),
        name="mixer_out",
    )(x, attn, p_in, p_in, pool_w, pool_scale, w_out, g)


def _memkv_kernel(mem_ref, g_ref, wkv_ref, k_ref, v_ref):
    d = mem_ref.shape[2]
    mem_n = _rmsnorm(mem_ref[0], g_ref[...]).astype(jnp.bfloat16)
    kv = _dot(mem_n, wkv_ref[...])
    k_ref[0] = kv[:, :d].astype(jnp.bfloat16)
    v_ref[0] = kv[:, d:].astype(jnp.bfloat16)


def _memkv(mem, g, wkv):
    b, m, d = mem.shape
    blk = pl.BlockSpec((1, m, d), lambda bi: (bi, 0, 0))
    return pl.pallas_call(
        _memkv_kernel,
        grid=(b,),
        in_specs=[blk, _const_spec((1, d)), _const_spec(wkv.shape)],
        out_specs=[blk, blk],
        out_shape=[jax.ShapeDtypeStruct((b, m, d), jnp.bfloat16)] * 2,
        compiler_params=_params("parallel"),
        name="mem_kv",
    )(mem, g, wkv)


def _xattn_kernel(x_ref, pre_g_ref, wq_ref, k_ref, v_ref, wo_ref, post_g_ref, o_ref):
    x = x_ref[0]
    d = x.shape[1]
    dh = d // XA_HEADS
    h = _rmsnorm(x, pre_g_ref[...]).astype(jnp.bfloat16)
    q = _dot(h, wq_ref[...]).astype(jnp.bfloat16)
    heads = []
    for hd in range(XA_HEADS):
        cols = slice(hd * dh, (hd + 1) * dh)
        s = lax.dot_general(q[:, cols], k_ref[0, :, cols], _NT,
                            preferred_element_type=jnp.float32) * (dh ** -0.5)
        e = jnp.exp(s - jnp.max(s, axis=-1, keepdims=True))
        p = e / jnp.sum(e, axis=-1, keepdims=True)
        heads.append(_dot(p.astype(jnp.bfloat16), v_ref[0, :, cols]))
    o = jnp.concatenate(heads, axis=-1).astype(jnp.bfloat16)
    c = _dot(o, wo_ref[...])
    o_ref[0] = x + _rmsnorm(c, post_g_ref[...])


def _xattn(x, pre_g, wq, k_mem, v_mem, wo, post_g):
    b, s, d = x.shape
    m = k_mem.shape[1]
    tile = pl.BlockSpec((1, TOKEN_TILE, d), lambda bi, t: (bi, t, 0))
    mem_blk = pl.BlockSpec((1, m, d), lambda bi, t: (bi, 0, 0))
    return pl.pallas_call(
        _xattn_kernel,
        grid=(b, s // TOKEN_TILE),
        in_specs=[tile, _const_spec((1, d)), _const_spec(wq.shape), mem_blk, mem_blk,
                  _const_spec(wo.shape), _const_spec((1, d))],
        out_specs=tile,
        out_shape=jax.ShapeDtypeStruct((b, s, d), jnp.float32),
        compiler_params=_params("parallel", "parallel"),
        name="mem_xattn",
    )(x, pre_g, wq, k_mem, v_mem, wo, post_g)


def _alibi_slopes(n_heads):
    return jnp.asarray(2.0 ** (-8.0 * np.arange(1, n_heads + 1) / n_heads), jnp.float32)


def kernel(x, mem, ffn1_pre_g, ffn1_w_gate, ffn1_w_up, ffn1_w_down, ffn1_post_g, mix_pre_g, w_in, pool_w, pool_scale, w_out, mix_post_g, xa_pre_g, mem_g, xa_wq, xa_wkv, xa_wo, xa_post_g, ffn2_pre_g, ffn2_w_gate, ffn2_w_up, ffn2_w_down, ffn2_post_g):
    b, s, d = x.shape
    depth = ffn1_pre_g.shape[0]
    bf = lambda w: w.astype(jnp.bfloat16)
    row = lambda g: g.reshape(1, -1)
    slopes = _alibi_slopes(N_ATTN_HEADS)
    for l in range(depth):
        x = _ffn(x.reshape(b * s, d), row(ffn1_pre_g[l]), bf(ffn1_w_gate[l]),
                 bf(ffn1_w_up[l]), bf(ffn1_w_down[l]), row(ffn1_post_g[l])).reshape(b, s, d)

        w = bf(w_in[l])
        qt, k, kmean, vt, p_in = _proj(
            x, row(mix_pre_g[l]), w[:, :D_ATTN].T, w[:, D_ATTN:2 * D_ATTN],
            w[:, 2 * D_ATTN:3 * D_ATTN].T, w[:, 3 * D_ATTN:])
        attn = _moba(slopes, qt, k, kmean.reshape(b, -1, D_ATTN), vt)
        x = _mixout(x, attn, p_in, bf(pool_w[l]), row(pool_scale[l]), bf(w_out[l]),
                    row(mix_post_g[l]))

        k_mem, v_mem = _memkv(mem, row(mem_g[l]), bf(xa_wkv[l]))
        x = _xattn(x, row(xa_pre_g[l]), bf(xa_wq[l]), k_mem, v_mem, bf(xa_wo[l]),
                   row(xa_post_g[l]))

        x = _ffn(x.reshape(b * s, d), row(ffn2_pre_g[l]), bf(ffn2_w_gate[l]),
                 bf(ffn2_w_up[l]), bf(ffn2_w_down[l]), row(ffn2_post_g[l])).reshape(b, s, d)
    return x
```

```python
import functools

import numpy as np
import jax
import jax.numpy as jnp
from jax import lax
from jax.experimental import pallas as pl
from jax.experimental.pallas import tpu as pltpu

N_ATTN_HEADS = 8
ATTN_HEAD_DIM = 64
D_ATTN = N_ATTN_HEADS * ATTN_HEAD_DIM
POOL_WINDOWS = (2, 4, 8, 16)
POOL_GROUP_DIM = 128
D_POOL = len(POOL_WINDOWS) * POOL_GROUP_DIM
MOBA_BLOCK = 256
MOBA_TOPK = 3
XA_HEADS = 4
MACARON_WEIGHT = 0.5
RMS_EPS = 1e-6

V7X_LANES = 128
V7X_VMEM_BYTES = 64 * 1024 * 1024
VMEM_LIMIT_BYTES = 56 * 1024 * 1024

TOKEN_TILE = 512
HEADS_PER_STEP = V7X_LANES // ATTN_HEAD_DIM
POOL_HALO = max(POOL_WINDOWS)

_NT = (((1,), (1,)), ((), ()))


def _rmsnorm(x, g):
    r = lax.rsqrt(jnp.mean(x * x, axis=-1, keepdims=True) + RMS_EPS)
    return (x * r) * g


def _dot(a, b):
    return jnp.dot(a, b, preferred_element_type=jnp.float32)


def _params(*semantics):
    return pltpu.CompilerParams(dimension_semantics=semantics,
                                vmem_limit_bytes=VMEM_LIMIT_BYTES)


def _const_spec(shape):
    return pl.BlockSpec(shape, lambda *_: (0,) * len(shape))


def _ffn_kernel(x_ref, pre_g_ref, wg_ref, wu_ref, wd_ref, post_g_ref, o_ref):
    x = x_ref[...]
    h = _rmsnorm(x, pre_g_ref[...]).astype(jnp.bfloat16)
    gate = _dot(h, wg_ref[...])
    up = _dot(h, wu_ref[...])
    act = (gate * jax.nn.sigmoid(gate) * up).astype(jnp.bfloat16)
    f = _dot(act, wd_ref[...])
    o_ref[...] = x + MACARON_WEIGHT * _rmsnorm(f, post_g_ref[...])


def _ffn(x, pre_g, wg, wu, wd, post_g):
    n, d = x.shape
    d_ff = wg.shape[1]
    tile = pl.BlockSpec((TOKEN_TILE, d), lambda i: (i, 0))
    return pl.pallas_call(
        _ffn_kernel,
        grid=(n // TOKEN_TILE,),
        in_specs=[tile, _const_spec((1, d)), _const_spec((d, d_ff)),
                  _const_spec((d, d_ff)), _const_spec((d_ff, d)),
                  _const_spec((1, d))],
        out_specs=tile,
        out_shape=jax.ShapeDtypeStruct((n, d), jnp.float32),
        compiler_params=_params("parallel"),
        name="ffn",
    )(x, pre_g, wg, wu, wd, post_g)


def _proj_kernel(x_ref, g_ref, wqt_ref, wk_ref, wvt_ref, wp_ref,
                 qt_ref, k_ref, kmean_ref, vt_ref, p_ref):
    h = _rmsnorm(x_ref[0], g_ref[...]).astype(jnp.bfloat16)
    qt = lax.dot_general(wqt_ref[...], h, _NT, preferred_element_type=jnp.float32)
    qt = (qt * (ATTN_HEAD_DIM ** -0.5)).astype(jnp.bfloat16)
    k = _dot(h, wk_ref[...])
    vt = lax.dot_general(wvt_ref[...], h, _NT,
                         preferred_element_type=jnp.float32).astype(jnp.bfloat16)
    p_ref[0] = _dot(h, wp_ref[...])
    for blk in range(TOKEN_TILE // MOBA_BLOCK):
        rows = slice(blk * MOBA_BLOCK, (blk + 1) * MOBA_BLOCK)
        qt_ref[0, blk] = qt[:, rows]
        vt_ref[0, blk] = vt[:, rows]
        k_blk = k[rows, :]
        k_ref[0, blk] = k_blk.astype(jnp.bfloat16)
        kmean_ref[0, blk] = jnp.sum(k_blk, axis=0, keepdims=True) * (1.0 / MOBA_BLOCK)


def _proj(x, g, wqt, wk, wvt, wp):
    b, s, d = x.shape
    nb = s // MOBA_BLOCK
    bpt = TOKEN_TILE // MOBA_BLOCK
    return pl.pallas_call(
        _proj_kernel,
        grid=(b, s // TOKEN_TILE),
        in_specs=[pl.BlockSpec((1, TOKEN_TILE, d), lambda bi, t: (bi, t, 0)),
                  _const_spec((1, d)), _const_spec((D_ATTN, d)),
                  _const_spec((d, D_ATTN)), _const_spec((D_ATTN, d)),
                  _const_spec((d, D_POOL))],
        out_specs=[
            pl.BlockSpec((1, bpt, D_ATTN, MOBA_BLOCK), lambda bi, t: (bi, t, 0, 0)),
            pl.BlockSpec((1, bpt, MOBA_BLOCK, D_ATTN), lambda bi, t: (bi, t, 0, 0)),
            pl.BlockSpec((1, bpt, 1, D_ATTN), lambda bi, t: (bi, t, 0, 0)),
            pl.BlockSpec((1, bpt, D_ATTN, MOBA_BLOCK), lambda bi, t: (bi, t, 0, 0)),
            pl.BlockSpec((1, TOKEN_TILE, D_POOL), lambda bi, t: (bi, t, 0)),
        ],
        out_shape=[
            jax.ShapeDtypeStruct((b, nb, D_ATTN, MOBA_BLOCK), jnp.bfloat16),
            jax.ShapeDtypeStruct((b, nb, MOBA_BLOCK, D_ATTN), jnp.bfloat16),
            jax.ShapeDtypeStruct((b, nb, 1, D_ATTN), jnp.float32),
            jax.ShapeDtypeStruct((b, nb, D_ATTN, MOBA_BLOCK), jnp.bfloat16),
            jax.ShapeDtypeStruct((b, s, D_POOL), jnp.float32),
        ],
        compiler_params=_params("parallel", "parallel"),
        name="mixer_proj",
    )(x, g, wqt, wk, wvt, wp)


def _moba_kernel(slopes_ref, qt_ref, k_ref, kmean_ref, vt_ref, o_ref, sel_ref, bias_ref):
    hp = pl.program_id(1)
    i = pl.program_id(2)
    L = MOBA_BLOCK
    nb = vt_ref.shape[1]
    heads = range(HEADS_PER_STEP)
    slopes = [slopes_ref[hp * HEADS_PER_STEP + hh] for hh in heads]
    head_rows = [slice(hh * ATTN_HEAD_DIM, (hh + 1) * ATTN_HEAD_DIM) for hh in heads]

    @pl.when(i == 0)
    def _():
        q_minus_k = (lax.broadcasted_iota(jnp.int32, (2 * L, L), 1)
                     - lax.broadcasted_iota(jnp.int32, (2 * L, L), 0)).astype(jnp.float32)
        for hh in heads:
            bias_ref[hh] = slopes[hh] * q_minus_k

    qt = qt_ref[0, 0]
    kmean = kmean_ref[0].astype(jnp.bfloat16)
    head_row = lax.broadcasted_iota(jnp.int32, qt.shape, 0) // ATTN_HEAD_DIM
    blk_id = lax.broadcasted_iota(jnp.int32, (nb, L), 0)
    causal = (lax.broadcasted_iota(jnp.int32, (L, L), 0)
              <= lax.broadcasted_iota(jnp.int32, (L, L), 1))
    own_rows = pl.ds(pl.multiple_of(i * L, L), L)

    qt_heads, carry = [], []
    for hh in heads:
        qt_h = jnp.where(head_row == hh, qt, jnp.zeros_like(qt))
        qt_heads.append(qt_h)

        gate = jnp.where(blk_id < i, _dot(kmean, qt_h), -jnp.inf)
        rank = jnp.zeros((nb, L), jnp.float32)
        for jp in range(nb):
            row = gate[jp:jp + 1, :]
            ahead = (row > gate) | ((row == gate) & (jp < blk_id))
            rank = rank + jnp.where(ahead, 1.0, 0.0)
        sel = (rank < MOBA_TOPK) & (blk_id < i)
        sel_ref[hh] = sel.astype(jnp.float32)

        s = _dot(k_ref[0, own_rows, :], qt_h) - bias_ref[hh, :L, :]
        s = jnp.where(causal, s, -jnp.inf)
        m = jnp.max(s, axis=0, keepdims=True)
        p = jnp.exp(s - m)
        l = jnp.sum(p, axis=0, keepdims=True)
        acc = _dot(vt_ref[0, i, head_rows[hh], :], p.astype(jnp.bfloat16))
        carry.append((m, l, acc))

    def past_pair(pair, carry):
        j0 = 2 * pair
        pair_rows = pl.ds(pl.multiple_of(j0 * L, L), 2 * L)
        k_pair = k_ref[0, pair_rows, :]
        block_dist = ((i - j0) * L).astype(jnp.float32)
        new_carry = []
        s_heads = [_dot(k_pair, qt_heads[hh]) for hh in heads]
        for hh in heads:
            m, l, acc = carry[hh]
            offset = slopes[hh] * block_dist
            t = s_heads[hh] - bias_ref[hh]
            keep = [sel_ref[hh, pl.ds(j0 + blk, 1), :] > 0.5 for blk in range(2)]
            t_blk = [t[:L, :], t[L:, :]]
            m_blk = [jnp.where(keep[blk], jnp.max(t_blk[blk], axis=0, keepdims=True),
                               -jnp.inf) for blk in range(2)]
            m_new = jnp.maximum(m, jnp.maximum(m_blk[0], m_blk[1]) - offset)
            alpha = jnp.exp(m - m_new)
            l = alpha * l
            acc = alpha * acc
            for blk in range(2):
                shift = jnp.where(keep[blk], m_new + offset, jnp.inf)
                p = jnp.exp(t_blk[blk] - shift)
                l = l + jnp.sum(p, axis=0, keepdims=True)
                acc = acc + _dot(vt_ref[0, j0 + blk, head_rows[hh], :],
                                 p.astype(jnp.bfloat16))
            new_carry.append((m_new, l, acc))
        return tuple(new_carry)

    carry = lax.fori_loop(0, (i + 1) // 2, past_pair, tuple(carry))
    o_t = jnp.concatenate([acc / l for _, l, acc in carry], axis=0)
    o_ref[0] = o_t.T.astype(o_ref.dtype)


def _moba(slopes, qt, k, kmean, vt):
    b, nb, _, L = qt.shape
    lanes = HEADS_PER_STEP * ATTN_HEAD_DIM
    return pl.pallas_call(
        _moba_kernel,
        grid=(b, N_ATTN_HEADS // HEADS_PER_STEP, nb),
        in_specs=[
            pl.BlockSpec(memory_space=pltpu.SMEM),
            pl.BlockSpec((1, 1, lanes, L), lambda bi, hp, i: (bi, i, hp, 0)),
            pl.BlockSpec((1, nb * L, lanes), lambda bi, hp, i: (bi, 0, hp)),
            pl.BlockSpec((1, nb, lanes), lambda bi, hp, i: (bi, 0, hp)),
            pl.BlockSpec((1, nb, lanes, L), lambda bi, hp, i: (bi, 0, hp, 0)),
        ],
        out_specs=pl.BlockSpec((1, L, lanes), lambda bi, hp, i: (bi, i, hp)),
        out_shape=jax.ShapeDtypeStruct((b, nb * L, D_ATTN), jnp.bfloat16),
        scratch_shapes=[pltpu.VMEM((HEADS_PER_STEP, nb, L), jnp.float32),
                        pltpu.VMEM((HEADS_PER_STEP, 2 * L, L), jnp.float32)],
        compiler_params=_params("parallel", "parallel", "arbitrary"),
        name="moba_attention",
    )(slopes, qt, k, kmean, vt)


def _mixout_kernel(x_ref, attn_ref, p_ref, halo_ref, pool_w_ref, pool_scale_ref,
                   w_out_ref, g_ref, o_ref):
    t = pl.program_id(1)
    tm = p_ref.shape[1]
    p = p_ref[0]
    halo = jnp.where(t > 0, halo_ref[0], 0.0)
    ext = jnp.concatenate([halo, p], axis=0)
    pos = t * tm + lax.broadcasted_iota(jnp.int32, (tm, 1), 0)
    pooled = []
    for g, w in enumerate(POOL_WINDOWS):
        cols = slice(g * POOL_GROUP_DIM, (g + 1) * POOL_GROUP_DIM)
        acc = ext[:, cols]
        n_valid = acc.shape[0]
        span = 1
        while span < w:
            acc = acc[span:, :] + acc[:n_valid - span, :]
            n_valid -= span
            span *= 2
        win_sum = acc[n_valid - tm:, :]
        cnt = jnp.minimum(pos + 1, w).astype(jnp.float32)
        d = win_sum / cnt - p[:, cols]
        y = _dot(d.astype(jnp.bfloat16), pool_w_ref[g])
        pooled.append(y * pool_scale_ref[:, cols])
    pool = jnp.concatenate(pooled, axis=-1).astype(jnp.bfloat16)
    mixed = jnp.concatenate([attn_ref[0], pool], axis=-1)
    y = _dot(mixed, w_out_ref[...])
    o_ref[0] = x_ref[0] + _rmsnorm(y, g_ref[...])


def _mixout(x, attn, p_in, pool_w, pool_scale, w_out, g):
    b, s, d = x.shape
    halo_blocks = TOKEN_TILE // POOL_HALO
    tile = lambda width: pl.BlockSpec((1, TOKEN_TILE, width), lambda bi, t: (bi, t, 0))
    return pl.pallas_call(
        _mixout_kernel,
        grid=(b, s // TOKEN_TILE),
        in_specs=[
            tile(d), tile(D_ATTN), tile(D_POOL),
            pl.BlockSpec((1, POOL_HALO, D_POOL),
                         lambda bi, t: (bi, jnp.maximum(t * halo_blocks - 1, 0), 0)),
            _const_spec(pool_w.shape), _const_spec((1, D_POOL)),
            _const_spec(w_out.shape), _const_spec((1, d)),
        ],
        out_specs=tile(d),
        out_shape=jax.ShapeDtypeStruct((b, s, d), jnp.float32),
        compiler_params=_params("parallel", "parallel"),
        name="mixer_out",
    )(x, attn, p_in, p_in, pool_w, pool_scale, w_out, g)


def _memkv_kernel(mem_ref, g_ref, wkv_ref, k_ref, v_ref):
    d = mem_ref.shape[2]
    mem_n = _rmsnorm(mem_ref[0], g_ref[...]).astype(jnp.bfloat16)
    kv = _dot(mem_n, wkv_ref[...])
    k_ref[0] = kv[:, :d].astype(jnp.bfloat16)
    v_ref[0] = kv[:, d:].astype(jnp.bfloat16)


def _memkv(mem, g, wkv):
    b, m, d = mem.shape
    blk = pl.BlockSpec((1, m, d), lambda bi: (bi, 0, 0))
    return pl.pallas_call(
        _memkv_kernel,
        grid=(b,),
        in_specs=[blk, _const_spec((1, d)), _const_spec(wkv.shape)],
        out_specs=[blk, blk],
        out_shape=[jax.ShapeDtypeStruct((b, m, d), jnp.bfloat16)] * 2,
        compiler_params=_params("parallel"),
        name="mem_kv",
    )(mem, g, wkv)


def _xattn_kernel(x_ref, pre_g_ref, wq_ref, k_ref, v_ref, wo_ref, post_g_ref, o_ref):
    x = x_ref[0]
    d = x.shape[1]
    dh = d // XA_HEADS
    h = _rmsnorm(x, pre_g_ref[...]).astype(jnp.bfloat16)
    q = _dot(h, wq_ref[...]).astype(jnp.bfloat16)
    heads = []
    for hd in range(XA_HEADS):
        cols = slice(hd * dh, (hd + 1) * dh)
        s = lax.dot_general(q[:, cols], k_ref[0, :, cols], _NT,
                            preferred_element_type=jnp.float32) * (dh ** -0.5)
        e = jnp.exp(s - jnp.max(s, axis=-1, keepdims=True))
        p = e / jnp.sum(e, axis=-1, keepdims=True)
        heads.append(_dot(p.astype(jnp.bfloat16), v_ref[0, :, cols]))
    o = jnp.concatenate(heads, axis=-1).astype(jnp.bfloat16)
    c = _dot(o, wo_ref[...])
    o_ref[0] = x + _rmsnorm(c, post_g_ref[...])


def _xattn(x, pre_g, wq, k_mem, v_mem, wo, post_g):
    b, s, d = x.shape
    m = k_mem.shape[1]
    tile = pl.BlockSpec((1, TOKEN_TILE, d), lambda bi, t: (bi, t, 0))
    mem_blk = pl.BlockSpec((1, m, d), lambda bi, t: (bi, 0, 0))
    return pl.pallas_call(
        _xattn_kernel,
        grid=(b, s // TOKEN_TILE),
        in_specs=[tile, _const_spec((1, d)), _const_spec(wq.shape), mem_blk, mem_blk,
                  _const_spec(wo.shape), _const_spec((1, d))],
        out_specs=tile,
        out_shape=jax.ShapeDtypeStruct((b, s, d), jnp.float32),
        compiler_params=_params("parallel", "parallel"),
        name="mem_xattn",
    )(x, pre_g, wq, k_mem, v_mem, wo, post_g)


def _alibi_slopes(n_heads):
    return jnp.asarray(2.0 ** (-8.0 * np.arange(1, n_heads + 1) / n_heads), jnp.float32)


def kernel(x, mem, ffn1_pre_g, ffn1_w_gate, ffn1_w_up, ffn1_w_down, ffn1_post_g, mix_pre_g, w_in, pool_w, pool_scale, w_out, mix_post_g, xa_pre_g, mem_g, xa_wq, xa_wkv, xa_wo, xa_post_g, ffn2_pre_g, ffn2_w_gate, ffn2_w_up, ffn2_w_down, ffn2_post_g):
    b, s, d = x.shape
    depth = ffn1_pre_g.shape[0]
    bf = lambda w: w.astype(jnp.bfloat16)
    row = lambda g: g.reshape(1, -1)
    slopes = _alibi_slopes(N_ATTN_HEADS)
    for l in range(depth):
        x = _ffn(x.reshape(b * s, d), row(ffn1_pre_g[l]), bf(ffn1_w_gate[l]),
                 bf(ffn1_w_up[l]), bf(ffn1_w_down[l]), row(ffn1_post_g[l])).reshape(b, s, d)

        w = bf(w_in[l])
        qt, k, kmean, vt, p_in = _proj(
            x, row(mix_pre_g[l]), w[:, :D_ATTN].T, w[:, D_ATTN:2 * D_ATTN],
            w[:, 2 * D_ATTN:3 * D_ATTN].T, w[:, 3 * D_ATTN:])
        attn = _moba(slopes, qt, k.reshape(b, s, D_ATTN), kmean.reshape(b, -1, D_ATTN), vt)
        x = _mixout(x, attn, p_in, bf(pool_w[l]), row(pool_scale[l]), bf(w_out[l]),
                    row(mix_post_g[l]))

        k_mem, v_mem = _memkv(mem, row(mem_g[l]), bf(xa_wkv[l]))
        x = _xattn(x, row(xa_pre_g[l]), bf(xa_wq[l]), k_mem, v_mem, bf(xa_wo[l]),
                   row(xa_post_g[l]))

        x = _ffn(x.reshape(b * s, d), row(ffn2_pre_g[l]), bf(ffn2_w_gate[l]),
                 bf(ffn2_w_up[l]), bf(ffn2_w_down[l]), row(ffn2_post_g[l])).reshape(b, s, d)
    return x
```

```python
import functools

import numpy as np
import jax
import jax.numpy as jnp
from jax import lax
from jax.experimental import pallas as pl
from jax.experimental.pallas import tpu as pltpu

N_ATTN_HEADS = 8
ATTN_HEAD_DIM = 64
D_ATTN = N_ATTN_HEADS * ATTN_HEAD_DIM
POOL_WINDOWS = (2, 4, 8, 16)
POOL_GROUP_DIM = 128
D_POOL = len(POOL_WINDOWS) * POOL_GROUP_DIM
MOBA_BLOCK = 256
MOBA_TOPK = 3
XA_HEADS = 4
MACARON_WEIGHT = 0.5
RMS_EPS = 1e-6

V7X_LANES = 128
V7X_VMEM_BYTES = 64 * 1024 * 1024
VMEM_LIMIT_BYTES = 56 * 1024 * 1024

TOKEN_TILE = 512
HEADS_PER_STEP = V7X_LANES // ATTN_HEAD_DIM
POOL_HALO = max(POOL_WINDOWS)

_NT = (((1,), (1,)), ((), ()))


def _rmsnorm(x, g):
    r = lax.rsqrt(jnp.mean(x * x, axis=-1, keepdims=True) + RMS_EPS)
    return (x * r) * g


def _dot(a, b):
    return jnp.dot(a, b, preferred_element_type=jnp.float32)


def _params(*semantics):
    return pltpu.CompilerParams(dimension_semantics=semantics,
                                vmem_limit_bytes=VMEM_LIMIT_BYTES)


def _const_spec(shape):
    return pl.BlockSpec(shape, lambda *_: (0,) * len(shape))


def _ffn_kernel(x_ref, pre_g_ref, wg_ref, wu_ref, wd_ref, post_g_ref, o_ref):
    x = x_ref[...]
    h = _rmsnorm(x, pre_g_ref[...]).astype(jnp.bfloat16)
    gate = _dot(h, wg_ref[...])
    up = _dot(h, wu_ref[...])
    act = (gate * jax.nn.sigmoid(gate) * up).astype(jnp.bfloat16)
    f = _dot(act, wd_ref[...])
    o_ref[...] = x + MACARON_WEIGHT * _rmsnorm(f, post_g_ref[...])


def _ffn(x, pre_g, wg, wu, wd, post_g):
    n, d = x.shape
    d_ff = wg.shape[1]
    tile = pl.BlockSpec((TOKEN_TILE, d), lambda i: (i, 0))
    return pl.pallas_call(
        _ffn_kernel,
        grid=(n // TOKEN_TILE,),
        in_specs=[tile, _const_spec((1, d)), _const_spec((d, d_ff)),
                  _const_spec((d, d_ff)), _const_spec((d_ff, d)),
                  _const_spec((1, d))],
        out_specs=tile,
        out_shape=jax.ShapeDtypeStruct((n, d), jnp.float32),
        compiler_params=_params("parallel"),
        name="ffn",
    )(x, pre_g, wg, wu, wd, post_g)


def _proj_kernel(x_ref, g_ref, wqt_ref, wk_ref, wvt_ref, wp_ref,
                 qt_ref, k_ref, kmean_ref, vt_ref, p_ref):
    h = _rmsnorm(x_ref[0], g_ref[...]).astype(jnp.bfloat16)
    qt = lax.dot_general(wqt_ref[...], h, _NT, preferred_element_type=jnp.float32)
    qt = (qt * (ATTN_HEAD_DIM ** -0.5)).astype(jnp.bfloat16)
    k = _dot(h, wk_ref[...])
    vt = lax.dot_general(wvt_ref[...], h, _NT,
                         preferred_element_type=jnp.float32).astype(jnp.bfloat16)
    p_ref[0] = _dot(h, wp_ref[...])
    for blk in range(TOKEN_TILE // MOBA_BLOCK):
        rows = slice(blk * MOBA_BLOCK, (blk + 1) * MOBA_BLOCK)
        qt_ref[0, blk] = qt[:, rows]
        vt_ref[0, blk] = vt[:, rows]
        k_blk = k[rows, :]
        k_ref[0, blk] = k_blk.astype(jnp.bfloat16)
        kmean_ref[0, blk] = jnp.sum(k_blk, axis=0, keepdims=True) * (1.0 / MOBA_BLOCK)


def _proj(x, g, wqt, wk, wvt, wp):
    b, s, d = x.shape
    nb = s // MOBA_BLOCK
    bpt = TOKEN_TILE // MOBA_BLOCK
    return pl.pallas_call(
        _proj_kernel,
        grid=(b, s // TOKEN_TILE),
        in_specs=[pl.BlockSpec((1, TOKEN_TILE, d), lambda bi, t: (bi, t, 0)),
                  _const_spec((1, d)), _const_spec((D_ATTN, d)),
                  _const_spec((d, D_ATTN)), _const_spec((D_ATTN, d)),
                  _const_spec((d, D_POOL))],
        out_specs=[
            pl.BlockSpec((1, bpt, D_ATTN, MOBA_BLOCK), lambda bi, t: (bi, t, 0, 0)),
            pl.BlockSpec((1, bpt, MOBA_BLOCK, D_ATTN), lambda bi, t: (bi, t, 0, 0)),
            pl.BlockSpec((1, bpt, 1, D_ATTN), lambda bi, t: (bi, t, 0, 0)),
            pl.BlockSpec((1, bpt, D_ATTN, MOBA_BLOCK), lambda bi, t: (bi, t, 0, 0)),
            pl.BlockSpec((1, TOKEN_TILE, D_POOL), lambda bi, t: (bi, t, 0)),
        ],
        out_shape=[
            jax.ShapeDtypeStruct((b, nb, D_ATTN, MOBA_BLOCK), jnp.bfloat16),
            jax.ShapeDtypeStruct((b, nb, MOBA_BLOCK, D_ATTN), jnp.bfloat16),
            jax.ShapeDtypeStruct((b, nb, 1, D_ATTN), jnp.float32),
            jax.ShapeDtypeStruct((b, nb, D_ATTN, MOBA_BLOCK), jnp.bfloat16),
            jax.ShapeDtypeStruct((b, s, D_POOL), jnp.float32),
        ],
        compiler_params=_params("parallel", "parallel"),
        name="mixer_proj",
    )(x, g, wqt, wk, wvt, wp)


def _moba_kernel(slopes_ref, qt_ref, k_ref, kmean_ref, vt_ref, o_ref,
                 sel_ref, bias_ref, s0_ref):
    hp = pl.program_id(1)
    i = pl.program_id(2)
    L = MOBA_BLOCK
    nb = vt_ref.shape[1]
    heads = range(HEADS_PER_STEP)
    slopes = [slopes_ref[hp * HEADS_PER_STEP + hh] for hh in heads]
    head_rows = [slice(hh * ATTN_HEAD_DIM, (hh + 1) * ATTN_HEAD_DIM) for hh in heads]

    @pl.when(i == 0)
    def _():
        key = lax.broadcasted_iota(jnp.int32, (2 * L, L), 0)
        query = lax.broadcasted_iota(jnp.int32, (2 * L, L), 1)
        q_minus_k = (query - key).astype(jnp.float32)
        own_first = jnp.where((key < L) & (key > query), jnp.inf, 0.0)
        own_second = jnp.where((key >= L) & (key - L > query), jnp.inf, 0.0)
        for hh in heads:
            base = slopes[hh] * q_minus_k
            bias_ref[hh, 0] = base
            bias_ref[hh, 1] = base + own_first
            bias_ref[hh, 2] = base + own_second

    qt = qt_ref[0, 0]
    kmean = kmean_ref[0].astype(jnp.bfloat16)
    head_row = lax.broadcasted_iota(jnp.int32, qt.shape, 0) // ATTN_HEAD_DIM
    blk_id = lax.broadcasted_iota(jnp.int32, (nb, L), 0)
    qt_heads = [jnp.where(head_row == hh, qt, jnp.zeros_like(qt)) for hh in heads]
    top = i >> 1

    def scores(hh, pair):
        rows = pl.ds(pl.multiple_of(pair * (2 * L), 2 * L), 2 * L)
        return _dot(k_ref[0, rows, :], qt_heads[hh])

    s0_ref[...] = scores(0, top)

    for hh in heads:
        gate = jnp.where(blk_id < i, _dot(kmean, qt_heads[hh]), -jnp.inf)
        rank = jnp.zeros((nb, L), jnp.float32)
        for jp in range(nb):
            row = gate[jp:jp + 1, :]
            ahead = (row > gate) | ((row == gate) & (jp < blk_id))
            rank = rank + jnp.where(ahead, 1.0, 0.0)
        sel = ((rank < MOBA_TOPK) & (blk_id < i)) | (blk_id == i)
        sel_ref[hh] = sel.astype(jnp.float32)

    def attend(hh, s_raw, pair, variant, state):
        m, l, acc = state
        j0 = 2 * pair
        offset = slopes[hh] * ((i - j0) * L).astype(jnp.float32)
        t = s_raw - bias_ref[hh, variant]
        keep = [sel_ref[hh, pl.ds(j0 + blk, 1), :] > 0.5 for blk in range(2)]
        t_blk = [t[:L, :], t[L:, :]]
        m_blk = [jnp.where(keep[blk], jnp.max(t_blk[blk], axis=0, keepdims=True),
                           -jnp.inf) for blk in range(2)]
        m_new = jnp.maximum(m, jnp.maximum(m_blk[0], m_blk[1]) - offset)
        alpha = jnp.exp(m - m_new)
        l = alpha * l
        acc = alpha * acc
        for blk in range(2):
            shift = jnp.where(keep[blk], m_new + offset, jnp.inf)
            p = jnp.exp(t_blk[blk] - shift)
            l = l + jnp.sum(p, axis=0, keepdims=True)
            acc = acc + _dot(vt_ref[0, j0 + blk, head_rows[hh], :], p.astype(jnp.bfloat16))
        return m_new, l, acc

    def visit(n, state, prefetch):
        pair = top - n
        variant = jnp.where(n == 0, 1 + (i & 1), 0)
        s1 = scores(1, pair)
        state0 = attend(0, s0_ref[...], pair, variant, state[0])
        if prefetch:
            s0_ref[...] = scores(0, pair - 1)
        state1 = attend(1, s1, pair, variant, state[1])
        return state0, state1

    init = (jnp.full((1, L), -jnp.inf, jnp.float32), jnp.zeros((1, L), jnp.float32),
            jnp.zeros((ATTN_HEAD_DIM, L), jnp.float32))
    state = lax.fori_loop(0, top, lambda n, st: visit(n, st, True), (init, init))
    state = visit(top, state, False)
    o_t = jnp.concatenate([acc / l for _, l, acc in state], axis=0)
    o_ref[0] = o_t.T.astype(o_ref.dtype)


def _moba(slopes, qt, k, kmean, vt):
    b, nb, _, L = qt.shape
    lanes = HEADS_PER_STEP * ATTN_HEAD_DIM
    return pl.pallas_call(
        _moba_kernel,
        grid=(b, N_ATTN_HEADS // HEADS_PER_STEP, nb),
        in_specs=[
            pl.BlockSpec(memory_space=pltpu.SMEM),
            pl.BlockSpec((1, 1, lanes, L), lambda bi, hp, i: (bi, i, hp, 0)),
            pl.BlockSpec((1, nb * L, lanes), lambda bi, hp, i: (bi, 0, hp)),
            pl.BlockSpec((1, nb, lanes), lambda bi, hp, i: (bi, 0, hp)),
            pl.BlockSpec((1, nb, lanes, L), lambda bi, hp, i: (bi, 0, hp, 0)),
        ],
        out_specs=pl.BlockSpec((1, L, lanes), lambda bi, hp, i: (bi, i, hp)),
        out_shape=jax.ShapeDtypeStruct((b, nb * L, D_ATTN), jnp.bfloat16),
        scratch_shapes=[pltpu.VMEM((HEADS_PER_STEP, nb, L), jnp.float32),
                        pltpu.VMEM((HEADS_PER_STEP, 3, 2 * L, L), jnp.float32),
                        pltpu.VMEM((2 * L, L), jnp.float32)],
        compiler_params=_params("parallel", "parallel", "arbitrary"),
        name="moba_attention",
    )(slopes, qt, k, kmean, vt)


def _mixout_kernel(x_ref, attn_ref, p_ref, halo_ref, pool_w_ref, pool_scale_ref,
                   w_out_ref, g_ref, o_ref):
    t = pl.program_id(1)
    tm = p_ref.shape[1]
    p = p_ref[0]
    halo = jnp.where(t > 0, halo_ref[0], 0.0)
    ext = jnp.concatenate([halo, p], axis=0)
    pos = t * tm + lax.broadcasted_iota(jnp.int32, (tm, 1), 0)
    pooled = []
    for g, w in enumerate(POOL_WINDOWS):
        cols = slice(g * POOL_GROUP_DIM, (g + 1) * POOL_GROUP_DIM)
        acc = ext[:, cols]
        n_valid = acc.shape[0]
        span = 1
        while span < w:
            acc = acc[span:, :] + acc[:n_valid - span, :]
            n_valid -= span
            span *= 2
        win_sum = acc[n_valid - tm:, :]
        cnt = jnp.minimum(pos + 1, w).astype(jnp.float32)
        d = win_sum / cnt - p[:, cols]
        y = _dot(d.astype(jnp.bfloat16), pool_w_ref[g])
        pooled.append(y * pool_scale_ref[:, cols])
    pool = jnp.concatenate(pooled, axis=-1).astype(jnp.bfloat16)
    mixed = jnp.concatenate([attn_ref[0], pool], axis=-1)
    y = _dot(mixed, w_out_ref[...])
    o_ref[0] = x_ref[0] + _rmsnorm(y, g_ref[...])


def _mixout(x, attn, p_in, pool_w, pool_scale, w_out, g):
    b, s, d = x.shape
    halo_blocks = TOKEN_TILE // POOL_HALO
    tile = lambda width: pl.BlockSpec((1, TOKEN_TILE, width), lambda bi, t: (bi, t, 0))
    return pl.pallas_call(
        _mixout_kernel,
        grid=(b, s // TOKEN_TILE),
        in_specs=[
            tile(d), tile(D_ATTN), tile(D_POOL),
            pl.BlockSpec((1, POOL_HALO, D_POOL),
                         lambda bi, t: (bi, jnp.maximum(t * halo_blocks - 1, 0), 0)),
            _const_spec(pool_w.shape), _const_spec((1, D_POOL)),
            _const_spec(w_out.shape), _const_spec((1, d)),
        ],
        out_specs=tile(d),
        out_shape=jax.ShapeDtypeStruct((b, s, d), jnp.float32),
        compiler_params=_params("parallel", "parallel"),
        name="mixer_out",
    )(x, attn, p_in, p_in, pool_w, pool_scale, w_out, g)


def _memkv_kernel(mem_ref, g_ref, wkv_ref, k_ref, v_ref):
    d = mem_ref.shape[2]
    mem_n = _rmsnorm(mem_ref[0], g_ref[...]).astype(jnp.bfloat16)
    kv = _dot(mem_n, wkv_ref[...])
    k_ref[0] = kv[:, :d].astype(jnp.bfloat16)
    v_ref[0] = kv[:, d:].astype(jnp.bfloat16)


def _memkv(mem, g, wkv):
    b, m, d = mem.shape
    blk = pl.BlockSpec((1, m, d), lambda bi: (bi, 0, 0))
    return pl.pallas_call(
        _memkv_kernel,
        grid=(b,),
        in_specs=[blk, _const_spec((1, d)), _const_spec(wkv.shape)],
        out_specs=[blk, blk],
        out_shape=[jax.ShapeDtypeStruct((b, m, d), jnp.bfloat16)] * 2,
        compiler_params=_params("parallel"),
        name="mem_kv",
    )(mem, g, wkv)


def _xattn_kernel(x_ref, pre_g_ref, wq_ref, k_ref, v_ref, wo_ref, post_g_ref, o_ref):
    x = x_ref[0]
    d = x.shape[1]
    dh = d // XA_HEADS
    h = _rmsnorm(x, pre_g_ref[...]).astype(jnp.bfloat16)
    q = _dot(h, wq_ref[...]).astype(jnp.bfloat16)
    heads = []
    for hd in range(XA_HEADS):
        cols = slice(hd * dh, (hd + 1) * dh)
        s = lax.dot_general(q[:, cols], k_ref[0, :, cols], _NT,
                            preferred_element_type=jnp.float32) * (dh ** -0.5)
        e = jnp.exp(s - jnp.max(s, axis=-1, keepdims=True))
        p = e / jnp.sum(e, axis=-1, keepdims=True)
        heads.append(_dot(p.astype(jnp.bfloat16), v_ref[0, :, cols]))
    o = jnp.concatenate(heads, axis=-1).astype(jnp.bfloat16)
    c = _dot(o, wo_ref[...])
    o_ref[0] = x + _rmsnorm(c, post_g_ref[...])


def _xattn(x, pre_g, wq, k_mem, v_mem, wo, post_g):
    b, s, d = x.shape
    m = k_mem.shape[1]
    tile = pl.BlockSpec((1, TOKEN_TILE, d), lambda bi, t: (bi, t, 0))
    mem_blk = pl.BlockSpec((1, m, d), lambda bi, t: (bi, 0, 0))
    return pl.pallas_call(
        _xattn_kernel,
        grid=(b, s // TOKEN_TILE),
        in_specs=[tile, _const_spec((1, d)), _const_spec(wq.shape), mem_blk, mem_blk,
                  _const_spec(wo.shape), _const_spec((1, d))],
        out_specs=tile,
        out_shape=jax.ShapeDtypeStruct((b, s, d), jnp.float32),
        compiler_params=_params("parallel", "parallel"),
        name="mem_xattn",
    )(x, pre_g, wq, k_mem, v_mem, wo, post_g)


def _alibi_slopes(n_heads):
    return jnp.asarray(2.0 ** (-8.0 * np.arange(1, n_heads + 1) / n_heads), jnp.float32)


def kernel(x, mem, ffn1_pre_g, ffn1_w_gate, ffn1_w_up, ffn1_w_down, ffn1_post_g, mix_pre_g, w_in, pool_w, pool_scale, w_out, mix_post_g, xa_pre_g, mem_g, xa_wq, xa_wkv, xa_wo, xa_post_g, ffn2_pre_g, ffn2_w_gate, ffn2_w_up, ffn2_w_down, ffn2_post_g):
    b, s, d = x.shape
    depth = ffn1_pre_g.shape[0]
    bf = lambda w: w.astype(jnp.bfloat16)
    row = lambda g: g.reshape(1, -1)
    slopes = _alibi_slopes(N_ATTN_HEADS)
    for l in range(depth):
        x = _ffn(x.reshape(b * s, d), row(ffn1_pre_g[l]), bf(ffn1_w_gate[l]),
                 bf(ffn1_w_up[l]), bf(ffn1_w_down[l]), row(ffn1_post_g[l])).reshape(b, s, d)

        w = bf(w_in[l])
        qt, k, kmean, vt, p_in = _proj(
            x, row(mix_pre_g[l]), w[:, :D_ATTN].T, w[:, D_ATTN:2 * D_ATTN],
            w[:, 2 * D_ATTN:3 * D_ATTN].T, w[:, 3 * D_ATTN:])
        attn = _moba(slopes, qt, k.reshape(b, s, D_ATTN), kmean.reshape(b, -1, D_ATTN), vt)
        x = _mixout(x, attn, p_in, bf(pool_w[l]), row(pool_scale[l]), bf(w_out[l]),
                    row(mix_post_g[l]))

        k_mem, v_mem = _memkv(mem, row(mem_g[l]), bf(xa_wkv[l]))
        x = _xattn(x, row(xa_pre_g[l]), bf(xa_wq[l]), k_mem, v_mem, bf(xa_wo[l]),
                   row(xa_post_g[l]))

        x = _ffn(x.reshape(b * s, d), row(ffn2_pre_g[l]), bf(ffn2_w_gate[l]),
                 bf(ffn2_w_up[l]), bf(ffn2_w_down[l]), row(ffn2_post_g[l])).reshape(b, s, d)
    return x
```

```python
import functools

import numpy as np
import jax
import jax.numpy as jnp
from jax import lax
from jax.experimental import pallas as pl
from jax.experimental.pallas import tpu as pltpu

N_ATTN_HEADS = 8
ATTN_HEAD_DIM = 64
D_ATTN = N_ATTN_HEADS * ATTN_HEAD_DIM
POOL_WINDOWS = (2, 4, 8, 16)
POOL_GROUP_DIM = 128
D_POOL = len(POOL_WINDOWS) * POOL_GROUP_DIM
MOBA_BLOCK = 256
MOBA_TOPK = 3
XA_HEADS = 4
MACARON_WEIGHT = 0.5
RMS_EPS = 1e-6

V7X_LANES = 128
V7X_VMEM_BYTES = 64 * 1024 * 1024
VMEM_LIMIT_BYTES = 56 * 1024 * 1024

TOKEN_TILE = 512
HEADS_PER_STEP = V7X_LANES // ATTN_HEAD_DIM
POOL_HALO = max(POOL_WINDOWS)

_NT = (((1,), (1,)), ((), ()))


def _rmsnorm(x, g):
    r = lax.rsqrt(jnp.mean(x * x, axis=-1, keepdims=True) + RMS_EPS)
    return (x * r) * g


def _dot(a, b):
    return jnp.dot(a, b, preferred_element_type=jnp.float32)


def _params(*semantics):
    return pltpu.CompilerParams(dimension_semantics=semantics,
                                vmem_limit_bytes=VMEM_LIMIT_BYTES)


def _const_spec(shape):
    return pl.BlockSpec(shape, lambda *_: (0,) * len(shape))


def _ffn_kernel(x_ref, pre_g_ref, wg_ref, wu_ref, wd_ref, post_g_ref, o_ref):
    x = x_ref[...]
    h = _rmsnorm(x, pre_g_ref[...]).astype(jnp.bfloat16)
    gate = _dot(h, wg_ref[...])
    up = _dot(h, wu_ref[...])
    act = (gate * jax.nn.sigmoid(gate) * up).astype(jnp.bfloat16)
    f = _dot(act, wd_ref[...])
    o_ref[...] = x + MACARON_WEIGHT * _rmsnorm(f, post_g_ref[...])


def _ffn(x, pre_g, wg, wu, wd, post_g):
    n, d = x.shape
    d_ff = wg.shape[1]
    tile = pl.BlockSpec((TOKEN_TILE, d), lambda i: (i, 0))
    return pl.pallas_call(
        _ffn_kernel,
        grid=(n // TOKEN_TILE,),
        in_specs=[tile, _const_spec((1, d)), _const_spec((d, d_ff)),
                  _const_spec((d, d_ff)), _const_spec((d_ff, d)),
                  _const_spec((1, d))],
        out_specs=tile,
        out_shape=jax.ShapeDtypeStruct((n, d), jnp.float32),
        compiler_params=_params("parallel"),
        name="ffn",
    )(x, pre_g, wg, wu, wd, post_g)


def _proj_kernel(x_ref, g_ref, wqt_ref, wk_ref, wvt_ref, wp_ref,
                 qt_ref, k_ref, kmean_ref, vt_ref, p_ref):
    h = _rmsnorm(x_ref[0], g_ref[...]).astype(jnp.bfloat16)
    qt = lax.dot_general(wqt_ref[...], h, _NT, preferred_element_type=jnp.float32)
    qt = (qt * (ATTN_HEAD_DIM ** -0.5)).astype(jnp.bfloat16)
    k = _dot(h, wk_ref[...])
    vt = lax.dot_general(wvt_ref[...], h, _NT,
                         preferred_element_type=jnp.float32).astype(jnp.bfloat16)
    p_ref[0] = _dot(h, wp_ref[...])
    for blk in range(TOKEN_TILE // MOBA_BLOCK):
        rows = slice(blk * MOBA_BLOCK, (blk + 1) * MOBA_BLOCK)
        qt_ref[0, blk] = qt[:, rows]
        vt_ref[0, blk] = vt[:, rows]
        k_blk = k[rows, :]
        k_ref[0, blk] = k_blk.astype(jnp.bfloat16)
        kmean_ref[0, blk] = jnp.sum(k_blk, axis=0, keepdims=True) * (1.0 / MOBA_BLOCK)


def _proj(x, g, wqt, wk, wvt, wp):
    b, s, d = x.shape
    nb = s // MOBA_BLOCK
    bpt = TOKEN_TILE // MOBA_BLOCK
    return pl.pallas_call(
        _proj_kernel,
        grid=(b, s // TOKEN_TILE),
        in_specs=[pl.BlockSpec((1, TOKEN_TILE, d), lambda bi, t: (bi, t, 0)),
                  _const_spec((1, d)), _const_spec((D_ATTN, d)),
                  _const_spec((d, D_ATTN)), _const_spec((D_ATTN, d)),
                  _const_spec((d, D_POOL))],
        out_specs=[
            pl.BlockSpec((1, bpt, D_ATTN, MOBA_BLOCK), lambda bi, t: (bi, t, 0, 0)),
            pl.BlockSpec((1, bpt, MOBA_BLOCK, D_ATTN), lambda bi, t: (bi, t, 0, 0)),
            pl.BlockSpec((1, bpt, 1, D_ATTN), lambda bi, t: (bi, t, 0, 0)),
            pl.BlockSpec((1, bpt, D_ATTN, MOBA_BLOCK), lambda bi, t: (bi, t, 0, 0)),
            pl.BlockSpec((1, TOKEN_TILE, D_POOL), lambda bi, t: (bi, t, 0)),
        ],
        out_shape=[
            jax.ShapeDtypeStruct((b, nb, D_ATTN, MOBA_BLOCK), jnp.bfloat16),
            jax.ShapeDtypeStruct((b, nb, MOBA_BLOCK, D_ATTN), jnp.bfloat16),
            jax.ShapeDtypeStruct((b, nb, 1, D_ATTN), jnp.float32),
            jax.ShapeDtypeStruct((b, nb, D_ATTN, MOBA_BLOCK), jnp.bfloat16),
            jax.ShapeDtypeStruct((b, s, D_POOL), jnp.float32),
        ],
        compiler_params=_params("parallel", "parallel"),
        name="mixer_proj",
    )(x, g, wqt, wk, wvt, wp)


MOBA_TILE = 2 * MOBA_BLOCK
ONES_ROWS = 16


def _moba_items(n_tiles):
    tiles, pairs = [], []
    for c in range(n_tiles):
        for pr in range(c, -1, -1):
            tiles.append(c)
            pairs.append(pr)
    return np.asarray(tiles, np.int32), np.asarray(pairs, np.int32)


def _moba_kernel(slopes_ref, tile_tab_ref, pair_tab_ref, qt_ref, k_ref, kmean_ref, vt_ref,
                 o_ref, sel_ref, bias_ref, s_ref, p_ref, acc_ref):
    hp = pl.program_id(1)
    L, T = MOBA_BLOCK, MOBA_TILE
    nb = vt_ref.shape[1]
    n_tiles = nb // 2
    n_items = tile_tab_ref.shape[0]
    heads = range(HEADS_PER_STEP)
    slopes = [slopes_ref[hp * HEADS_PER_STEP + hh] for hh in heads]
    head_rows = [slice(hh * ATTN_HEAD_DIM, (hh + 1) * ATTN_HEAD_DIM) for hh in heads]
    head_row = lax.broadcasted_iota(jnp.int32, (HEADS_PER_STEP * ATTN_HEAD_DIM, T), 0) // ATTN_HEAD_DIM

    def query_tile(c):
        qt = jnp.concatenate([qt_ref[0, 2 * c], qt_ref[0, 2 * c + 1]], axis=1)
        return [jnp.where(head_row == hh, qt, jnp.zeros_like(qt)) for hh in heads]

    key = lax.broadcasted_iota(jnp.int32, (T, T), 0)
    query = lax.broadcasted_iota(jnp.int32, (T, T), 1)
    q_minus_k = (query - key).astype(jnp.float32)
    non_causal = jnp.where(key > query, jnp.inf, 0.0)
    for hh in heads:
        base = slopes[hh] * q_minus_k
        bias_ref[hh, 0] = base
        bias_ref[hh, 1] = base + non_causal

    kmean = kmean_ref[0].astype(jnp.bfloat16)
    blk_id = lax.broadcasted_iota(jnp.int32, (nb, T), 0)
    second_half = lax.broadcasted_iota(jnp.int32, (nb, T), 1) // L

    def select(c, _):
        qt_heads = query_tile(c)
        own = 2 * c + second_half
        for hh in heads:
            gate = jnp.where(blk_id < own, _dot(kmean, qt_heads[hh]), -jnp.inf)
            rank = jnp.zeros((nb, T), jnp.float32)
            for jp in range(nb):
                row = gate[jp:jp + 1, :]
                ahead = (row > gate) | ((row == gate) & (jp < blk_id))
                rank = rank + jnp.where(ahead, 1.0, 0.0)
            sel = ((rank < MOBA_TOPK) & (blk_id < own)) | (blk_id == own)
            sel_ref[hh, c] = sel.astype(jnp.float32)
        return 0

    lax.fori_loop(0, n_tiles, select, 0)

    blk_rows = [slice(blk * L, (blk + 1) * L) for blk in range(2)]
    ones_rows = jnp.ones((ONES_ROWS, L), jnp.bfloat16)

    def biased_scores(e, slot):
        c, pr = tile_tab_ref[e], pair_tab_ref[e]
        variant = jnp.where(pr == c, 1, 0)
        k_pair = k_ref[0, pl.ds(pl.multiple_of(pr * T, T), T), :]
        col_max = []
        for hh, qt_h in enumerate(query_tile(c)):
            t = _dot(k_pair, qt_h) - bias_ref[hh, variant]
            s_ref[slot, hh] = t
            col_max.append(tuple(jnp.max(t[rows, :], axis=0, keepdims=True) for rows in blk_rows))
        return tuple(col_max)

    def value_product(pr, slot):
        out = []
        for hh in heads:
            lhs = [jnp.concatenate([vt_ref[0, 2 * pr + blk, head_rows[hh], :], ones_rows], axis=0)
                   for blk in range(2)]
            out.append(_dot(lhs[0], p_ref[slot, hh, blk_rows[0], :])
                       + _dot(lhs[1], p_ref[slot, hh, blk_rows[1], :]))
        return out

    def denominator(hh):
        d = acc_ref[hh, ATTN_HEAD_DIM:ATTN_HEAD_DIM + 1, :]
        return jnp.where(d == 0.0, 1.0, d)

    def write_tile(c):
        o_t = jnp.concatenate(
            [acc_ref[hh, :ATTN_HEAD_DIM, :] / denominator(hh) for hh in heads], axis=0)
        o_ref[0, pl.ds(pl.multiple_of(c * T, T), T), :] = o_t.T.astype(o_ref.dtype)

    def item(e, slot, state):
        m, alpha, col_max = state
        other = 1 - slot
        c, pr = tile_tab_ref[e], pair_tab_ref[e]
        e_prev = jnp.maximum(e - 1, 0)
        c_prev, pr_prev = tile_tab_ref[e_prev], pair_tab_ref[e_prev]

        pv_prev = value_product(pr_prev, other)
        col_max_next = biased_scores(jnp.minimum(e + 1, n_items - 1), other)

        opens = pr == c
        m_out, alpha_out = [], []
        for hh in heads:
            m_old = jnp.where(opens, -jnp.inf, m[hh])
            offset = slopes[hh] * ((c - pr) * T).astype(jnp.float32)
            keep = [sel_ref[hh, c, pl.ds(2 * pr + blk, 1), :] > 0.5 for blk in range(2)]
            m_blk = [jnp.where(keep[blk], col_max[hh][blk], -jnp.inf) for blk in range(2)]
            m_new = jnp.maximum(m_old, jnp.maximum(m_blk[0], m_blk[1]) - offset)
            for blk in range(2):
                shift = jnp.where(keep[blk], m_new + offset, jnp.inf)
                p = jnp.exp(s_ref[slot, hh, blk_rows[blk], :] - shift)
                p_ref[slot, hh, blk_rows[blk], :] = p.astype(jnp.bfloat16)
            m_out.append(m_new)
            alpha_out.append(jnp.exp(m_old - m_new))

        for hh in heads:
            acc_ref[hh] = alpha[hh] * acc_ref[hh] + pv_prev[hh]

        write_tile(c_prev)

        return tuple(m_out), tuple(alpha_out), col_max_next

    col_max = biased_scores(0, 0)
    for hh in heads:
        acc_ref[hh] = jnp.zeros(acc_ref.shape[1:], jnp.float32)
    p_ref[1] = jnp.zeros(p_ref.shape[1:], p_ref.dtype)
    row = lambda v: tuple(jnp.full((1, T), v, jnp.float32) for _ in heads)
    state = (row(-jnp.inf), row(0.0), col_max)

    def two_items(it, state):
        state = item(2 * it, 0, state)
        return item(2 * it + 1, 1, state)

    _, alpha, _ = lax.fori_loop(0, n_items // 2, two_items, state)

    last = n_items - 1
    pv_last = value_product(pair_tab_ref[last], last % 2)
    for hh in heads:
        acc_ref[hh] = alpha[hh] * acc_ref[hh] + pv_last[hh]
    write_tile(tile_tab_ref[last])


def _moba(slopes, qt, k, kmean, vt):
    b, nb, _, L = qt.shape
    lanes = HEADS_PER_STEP * ATTN_HEAD_DIM
    T = MOBA_TILE
    tile_tab, pair_tab = _moba_items(nb // 2)
    assert len(tile_tab) % 2 == 0
    smem = pl.BlockSpec(memory_space=pltpu.SMEM)
    return pl.pallas_call(
        _moba_kernel,
        grid=(b, N_ATTN_HEADS // HEADS_PER_STEP),
        in_specs=[
            smem, smem, smem,
            pl.BlockSpec((1, nb, lanes, L), lambda bi, hp: (bi, 0, hp, 0)),
            pl.BlockSpec((1, nb * L, lanes), lambda bi, hp: (bi, 0, hp)),
            pl.BlockSpec((1, nb, lanes), lambda bi, hp: (bi, 0, hp)),
            pl.BlockSpec((1, nb, lanes, L), lambda bi, hp: (bi, 0, hp, 0)),
        ],
        out_specs=pl.BlockSpec((1, nb * L, lanes), lambda bi, hp: (bi, 0, hp)),
        out_shape=jax.ShapeDtypeStruct((b, nb * L, D_ATTN), jnp.bfloat16),
        scratch_shapes=[
            pltpu.VMEM((HEADS_PER_STEP, nb // 2, nb, T), jnp.float32),
            pltpu.VMEM((HEADS_PER_STEP, 2, T, T), jnp.float32),
            pltpu.VMEM((2, HEADS_PER_STEP, T, T), jnp.float32),
            pltpu.VMEM((2, HEADS_PER_STEP, T, T), jnp.bfloat16),
            pltpu.VMEM((HEADS_PER_STEP, ATTN_HEAD_DIM + ONES_ROWS, T), jnp.float32),
        ],
        compiler_params=_params("parallel", "parallel"),
        name="moba_attention",
    )(slopes, jnp.asarray(tile_tab), jnp.asarray(pair_tab), qt, k, kmean, vt)


def _mixout_kernel(x_ref, attn_ref, p_ref, halo_ref, pool_w_ref, pool_scale_ref,
                   w_out_ref, g_ref, o_ref):
    t = pl.program_id(1)
    tm = p_ref.shape[1]
    p = p_ref[0]
    halo = jnp.where(t > 0, halo_ref[0], 0.0)
    ext = jnp.concatenate([halo, p], axis=0)
    pos = t * tm + lax.broadcasted_iota(jnp.int32, (tm, 1), 0)
    pooled = []
    for g, w in enumerate(POOL_WINDOWS):
        cols = slice(g * POOL_GROUP_DIM, (g + 1) * POOL_GROUP_DIM)
        acc = ext[:, cols]
        n_valid = acc.shape[0]
        span = 1
        while span < w:
            acc = acc[span:, :] + acc[:n_valid - span, :]
            n_valid -= span
            span *= 2
        win_sum = acc[n_valid - tm:, :]
        cnt = jnp.minimum(pos + 1, w).astype(jnp.float32)
        d = win_sum / cnt - p[:, cols]
        y = _dot(d.astype(jnp.bfloat16), pool_w_ref[g])
        pooled.append(y * pool_scale_ref[:, cols])
    pool = jnp.concatenate(pooled, axis=-1).astype(jnp.bfloat16)
    mixed = jnp.concatenate([attn_ref[0], pool], axis=-1)
    y = _dot(mixed, w_out_ref[...])
    o_ref[0] = x_ref[0] + _rmsnorm(y, g_ref[...])


def _mixout(x, attn, p_in, pool_w, pool_scale, w_out, g):
    b, s, d = x.shape
    halo_blocks = TOKEN_TILE // POOL_HALO
    tile = lambda width: pl.BlockSpec((1, TOKEN_TILE, width), lambda bi, t: (bi, t, 0))
    return pl.pallas_call(
        _mixout_kernel,
        grid=(b, s // TOKEN_TILE),
        in_specs=[
            tile(d), tile(D_ATTN), tile(D_POOL),
            pl.BlockSpec((1, POOL_HALO, D_POOL),
                         lambda bi, t: (bi, jnp.maximum(t * halo_blocks - 1, 0), 0)),
            _const_spec(pool_w.shape), _const_spec((1, D_POOL)),
            _const_spec(w_out.shape), _const_spec((1, d)),
        ],
        out_specs=tile(d),
        out_shape=jax.ShapeDtypeStruct((b, s, d), jnp.float32),
        compiler_params=_params("parallel", "parallel"),
        name="mixer_out",
    )(x, attn, p_in, p_in, pool_w, pool_scale, w_out, g)


def _memkv_kernel(mem_ref, g_ref, wkv_ref, k_ref, v_ref):
    d = mem_ref.shape[2]
    mem_n = _rmsnorm(mem_ref[0], g_ref[...]).astype(jnp.bfloat16)
    kv = _dot(mem_n, wkv_ref[...])
    k_ref[0] = kv[:, :d].astype(jnp.bfloat16)
    v_ref[0] = kv[:, d:].astype(jnp.bfloat16)


def _memkv(mem, g, wkv):
    b, m, d = mem.shape
    blk = pl.BlockSpec((1, m, d), lambda bi: (bi, 0, 0))
    return pl.pallas_call(
        _memkv_kernel,
        grid=(b,),
        in_specs=[blk, _const_spec((1, d)), _const_spec(wkv.shape)],
        out_specs=[blk, blk],
        out_shape=[jax.ShapeDtypeStruct((b, m, d), jnp.bfloat16)] * 2,
        compiler_params=_params("parallel"),
        name="mem_kv",
    )(mem, g, wkv)


def _xattn_kernel(x_ref, pre_g_ref, wq_ref, k_ref, v_ref, wo_ref, post_g_ref, o_ref):
    x = x_ref[0]
    d = x.shape[1]
    dh = d // XA_HEADS
    h = _rmsnorm(x, pre_g_ref[...]).astype(jnp.bfloat16)
    q = _dot(h, wq_ref[...]).astype(jnp.bfloat16)
    heads = []
    for hd in range(XA_HEADS):
        cols = slice(hd * dh, (hd + 1) * dh)
        s = lax.dot_general(q[:, cols], k_ref[0, :, cols], _NT,
                            preferred_element_type=jnp.float32) * (dh ** -0.5)
        e = jnp.exp(s - jnp.max(s, axis=-1, keepdims=True))
        p = e / jnp.sum(e, axis=-1, keepdims=True)
        heads.append(_dot(p.astype(jnp.bfloat16), v_ref[0, :, cols]))
    o = jnp.concatenate(heads, axis=-1).astype(jnp.bfloat16)
    c = _dot(o, wo_ref[...])
    o_ref[0] = x + _rmsnorm(c, post_g_ref[...])


def _xattn(x, pre_g, wq, k_mem, v_mem, wo, post_g):
    b, s, d = x.shape
    m = k_mem.shape[1]
    tile = pl.BlockSpec((1, TOKEN_TILE, d), lambda bi, t: (bi, t, 0))
    mem_blk = pl.BlockSpec((1, m, d), lambda bi, t: (bi, 0, 0))
    return pl.pallas_call(
        _xattn_kernel,
        grid=(b, s // TOKEN_TILE),
        in_specs=[tile, _const_spec((1, d)), _const_spec(wq.shape), mem_blk, mem_blk,
                  _const_spec(wo.shape), _const_spec((1, d))],
        out_specs=tile,
        out_shape=jax.ShapeDtypeStruct((b, s, d), jnp.float32),
        compiler_params=_params("parallel", "parallel"),
        name="mem_xattn",
    )(x, pre_g, wq, k_mem, v_mem, wo, post_g)


def _alibi_slopes(n_heads):
    return jnp.asarray(2.0 ** (-8.0 * np.arange(1, n_heads + 1) / n_heads), jnp.float32)


def kernel(x, mem, ffn1_pre_g, ffn1_w_gate, ffn1_w_up, ffn1_w_down, ffn1_post_g, mix_pre_g, w_in, pool_w, pool_scale, w_out, mix_post_g, xa_pre_g, mem_g, xa_wq, xa_wkv, xa_wo, xa_post_g, ffn2_pre_g, ffn2_w_gate, ffn2_w_up, ffn2_w_down, ffn2_post_g):
    b, s, d = x.shape
    depth = ffn1_pre_g.shape[0]
    bf = lambda w: w.astype(jnp.bfloat16)
    row = lambda g: g.reshape(1, -1)
    slopes = _alibi_slopes(N_ATTN_HEADS)
    for l in range(depth):
        x = _ffn(x.reshape(b * s, d), row(ffn1_pre_g[l]), bf(ffn1_w_gate[l]),
                 bf(ffn1_w_up[l]), bf(ffn1_w_down[l]), row(ffn1_post_g[l])).reshape(b, s, d)

        w = bf(w_in[l])
        qt, k, kmean, vt, p_in = _proj(
            x, row(mix_pre_g[l]), w[:, :D_ATTN].T, w[:, D_ATTN:2 * D_ATTN],
            w[:, 2 * D_ATTN:3 * D_ATTN].T, w[:, 3 * D_ATTN:])
        attn = _moba(slopes, qt, k.reshape(b, s, D_ATTN), kmean.reshape(b, -1, D_ATTN), vt)
        x = _mixout(x, attn, p_in, bf(pool_w[l]), row(pool_scale[l]), bf(w_out[l]),
                    row(mix_post_g[l]))

        k_mem, v_mem = _memkv(mem, row(mem_g[l]), bf(xa_wkv[l]))
        x = _xattn(x, row(xa_pre_g[l]), bf(xa_wq[l]), k_mem, v_mem, bf(xa_wo[l]),
                   row(xa_post_g[l]))

        x = _ffn(x.reshape(b * s, d), row(ffn2_pre_g[l]), bf(ffn2_w_gate[l]),
                 bf(ffn2_w_up[l]), bf(ffn2_w_down[l]), row(ffn2_post_g[l])).reshape(b, s, d)
    return x
```

```python
import functools

import numpy as np
import jax
import jax.numpy as jnp
from jax import lax
from jax.experimental import pallas as pl
from jax.experimental.pallas import tpu as pltpu

N_ATTN_HEADS = 8
ATTN_HEAD_DIM = 64
D_ATTN = N_ATTN_HEADS * ATTN_HEAD_DIM
POOL_WINDOWS = (2, 4, 8, 16)
POOL_GROUP_DIM = 128
D_POOL = len(POOL_WINDOWS) * POOL_GROUP_DIM
MOBA_BLOCK = 256
MOBA_TOPK = 3
XA_HEADS = 4
MACARON_WEIGHT = 0.5
RMS_EPS = 1e-6
LOG2_E = 1.4426950408889634

V7X_LANES = 128
V7X_VMEM_BYTES = 64 * 1024 * 1024
VMEM_LIMIT_BYTES = 56 * 1024 * 1024

TOKEN_TILE = 512
HEADS_PER_STEP = V7X_LANES // ATTN_HEAD_DIM
MIX_TOKEN_TILE = 1024
MIX_SUB_TILE = 256
FFN_SUB_TILE = 256
POOL_HALO = max(POOL_WINDOWS)

_NT = (((1,), (1,)), ((), ()))


def _rmsnorm(x, g):
    r = lax.rsqrt(jnp.mean(x * x, axis=-1, keepdims=True) + RMS_EPS)
    return (x * r) * g


def _dot(a, b):
    return jnp.dot(a, b, preferred_element_type=jnp.float32)


def _params(*semantics):
    return pltpu.CompilerParams(dimension_semantics=semantics,
                                vmem_limit_bytes=VMEM_LIMIT_BYTES)


def _const_spec(shape):
    return pl.BlockSpec(shape, lambda *_: (0,) * len(shape))


def _ffn_kernel(x_ref, pre_g_ref, wg_ref, wu_ref, wd_ref, post_g_ref, o_ref):
    subs = [slice(i * FFN_SUB_TILE, (i + 1) * FFN_SUB_TILE)
            for i in range(x_ref.shape[0] // FFN_SUB_TILE)]
    xs = [x_ref[rows, :] for rows in subs]
    hs = [_rmsnorm(x, pre_g_ref[...]).astype(jnp.bfloat16) for x in xs]
    gates = [_dot(h, wg_ref[...]) for h in hs]
    ups = [_dot(h, wu_ref[...]) for h in hs]
    acts = [(g * jax.nn.sigmoid(g) * u).astype(jnp.bfloat16) for g, u in zip(gates, ups)]
    fs = [_dot(a, wd_ref[...]) for a in acts]
    for rows, x, f in zip(subs, xs, fs):
        o_ref[rows, :] = x + MACARON_WEIGHT * _rmsnorm(f, post_g_ref[...])


def _ffn(x, pre_g, wg, wu, wd, post_g):
    n, d = x.shape
    d_ff = wg.shape[1]
    tile = pl.BlockSpec((TOKEN_TILE, d), lambda i: (i, 0))
    return pl.pallas_call(
        _ffn_kernel,
        grid=(n // TOKEN_TILE,),
        in_specs=[tile, _const_spec((1, d)), _const_spec((d, d_ff)),
                  _const_spec((d, d_ff)), _const_spec((d_ff, d)),
                  _const_spec((1, d))],
        out_specs=tile,
        out_shape=jax.ShapeDtypeStruct((n, d), jnp.float32),
        compiler_params=_params("parallel"),
        name="ffn",
    )(x, pre_g, wg, wu, wd, post_g)


def _proj_kernel(x_ref, g_ref, wqt_ref, wk_ref, wvt_ref, wp_ref,
                 qt_ref, k_ref, kmean_ref, vt_ref, p_ref):
    h = _rmsnorm(x_ref[0], g_ref[...]).astype(jnp.bfloat16)
    qt = lax.dot_general(wqt_ref[...], h, _NT, preferred_element_type=jnp.float32)
    qt = (qt * (ATTN_HEAD_DIM ** -0.5 * LOG2_E)).astype(jnp.bfloat16)
    k = _dot(h, wk_ref[...])
    vt = lax.dot_general(wvt_ref[...], h, _NT,
                         preferred_element_type=jnp.float32).astype(jnp.bfloat16)
    p_ref[0] = _dot(h, wp_ref[...])
    for blk in range(TOKEN_TILE // MOBA_BLOCK):
        rows = slice(blk * MOBA_BLOCK, (blk + 1) * MOBA_BLOCK)
        qt_ref[0, blk] = qt[:, rows]
        vt_ref[0, blk] = vt[:, rows]
        k_blk = k[rows, :]
        k_ref[0, blk] = k_blk.astype(jnp.bfloat16)
        kmean_ref[0, blk] = jnp.sum(k_blk, axis=0, keepdims=True) * (1.0 / MOBA_BLOCK)


def _proj(x, g, wqt, wk, wvt, wp):
    b, s, d = x.shape
    nb = s // MOBA_BLOCK
    bpt = TOKEN_TILE // MOBA_BLOCK
    return pl.pallas_call(
        _proj_kernel,
        grid=(b, s // TOKEN_TILE),
        in_specs=[pl.BlockSpec((1, TOKEN_TILE, d), lambda bi, t: (bi, t, 0)),
                  _const_spec((1, d)), _const_spec((D_ATTN, d)),
                  _const_spec((d, D_ATTN)), _const_spec((D_ATTN, d)),
                  _const_spec((d, D_POOL))],
        out_specs=[
            pl.BlockSpec((1, bpt, D_ATTN, MOBA_BLOCK), lambda bi, t: (bi, t, 0, 0)),
            pl.BlockSpec((1, bpt, MOBA_BLOCK, D_ATTN), lambda bi, t: (bi, t, 0, 0)),
            pl.BlockSpec((1, bpt, 1, D_ATTN), lambda bi, t: (bi, t, 0, 0)),
            pl.BlockSpec((1, bpt, D_ATTN, MOBA_BLOCK), lambda bi, t: (bi, t, 0, 0)),
            pl.BlockSpec((1, TOKEN_TILE, D_POOL), lambda bi, t: (bi, t, 0)),
        ],
        out_shape=[
            jax.ShapeDtypeStruct((b, nb, D_ATTN, MOBA_BLOCK), jnp.bfloat16),
            jax.ShapeDtypeStruct((b, nb, MOBA_BLOCK, D_ATTN), jnp.bfloat16),
            jax.ShapeDtypeStruct((b, nb, 1, D_ATTN), jnp.float32),
            jax.ShapeDtypeStruct((b, nb, D_ATTN, MOBA_BLOCK), jnp.bfloat16),
            jax.ShapeDtypeStruct((b, s, D_POOL), jnp.float32),
        ],
        compiler_params=_params("parallel", "parallel"),
        name="mixer_proj",
    )(x, g, wqt, wk, wvt, wp)


MOBA_TILE = 2 * MOBA_BLOCK
ONES_ROWS = 16


def _moba_items(n_tiles):
    tiles, pairs = [], []
    for c in range(n_tiles):
        for pr in range(c, -1, -1):
            tiles.append(c)
            pairs.append(pr)
    return np.asarray(tiles, np.int32), np.asarray(pairs, np.int32)


def _moba_kernel(slopes_ref, tile_tab_ref, pair_tab_ref, qt_ref, k_ref, kmean_ref, vt_ref,
                 o_ref, sel_ref, bias_ref, s_ref, p_ref, acc_ref):
    hp = pl.program_id(1)
    L, T = MOBA_BLOCK, MOBA_TILE
    nb = vt_ref.shape[1]
    n_tiles = nb // 2
    n_items = tile_tab_ref.shape[0]
    heads = range(HEADS_PER_STEP)
    slopes = [slopes_ref[hp * HEADS_PER_STEP + hh] for hh in heads]
    head_rows = [slice(hh * ATTN_HEAD_DIM, (hh + 1) * ATTN_HEAD_DIM) for hh in heads]
    head_row = lax.broadcasted_iota(jnp.int32, (HEADS_PER_STEP * ATTN_HEAD_DIM, T), 0) // ATTN_HEAD_DIM

    def query_tile(c):
        qt = jnp.concatenate([qt_ref[0, 2 * c], qt_ref[0, 2 * c + 1]], axis=1)
        return [jnp.where(head_row == hh, qt, jnp.zeros_like(qt)) for hh in heads]

    key = lax.broadcasted_iota(jnp.int32, (T, T), 0)
    query = lax.broadcasted_iota(jnp.int32, (T, T), 1)
    q_minus_k = (query - key).astype(jnp.float32)
    non_causal = jnp.where(key > query, jnp.inf, 0.0)
    for hh in heads:
        base = slopes[hh] * q_minus_k
        bias_ref[hh, 0] = base
        bias_ref[hh, 1] = base + non_causal

    kmean = kmean_ref[0].astype(jnp.bfloat16)
    blk_id = lax.broadcasted_iota(jnp.int32, (nb, T), 0)
    second_half = lax.broadcasted_iota(jnp.int32, (nb, T), 1) // L

    def select(c, _):
        qt_heads = query_tile(c)
        own = 2 * c + second_half
        for hh in heads:
            gate = jnp.where(blk_id < own, _dot(kmean, qt_heads[hh]), -jnp.inf)
            rank = jnp.zeros((nb, T), jnp.float32)
            for jp in range(nb):
                row = gate[jp:jp + 1, :]
                ahead = (row > gate) | ((row == gate) & (jp < blk_id))
                rank = rank + jnp.where(ahead, 1.0, 0.0)
            sel = ((rank < MOBA_TOPK) & (blk_id < own)) | (blk_id == own)
            sel_ref[hh, c] = sel.astype(jnp.float32)
        return 0

    lax.fori_loop(0, n_tiles, select, 0)

    blk_rows = [slice(blk * L, (blk + 1) * L) for blk in range(2)]
    ones_rows = jnp.ones((ONES_ROWS, L), jnp.bfloat16)

    def biased_scores(e, slot):
        c, pr = tile_tab_ref[e], pair_tab_ref[e]
        variant = jnp.where(pr == c, 1, 0)
        k_pair = k_ref[0, pl.ds(pl.multiple_of(pr * T, T), T), :]
        col_max = []
        for hh, qt_h in enumerate(query_tile(c)):
            t = _dot(k_pair, qt_h) - bias_ref[hh, variant]
            s_ref[slot, hh] = t
            col_max.append(tuple(jnp.max(t[rows, :], axis=0, keepdims=True) for rows in blk_rows))
        return tuple(col_max)

    def value_product(pr, slot):
        out = []
        for hh in heads:
            lhs = [jnp.concatenate([vt_ref[0, 2 * pr + blk, head_rows[hh], :], ones_rows], axis=0)
                   for blk in range(2)]
            out.append(_dot(lhs[0], p_ref[slot, hh, blk_rows[0], :])
                       + _dot(lhs[1], p_ref[slot, hh, blk_rows[1], :]))
        return out

    def denominator(hh):
        d = acc_ref[hh, ATTN_HEAD_DIM:ATTN_HEAD_DIM + 1, :]
        return jnp.where(d == 0.0, 1.0, d)

    def write_tile(c):
        o_t = jnp.concatenate(
            [acc_ref[hh, :ATTN_HEAD_DIM, :] / denominator(hh) for hh in heads], axis=0)
        o_ref[0, pl.ds(pl.multiple_of(c * T, T), T), :] = o_t.T.astype(o_ref.dtype)

    def item(e, slot, state):
        m, alpha, col_max = state
        other = 1 - slot
        c, pr = tile_tab_ref[e], pair_tab_ref[e]
        e_prev = jnp.maximum(e - 1, 0)
        c_prev, pr_prev = tile_tab_ref[e_prev], pair_tab_ref[e_prev]

        pv_prev = value_product(pr_prev, other)
        col_max_next = biased_scores(jnp.minimum(e + 1, n_items - 1), other)

        opens = pr == c
        m_out, alpha_out = [], []
        for hh in heads:
            m_old = jnp.where(opens, -jnp.inf, m[hh])
            offset = slopes[hh] * ((c - pr) * T).astype(jnp.float32)
            keep = [sel_ref[hh, c, pl.ds(2 * pr + blk, 1), :] > 0.5 for blk in range(2)]
            m_blk = [jnp.where(keep[blk], col_max[hh][blk], -jnp.inf) for blk in range(2)]
            m_new = jnp.maximum(m_old, jnp.maximum(m_blk[0], m_blk[1]) - offset)
            for blk in range(2):
                shift = jnp.where(keep[blk], m_new + offset, jnp.inf)
                p = jnp.exp2(s_ref[slot, hh, blk_rows[blk], :] - shift)
                p_ref[slot, hh, blk_rows[blk], :] = p.astype(jnp.bfloat16)
            m_out.append(m_new)
            alpha_out.append(jnp.exp2(m_old - m_new))

        for hh in heads:
            acc_ref[hh] = alpha[hh] * acc_ref[hh] + pv_prev[hh]

        write_tile(c_prev)

        return tuple(m_out), tuple(alpha_out), col_max_next

    col_max = biased_scores(0, 0)
    for hh in heads:
        acc_ref[hh] = jnp.zeros(acc_ref.shape[1:], jnp.float32)
    p_ref[1] = jnp.zeros(p_ref.shape[1:], p_ref.dtype)
    row = lambda v: tuple(jnp.full((1, T), v, jnp.float32) for _ in heads)
    state = (row(-jnp.inf), row(0.0), col_max)

    def two_items(it, state):
        state = item(2 * it, 0, state)
        return item(2 * it + 1, 1, state)

    _, alpha, _ = lax.fori_loop(0, n_items // 2, two_items, state)

    last = n_items - 1
    pv_last = value_product(pair_tab_ref[last], last % 2)
    for hh in heads:
        acc_ref[hh] = alpha[hh] * acc_ref[hh] + pv_last[hh]
    write_tile(tile_tab_ref[last])


def _moba(slopes, qt, k, kmean, vt):
    b, nb, _, L = qt.shape
    lanes = HEADS_PER_STEP * ATTN_HEAD_DIM
    T = MOBA_TILE
    tile_tab, pair_tab = _moba_items(nb // 2)
    assert len(tile_tab) % 2 == 0
    smem = pl.BlockSpec(memory_space=pltpu.SMEM)
    return pl.pallas_call(
        _moba_kernel,
        grid=(b, N_ATTN_HEADS // HEADS_PER_STEP),
        in_specs=[
            smem, smem, smem,
            pl.BlockSpec((1, nb, lanes, L), lambda bi, hp: (bi, 0, hp, 0)),
            pl.BlockSpec((1, nb * L, lanes), lambda bi, hp: (bi, 0, hp)),
            pl.BlockSpec((1, nb, lanes), lambda bi, hp: (bi, 0, hp)),
            pl.BlockSpec((1, nb, lanes, L), lambda bi, hp: (bi, 0, hp, 0)),
        ],
        out_specs=pl.BlockSpec((1, nb * L, lanes), lambda bi, hp: (bi, 0, hp)),
        out_shape=jax.ShapeDtypeStruct((b, nb * L, D_ATTN), jnp.bfloat16),
        scratch_shapes=[
            pltpu.VMEM((HEADS_PER_STEP, nb // 2, nb, T), jnp.float32),
            pltpu.VMEM((HEADS_PER_STEP, 2, T, T), jnp.float32),
            pltpu.VMEM((2, HEADS_PER_STEP, T, T), jnp.float32),
            pltpu.VMEM((2, HEADS_PER_STEP, T, T), jnp.bfloat16),
            pltpu.VMEM((HEADS_PER_STEP, ATTN_HEAD_DIM + ONES_ROWS, T), jnp.float32),
        ],
        compiler_params=_params("parallel", "parallel"),
        name="moba_attention",
    )(slopes, jnp.asarray(tile_tab), jnp.asarray(pair_tab), qt, k, kmean, vt)


def _memkv_kernel(mem_ref, g_ref, wkv_ref, k_ref, v_ref):
    d = mem_ref.shape[2]
    mem_n = _rmsnorm(mem_ref[0], g_ref[...]).astype(jnp.bfloat16)
    kv = _dot(mem_n, wkv_ref[...])
    k_ref[0] = kv[:, :d].astype(jnp.bfloat16)
    v_ref[0] = kv[:, d:].astype(jnp.bfloat16)


def _memkv(mem, g, wkv):
    b, m, d = mem.shape
    blk = pl.BlockSpec((1, m, d), lambda bi: (bi, 0, 0))
    return pl.pallas_call(
        _memkv_kernel,
        grid=(b,),
        in_specs=[blk, _const_spec((1, d)), _const_spec(wkv.shape)],
        out_specs=[blk, blk],
        out_shape=[jax.ShapeDtypeStruct((b, m, d), jnp.bfloat16)] * 2,
        compiler_params=_params("parallel"),
        name="mem_kv",
    )(mem, g, wkv)


def _pooling_mixer(ext, p, first_pos, pool_w_ref, pool_scale_ref):
    rows = p.shape[0]
    pos = first_pos + lax.broadcasted_iota(jnp.int32, (rows, 1), 0)
    pooled = []
    for g, w in enumerate(POOL_WINDOWS):
        cols = slice(g * POOL_GROUP_DIM, (g + 1) * POOL_GROUP_DIM)
        acc = ext[:, cols]
        n_valid = acc.shape[0]
        span = 1
        while span < w:
            acc = acc[span:, :] + acc[:n_valid - span, :]
            n_valid -= span
            span *= 2
        win_sum = acc[n_valid - rows:, :]
        cnt = jnp.minimum(pos + 1, w).astype(jnp.float32)
        d = win_sum / cnt - p[:, cols]
        y = _dot(d.astype(jnp.bfloat16), pool_w_ref[g])
        pooled.append(y * pool_scale_ref[:, cols])
    return jnp.concatenate(pooled, axis=-1).astype(jnp.bfloat16)


def _mix_xattn_kernel(x_ref, attn_ref, p_ref, halo_ref, pool_w_ref, pool_scale_ref,
                      w_out_ref, mix_g_ref, pre_g_ref, wq_ref, k_ref, v_ref, wo_ref,
                      post_g_ref, o_ref):
    t = pl.program_id(1)
    tm, d = x_ref.shape[1], x_ref.shape[2]
    dh = d // XA_HEADS
    subs = [slice(i * MIX_SUB_TILE, (i + 1) * MIX_SUB_TILE) for i in range(tm // MIX_SUB_TILE)]
    halo = jnp.where(t > 0, halo_ref[0], 0.0)
    ext = jnp.concatenate([halo, p_ref[0]], axis=0)
    pools = [_pooling_mixer(ext[rows.start:rows.stop + POOL_HALO, :], p_ref[0, rows, :],
                            t * tm + rows.start, pool_w_ref, pool_scale_ref) for rows in subs]
    ys = [_dot(jnp.concatenate([attn_ref[0, rows, :], pool], axis=-1), w_out_ref[...])
          for rows, pool in zip(subs, pools)]
    xs = [x_ref[0, rows, :] + _rmsnorm(y, mix_g_ref[...]) for rows, y in zip(subs, ys)]

    hs = [_rmsnorm(x, pre_g_ref[...]).astype(jnp.bfloat16) for x in xs]
    qs = [_dot(h, wq_ref[...]).astype(jnp.bfloat16) for h in hs]
    heads = [[] for _ in subs]
    for hd in range(XA_HEADS):
        cols = slice(hd * dh, (hd + 1) * dh)
        ss = [lax.dot_general(q[:, cols], k_ref[0, :, cols], _NT,
                              preferred_element_type=jnp.float32) * (dh ** -0.5) for q in qs]
        for i, s in enumerate(ss):
            e = jnp.exp(s - jnp.max(s, axis=-1, keepdims=True))
            p = e / jnp.sum(e, axis=-1, keepdims=True)
            heads[i].append(_dot(p.astype(jnp.bfloat16), v_ref[0, :, cols]))
    os = [jnp.concatenate(hl, axis=-1).astype(jnp.bfloat16) for hl in heads]
    cs = [_dot(o, wo_ref[...]) for o in os]
    for rows, x, c in zip(subs, xs, cs):
        o_ref[0, rows, :] = x + _rmsnorm(c, post_g_ref[...])


def _mix_xattn(x, attn, p_in, pool_w, pool_scale, w_out, mix_g, pre_g, wq, k_mem, v_mem, wo,
               post_g):
    b, s, d = x.shape
    m = k_mem.shape[1]
    halo_blocks = MIX_TOKEN_TILE // POOL_HALO
    tile = lambda width: pl.BlockSpec((1, MIX_TOKEN_TILE, width), lambda bi, t: (bi, t, 0))
    mem_blk = pl.BlockSpec((1, m, d), lambda bi, t: (bi, 0, 0))
    return pl.pallas_call(
        _mix_xattn_kernel,
        grid=(b, s // MIX_TOKEN_TILE),
        in_specs=[
            tile(d), tile(D_ATTN), tile(D_POOL),
            pl.BlockSpec((1, POOL_HALO, D_POOL),
                         lambda bi, t: (bi, jnp.maximum(t * halo_blocks - 1, 0), 0)),
            _const_spec(pool_w.shape), _const_spec((1, D_POOL)),
            _const_spec(w_out.shape), _const_spec((1, d)),
            _const_spec((1, d)), _const_spec(wq.shape), mem_blk, mem_blk,
            _const_spec(wo.shape), _const_spec((1, d)),
        ],
        out_specs=tile(d),
        out_shape=jax.ShapeDtypeStruct((b, s, d), jnp.float32),
        compiler_params=_params("parallel", "parallel"),
        name="mixer_out_mem_xattn",
    )(x, attn, p_in, p_in, pool_w, pool_scale, w_out, mix_g, pre_g, wq, k_mem, v_mem, wo, post_g)


def _alibi_slopes(n_heads):
    return jnp.asarray(2.0 ** (-8.0 * np.arange(1, n_heads + 1) / n_heads), jnp.float32)


def kernel(x, mem, ffn1_pre_g, ffn1_w_gate, ffn1_w_up, ffn1_w_down, ffn1_post_g, mix_pre_g, w_in, pool_w, pool_scale, w_out, mix_post_g, xa_pre_g, mem_g, xa_wq, xa_wkv, xa_wo, xa_post_g, ffn2_pre_g, ffn2_w_gate, ffn2_w_up, ffn2_w_down, ffn2_post_g):
    b, s, d = x.shape
    depth = ffn1_pre_g.shape[0]
    bf = lambda w: w.astype(jnp.bfloat16)
    row = lambda g: g.reshape(1, -1)
    slopes = _alibi_slopes(N_ATTN_HEADS) * LOG2_E
    for l in range(depth):
        x = _ffn(x.reshape(b * s, d), row(ffn1_pre_g[l]), bf(ffn1_w_gate[l]),
                 bf(ffn1_w_up[l]), bf(ffn1_w_down[l]), row(ffn1_post_g[l])).reshape(b, s, d)

        w = bf(w_in[l])
        qt, k, kmean, vt, p_in = _proj(
            x, row(mix_pre_g[l]), w[:, :D_ATTN].T, w[:, D_ATTN:2 * D_ATTN],
            w[:, 2 * D_ATTN:3 * D_ATTN].T, w[:, 3 * D_ATTN:])
        attn = _moba(slopes, qt, k.reshape(b, s, D_ATTN), kmean.reshape(b, -1, D_ATTN), vt)
        k_mem, v_mem = _memkv(mem, row(mem_g[l]), bf(xa_wkv[l]))
        x = _mix_xattn(x, attn, p_in, bf(pool_w[l]), row(pool_scale[l]), bf(w_out[l]),
                       row(mix_post_g[l]), row(xa_pre_g[l]), bf(xa_wq[l]), k_mem, v_mem,
                       bf(xa_wo[l]), row(xa_post_g[l]))

        x = _ffn(x.reshape(b * s, d), row(ffn2_pre_g[l]), bf(ffn2_w_gate[l]),
                 bf(ffn2_w_up[l]), bf(ffn2_w_down[l]), row(ffn2_post_g[l])).reshape(b, s, d)
    return x
```

```python
import functools

import numpy as np
import jax
import jax.numpy as jnp
from jax import lax
from jax.experimental import pallas as pl
from jax.experimental.pallas import tpu as pltpu

N_ATTN_HEADS = 8
ATTN_HEAD_DIM = 64
D_ATTN = N_ATTN_HEADS * ATTN_HEAD_DIM
POOL_WINDOWS = (2, 4, 8, 16)
POOL_GROUP_DIM = 128
D_POOL = len(POOL_WINDOWS) * POOL_GROUP_DIM
MOBA_BLOCK = 256
MOBA_TOPK = 3
XA_HEADS = 4
MACARON_WEIGHT = 0.5
RMS_EPS = 1e-6
LOG2_E = 1.4426950408889634

V7X_LANES = 128
V7X_VMEM_BYTES = 64 * 1024 * 1024
VMEM_LIMIT_BYTES = 56 * 1024 * 1024

TOKEN_TILE = 512
HEADS_PER_STEP = V7X_LANES // ATTN_HEAD_DIM
PROJ_TOKEN_TILE = 1024
MIX_TOKEN_TILE = 1024
MIX_SUB_TILE = 256
FFN_SUB_TILE = 256
POOL_HALO = max(POOL_WINDOWS)

_NT = (((1,), (1,)), ((), ()))


def _rmsnorm(x, g):
    r = lax.rsqrt(jnp.mean(x * x, axis=-1, keepdims=True) + RMS_EPS)
    return (x * r) * g


def _dot(a, b):
    return jnp.dot(a, b, preferred_element_type=jnp.float32)


def _params(*semantics):
    return pltpu.CompilerParams(dimension_semantics=semantics,
                                vmem_limit_bytes=VMEM_LIMIT_BYTES)


def _const_spec(shape):
    return pl.BlockSpec(shape, lambda *_: (0,) * len(shape))


def _ffn_kernel(x_ref, pre_g_ref, wg_ref, wu_ref, wd_ref, post_g_ref, o_ref):
    subs = [slice(i * FFN_SUB_TILE, (i + 1) * FFN_SUB_TILE)
            for i in range(x_ref.shape[0] // FFN_SUB_TILE)]
    xs = [x_ref[rows, :] for rows in subs]
    hs = [_rmsnorm(x, pre_g_ref[...]).astype(jnp.bfloat16) for x in xs]
    gates = [_dot(h, wg_ref[...]) for h in hs]
    ups = [_dot(h, wu_ref[...]) for h in hs]
    acts = [(g * jax.nn.sigmoid(g) * u).astype(jnp.bfloat16) for g, u in zip(gates, ups)]
    fs = [_dot(a, wd_ref[...]) for a in acts]
    for rows, x, f in zip(subs, xs, fs):
        o_ref[rows, :] = x + MACARON_WEIGHT * _rmsnorm(f, post_g_ref[...])


def _ffn(x, pre_g, wg, wu, wd, post_g):
    n, d = x.shape
    d_ff = wg.shape[1]
    tile = pl.BlockSpec((TOKEN_TILE, d), lambda i: (i, 0))
    return pl.pallas_call(
        _ffn_kernel,
        grid=(n // TOKEN_TILE,),
        in_specs=[tile, _const_spec((1, d)), _const_spec((d, d_ff)),
                  _const_spec((d, d_ff)), _const_spec((d_ff, d)),
                  _const_spec((1, d))],
        out_specs=tile,
        out_shape=jax.ShapeDtypeStruct((n, d), jnp.float32),
        compiler_params=_params("parallel"),
        name="ffn",
    )(x, pre_g, wg, wu, wd, post_g)


def _proj_kernel(x_ref, g_ref, wqt_ref, wk_ref, wvt_ref, wp_ref,
                 qt_ref, k_ref, kmean_ref, vt_ref, p_ref):
    blocks = [slice(i * MOBA_BLOCK, (i + 1) * MOBA_BLOCK) for i in range(PROJ_TOKEN_TILE // MOBA_BLOCK)]
    hs = [_rmsnorm(x_ref[0, rows, :], g_ref[...]).astype(jnp.bfloat16) for rows in blocks]
    qts = [lax.dot_general(wqt_ref[...], h, _NT, preferred_element_type=jnp.float32) for h in hs]
    ks = [_dot(h, wk_ref[...]) for h in hs]
    vts = [lax.dot_general(wvt_ref[...], h, _NT, preferred_element_type=jnp.float32) for h in hs]
    ps = [_dot(h, wp_ref[...]) for h in hs]
    for blk, rows in enumerate(blocks):
        qt_ref[0, blk] = (qts[blk] * (ATTN_HEAD_DIM ** -0.5 * LOG2_E)).astype(jnp.bfloat16)
        vt_ref[0, blk] = vts[blk].astype(jnp.bfloat16)
        k_ref[0, blk] = ks[blk].astype(jnp.bfloat16)
        kmean_ref[0, blk] = jnp.sum(ks[blk], axis=0, keepdims=True) * (1.0 / MOBA_BLOCK)
        p_ref[0, rows, :] = ps[blk]


def _proj(x, g, wqt, wk, wvt, wp):
    b, s, d = x.shape
    nb = s // MOBA_BLOCK
    bpt = PROJ_TOKEN_TILE // MOBA_BLOCK
    return pl.pallas_call(
        _proj_kernel,
        grid=(b, s // PROJ_TOKEN_TILE),
        in_specs=[pl.BlockSpec((1, PROJ_TOKEN_TILE, d), lambda bi, t: (bi, t, 0)),
                  _const_spec((1, d)), _const_spec((D_ATTN, d)),
                  _const_spec((d, D_ATTN)), _const_spec((D_ATTN, d)),
                  _const_spec((d, D_POOL))],
        out_specs=[
            pl.BlockSpec((1, bpt, D_ATTN, MOBA_BLOCK), lambda bi, t: (bi, t, 0, 0)),
            pl.BlockSpec((1, bpt, MOBA_BLOCK, D_ATTN), lambda bi, t: (bi, t, 0, 0)),
            pl.BlockSpec((1, bpt, 1, D_ATTN), lambda bi, t: (bi, t, 0, 0)),
            pl.BlockSpec((1, bpt, D_ATTN, MOBA_BLOCK), lambda bi, t: (bi, t, 0, 0)),
            pl.BlockSpec((1, PROJ_TOKEN_TILE, D_POOL), lambda bi, t: (bi, t, 0)),
        ],
        out_shape=[
            jax.ShapeDtypeStruct((b, nb, D_ATTN, MOBA_BLOCK), jnp.bfloat16),
            jax.ShapeDtypeStruct((b, nb, MOBA_BLOCK, D_ATTN), jnp.bfloat16),
            jax.ShapeDtypeStruct((b, nb, 1, D_ATTN), jnp.float32),
            jax.ShapeDtypeStruct((b, nb, D_ATTN, MOBA_BLOCK), jnp.bfloat16),
            jax.ShapeDtypeStruct((b, s, D_POOL), jnp.float32),
        ],
        compiler_params=_params("parallel", "parallel"),
        name="mixer_proj",
    )(x, g, wqt, wk, wvt, wp)


MOBA_TILE = 2 * MOBA_BLOCK
SELECT_UNROLL = 4
ONES_ROWS = 16


def _moba_items(n_tiles):
    tiles, pairs = [], []
    for c in range(n_tiles):
        for pr in range(c, -1, -1):
            tiles.append(c)
            pairs.append(pr)
    return np.asarray(tiles, np.int32), np.asarray(pairs, np.int32)


def _moba_kernel(slopes_ref, tile_tab_ref, pair_tab_ref, qt_ref, k_ref, kmean_ref, vt_ref,
                 o_ref, sel_ref, bias_ref, s_ref, p_ref, acc_ref):
    hp = pl.program_id(1)
    L, T = MOBA_BLOCK, MOBA_TILE
    nb = vt_ref.shape[1]
    n_tiles = nb // 2
    n_items = tile_tab_ref.shape[0]
    heads = range(HEADS_PER_STEP)
    slopes = [slopes_ref[hp * HEADS_PER_STEP + hh] for hh in heads]
    head_rows = [slice(hh * ATTN_HEAD_DIM, (hh + 1) * ATTN_HEAD_DIM) for hh in heads]
    head_row = lax.broadcasted_iota(jnp.int32, (HEADS_PER_STEP * ATTN_HEAD_DIM, T), 0) // ATTN_HEAD_DIM

    def query_tile(c):
        qt = jnp.concatenate([qt_ref[0, 2 * c], qt_ref[0, 2 * c + 1]], axis=1)
        return [jnp.where(head_row == hh, qt, jnp.zeros_like(qt)) for hh in heads]

    key = lax.broadcasted_iota(jnp.int32, (T, T), 0)
    query = lax.broadcasted_iota(jnp.int32, (T, T), 1)
    q_minus_k = (query - key).astype(jnp.float32)
    non_causal = jnp.where(key > query, jnp.inf, 0.0)
    for hh in heads:
        base = slopes[hh] * q_minus_k
        bias_ref[hh, 0] = base
        bias_ref[hh, 1] = base + non_causal

    kmean = kmean_ref[0].astype(jnp.bfloat16)
    blk_id = lax.broadcasted_iota(jnp.int32, (nb, T), 0)
    second_half = lax.broadcasted_iota(jnp.int32, (nb, T), 1) // L

    def select_tile(c):
        qt_heads = query_tile(c)
        own = 2 * c + second_half
        for hh in heads:
            gate = jnp.where(blk_id < own, _dot(kmean, qt_heads[hh]), -jnp.inf)
            sel = blk_id == own
            for _ in range(MOBA_TOPK):
                best = jnp.max(gate, axis=0, keepdims=True)
                first = jnp.min(jnp.where(gate == best, blk_id, nb), axis=0, keepdims=True)
                taken = blk_id == first
                sel = sel | (taken & (blk_id < own))
                gate = jnp.where(taken, -jnp.inf, gate)
            sel_ref[hh, c] = sel.astype(jnp.float32)

    def select(it, _):
        for u in range(SELECT_UNROLL):
            select_tile(SELECT_UNROLL * it + u)
        return 0

    lax.fori_loop(0, n_tiles // SELECT_UNROLL, select, 0)

    blk_rows = [slice(blk * L, (blk + 1) * L) for blk in range(2)]
    ones_rows = jnp.ones((ONES_ROWS, L), jnp.bfloat16)

    def biased_scores(e, slot):
        c, pr = tile_tab_ref[e], pair_tab_ref[e]
        variant = jnp.where(pr == c, 1, 0)
        k_pair = k_ref[0, pl.ds(pl.multiple_of(pr * T, T), T), :]
        col_max = []
        for hh, qt_h in enumerate(query_tile(c)):
            t = _dot(k_pair, qt_h) - bias_ref[hh, variant]
            s_ref[slot, hh] = t
            col_max.append(tuple(jnp.max(t[rows, :], axis=0, keepdims=True) for rows in blk_rows))
        return tuple(col_max)

    def value_product(pr, slot):
        out = []
        for hh in heads:
            lhs = [jnp.concatenate([vt_ref[0, 2 * pr + blk, head_rows[hh], :], ones_rows], axis=0)
                   for blk in range(2)]
            out.append(_dot(lhs[0], p_ref[slot, hh, blk_rows[0], :])
                       + _dot(lhs[1], p_ref[slot, hh, blk_rows[1], :]))
        return out

    def denominator(hh):
        d = acc_ref[hh, ATTN_HEAD_DIM:ATTN_HEAD_DIM + 1, :]
        return jnp.where(d == 0.0, 1.0, d)

    def write_tile(c):
        o_t = jnp.concatenate(
            [acc_ref[hh, :ATTN_HEAD_DIM, :] / denominator(hh) for hh in heads], axis=0)
        o_ref[0, pl.ds(pl.multiple_of(c * T, T), T), :] = o_t.T.astype(o_ref.dtype)

    def item(e, slot, state):
        m, alpha_lag1, alpha_lag2, col_max = state
        other = 1 - slot
        c, pr = tile_tab_ref[e], pair_tab_ref[e]
        e_lag2 = jnp.maximum(e - 2, 0)

        pv_lag2 = value_product(pair_tab_ref[e_lag2], slot)
        col_max_next = biased_scores(jnp.minimum(e + 1, n_items - 1), other)

        opens = pr == c
        m_out, alpha_out = [], []
        for hh in heads:
            m_old = jnp.where(opens, -jnp.inf, m[hh])
            offset = slopes[hh] * ((c - pr) * T).astype(jnp.float32)
            keep = [sel_ref[hh, c, pl.ds(2 * pr + blk, 1), :] > 0.5 for blk in range(2)]
            m_blk = [jnp.where(keep[blk], col_max[hh][blk], -jnp.inf) for blk in range(2)]
            m_new = jnp.maximum(m_old, jnp.maximum(m_blk[0], m_blk[1]) - offset)
            for blk in range(2):
                shift = jnp.where(keep[blk], m_new + offset, jnp.inf)
                p = jnp.exp2(s_ref[slot, hh, blk_rows[blk], :] - shift)
                p_ref[slot, hh, blk_rows[blk], :] = p.astype(jnp.bfloat16)
            m_out.append(m_new)
            alpha_out.append(jnp.exp2(m_old - m_new))

        fold(pv_lag2, alpha_lag2)
        write_tile(tile_tab_ref[e_lag2])

        return tuple(m_out), tuple(alpha_out), alpha_lag1, col_max_next

    def fold(pv, alpha):
        for hh in heads:
            acc_ref[hh] = alpha[hh] * acc_ref[hh] + pv[hh]

    col_max = biased_scores(0, 0)
    for hh in heads:
        acc_ref[hh] = jnp.zeros(acc_ref.shape[1:], jnp.float32)
    p_ref[...] = jnp.zeros(p_ref.shape, p_ref.dtype)
    row = lambda v: tuple(jnp.full((1, T), v, jnp.float32) for _ in heads)
    state = (row(-jnp.inf), row(0.0), row(0.0), col_max)

    def two_items(it, state):
        state = item(2 * it, 0, state)
        return item(2 * it + 1, 1, state)

    _, alpha_lag1, alpha_lag2, _ = lax.fori_loop(0, n_items // 2, two_items, state)

    for e, alpha in ((n_items - 2, alpha_lag2), (n_items - 1, alpha_lag1)):
        fold(value_product(pair_tab_ref[e], e % 2), alpha)
        write_tile(tile_tab_ref[e])


def _moba(slopes, qt, k, kmean, vt):
    b, nb, _, L = qt.shape
    lanes = HEADS_PER_STEP * ATTN_HEAD_DIM
    T = MOBA_TILE
    tile_tab, pair_tab = _moba_items(nb // 2)
    assert len(tile_tab) % 2 == 0
    smem = pl.BlockSpec(memory_space=pltpu.SMEM)
    return pl.pallas_call(
        _moba_kernel,
        grid=(b, N_ATTN_HEADS // HEADS_PER_STEP),
        in_specs=[
            smem, smem, smem,
            pl.BlockSpec((1, nb, lanes, L), lambda bi, hp: (bi, 0, hp, 0)),
            pl.BlockSpec((1, nb * L, lanes), lambda bi, hp: (bi, 0, hp)),
            pl.BlockSpec((1, nb, lanes), lambda bi, hp: (bi, 0, hp)),
            pl.BlockSpec((1, nb, lanes, L), lambda bi, hp: (bi, 0, hp, 0)),
        ],
        out_specs=pl.BlockSpec((1, nb * L, lanes), lambda bi, hp: (bi, 0, hp)),
        out_shape=jax.ShapeDtypeStruct((b, nb * L, D_ATTN), jnp.bfloat16),
        scratch_shapes=[
            pltpu.VMEM((HEADS_PER_STEP, nb // 2, nb, T), jnp.float32),
            pltpu.VMEM((HEADS_PER_STEP, 2, T, T), jnp.float32),
            pltpu.VMEM((2, HEADS_PER_STEP, T, T), jnp.float32),
            pltpu.VMEM((2, HEADS_PER_STEP, T, T), jnp.bfloat16),
            pltpu.VMEM((HEADS_PER_STEP, ATTN_HEAD_DIM + ONES_ROWS, T), jnp.float32),
        ],
        compiler_params=_params("parallel", "parallel"),
        name="moba_attention",
    )(slopes, jnp.asarray(tile_tab), jnp.asarray(pair_tab), qt, k, kmean, vt)


def _memkv_kernel(mem_ref, g_ref, wkv_ref, k_ref, v_ref):
    d = mem_ref.shape[2]
    mem_n = _rmsnorm(mem_ref[0], g_ref[...]).astype(jnp.bfloat16)
    kv = _dot(mem_n, wkv_ref[...])
    k_ref[0] = kv[:, :d].astype(jnp.bfloat16)
    v_ref[0] = kv[:, d:].astype(jnp.bfloat16)


def _memkv(mem, g, wkv):
    b, m, d = mem.shape
    blk = pl.BlockSpec((1, m, d), lambda bi: (bi, 0, 0))
    return pl.pallas_call(
        _memkv_kernel,
        grid=(b,),
        in_specs=[blk, _const_spec((1, d)), _const_spec(wkv.shape)],
        out_specs=[blk, blk],
        out_shape=[jax.ShapeDtypeStruct((b, m, d), jnp.bfloat16)] * 2,
        compiler_params=_params("parallel"),
        name="mem_kv",
    )(mem, g, wkv)


def _pooling_mixer(ext, p, first_pos, pool_w_ref, pool_scale_ref):
    rows = p.shape[0]
    pos = first_pos + lax.broadcasted_iota(jnp.int32, (rows, 1), 0)
    pooled = []
    for g, w in enumerate(POOL_WINDOWS):
        cols = slice(g * POOL_GROUP_DIM, (g + 1) * POOL_GROUP_DIM)
        acc = ext[:, cols]
        n_valid = acc.shape[0]
        span = 1
        while span < w:
            acc = acc[span:, :] + acc[:n_valid - span, :]
            n_valid -= span
            span *= 2
        win_sum = acc[n_valid - rows:, :]
        cnt = jnp.minimum(pos + 1, w).astype(jnp.float32)
        d = win_sum / cnt - p[:, cols]
        y = _dot(d.astype(jnp.bfloat16), pool_w_ref[g])
        pooled.append(y * pool_scale_ref[:, cols])
    return jnp.concatenate(pooled, axis=-1).astype(jnp.bfloat16)


def _mix_xattn_kernel(x_ref, attn_ref, p_ref, halo_ref, pool_w_ref, pool_scale_ref,
                      w_out_ref, mix_g_ref, pre_g_ref, wq_ref, k_ref, v_ref, wo_ref,
                      post_g_ref, o_ref):
    t = pl.program_id(1)
    tm, d = x_ref.shape[1], x_ref.shape[2]
    dh = d // XA_HEADS
    subs = [slice(i * MIX_SUB_TILE, (i + 1) * MIX_SUB_TILE) for i in range(tm // MIX_SUB_TILE)]
    halo = jnp.where(t > 0, halo_ref[0], 0.0)
    ext = jnp.concatenate([halo, p_ref[0]], axis=0)
    pools = [_pooling_mixer(ext[rows.start:rows.stop + POOL_HALO, :], p_ref[0, rows, :],
                            t * tm + rows.start, pool_w_ref, pool_scale_ref) for rows in subs]
    ys = [_dot(jnp.concatenate([attn_ref[0, rows, :], pool], axis=-1), w_out_ref[...])
          for rows, pool in zip(subs, pools)]
    xs = [x_ref[0, rows, :] + _rmsnorm(y, mix_g_ref[...]) for rows, y in zip(subs, ys)]

    hs = [_rmsnorm(x, pre_g_ref[...]).astype(jnp.bfloat16) for x in xs]
    qs = [_dot(h, wq_ref[...]).astype(jnp.bfloat16) for h in hs]
    heads = [[] for _ in subs]
    for hd in range(XA_HEADS):
        cols = slice(hd * dh, (hd + 1) * dh)
        ss = [lax.dot_general(q[:, cols], k_ref[0, :, cols], _NT,
                              preferred_element_type=jnp.float32) * (dh ** -0.5) for q in qs]
        for i, s in enumerate(ss):
            e = jnp.exp(s - jnp.max(s, axis=-1, keepdims=True))
            p = e / jnp.sum(e, axis=-1, keepdims=True)
            heads[i].append(_dot(p.astype(jnp.bfloat16), v_ref[0, :, cols]))
    os = [jnp.concatenate(hl, axis=-1).astype(jnp.bfloat16) for hl in heads]
    cs = [_dot(o, wo_ref[...]) for o in os]
    for rows, x, c in zip(subs, xs, cs):
        o_ref[0, rows, :] = x + _rmsnorm(c, post_g_ref[...])


def _mix_xattn(x, attn, p_in, pool_w, pool_scale, w_out, mix_g, pre_g, wq, k_mem, v_mem, wo,
               post_g):
    b, s, d = x.shape
    m = k_mem.shape[1]
    halo_blocks = MIX_TOKEN_TILE // POOL_HALO
    tile = lambda width: pl.BlockSpec((1, MIX_TOKEN_TILE, width), lambda bi, t: (bi, t, 0))
    mem_blk = pl.BlockSpec((1, m, d), lambda bi, t: (bi, 0, 0))
    return pl.pallas_call(
        _mix_xattn_kernel,
        grid=(b, s // MIX_TOKEN_TILE),
        in_specs=[
            tile(d), tile(D_ATTN), tile(D_POOL),
            pl.BlockSpec((1, POOL_HALO, D_POOL),
                         lambda bi, t: (bi, jnp.maximum(t * halo_blocks - 1, 0), 0)),
            _const_spec(pool_w.shape), _const_spec((1, D_POOL)),
            _const_spec(w_out.shape), _const_spec((1, d)),
            _const_spec((1, d)), _const_spec(wq.shape), mem_blk, mem_blk,
            _const_spec(wo.shape), _const_spec((1, d)),
        ],
        out_specs=tile(d),
        out_shape=jax.ShapeDtypeStruct((b, s, d), jnp.float32),
        compiler_params=_params("parallel", "parallel"),
        name="mixer_out_mem_xattn",
    )(x, attn, p_in, p_in, pool_w, pool_scale, w_out, mix_g, pre_g, wq, k_mem, v_mem, wo, post_g)


def _alibi_slopes(n_heads):
    return jnp.asarray(2.0 ** (-8.0 * np.arange(1, n_heads + 1) / n_heads), jnp.float32)


def kernel(x, mem, ffn1_pre_g, ffn1_w_gate, ffn1_w_up, ffn1_w_down, ffn1_post_g, mix_pre_g, w_in, pool_w, pool_scale, w_out, mix_post_g, xa_pre_g, mem_g, xa_wq, xa_wkv, xa_wo, xa_post_g, ffn2_pre_g, ffn2_w_gate, ffn2_w_up, ffn2_w_down, ffn2_post_g):
    b, s, d = x.shape
    depth = ffn1_pre_g.shape[0]
    bf = lambda w: w.astype(jnp.bfloat16)
    row = lambda g: g.reshape(1, -1)
    slopes = _alibi_slopes(N_ATTN_HEADS) * LOG2_E
    for l in range(depth):
        x = _ffn(x.reshape(b * s, d), row(ffn1_pre_g[l]), bf(ffn1_w_gate[l]),
                 bf(ffn1_w_up[l]), bf(ffn1_w_down[l]), row(ffn1_post_g[l])).reshape(b, s, d)

        w = bf(w_in[l])
        qt, k, kmean, vt, p_in = _proj(
            x, row(mix_pre_g[l]), w[:, :D_ATTN].T, w[:, D_ATTN:2 * D_ATTN],
            w[:, 2 * D_ATTN:3 * D_ATTN].T, w[:, 3 * D_ATTN:])
        attn = _moba(slopes, qt, k.reshape(b, s, D_ATTN), kmean.reshape(b, -1, D_ATTN), vt)
        k_mem, v_mem = _memkv(mem, row(mem_g[l]), bf(xa_wkv[l]))
        x = _mix_xattn(x, attn, p_in, bf(pool_w[l]), row(pool_scale[l]), bf(w_out[l]),
                       row(mix_post_g[l]), row(xa_pre_g[l]), bf(xa_wq[l]), k_mem, v_mem,
                       bf(xa_wo[l]), row(xa_post_g[l]))

        x = _ffn(x.reshape(b * s, d), row(ffn2_pre_g[l]), bf(ffn2_w_gate[l]),
                 bf(ffn2_w_up[l]), bf(ffn2_w_down[l]), row(ffn2_post_g[l])).reshape(b, s, d)
    return x
```

```python
import functools

import numpy as np
import jax
import jax.numpy as jnp
from jax import lax
from jax.experimental import pallas as pl
from jax.experimental.pallas import tpu as pltpu

N_ATTN_HEADS = 8
ATTN_HEAD_DIM = 64
D_ATTN = N_ATTN_HEADS * ATTN_HEAD_DIM
POOL_WINDOWS = (2, 4, 8, 16)
POOL_GROUP_DIM = 128
D_POOL = len(POOL_WINDOWS) * POOL_GROUP_DIM
MOBA_BLOCK = 256
MOBA_TOPK = 3
XA_HEADS = 4
MACARON_WEIGHT = 0.5
RMS_EPS = 1e-6
LOG2_E = 1.4426950408889634

V7X_LANES = 128
V7X_VMEM_BYTES = 64 * 1024 * 1024
VMEM_LIMIT_BYTES = 56 * 1024 * 1024

TOKEN_TILE = 512
HEADS_PER_STEP = V7X_LANES // ATTN_HEAD_DIM
PROJ_TOKEN_TILE = 1024
MIX_TOKEN_TILE = 1024
MIX_SUB_TILE = 256
WEIGHT_CHUNK_ROWS = 128
FFN_SUB_TILE = 256
POOL_HALO = max(POOL_WINDOWS)

_NT = (((1,), (1,)), ((), ()))


def _rmsnorm(x, g):
    r = lax.rsqrt(jnp.mean(x * x, axis=-1, keepdims=True) + RMS_EPS)
    return (x * r) * g


def _dot(a, b):
    return jnp.dot(a, b, preferred_element_type=jnp.float32)


def _params(*semantics):
    return pltpu.CompilerParams(dimension_semantics=semantics,
                                vmem_limit_bytes=VMEM_LIMIT_BYTES)


def _const_spec(shape):
    return pl.BlockSpec(shape, lambda *_: (0,) * len(shape))


def _stage_cast(src_hbm, dst_ref, stage_ref, sem):
    rows, cols = src_hbm.shape
    n_chunks = rows // WEIGHT_CHUNK_ROWS

    def chunk_copy(c):
        slot = c % 2
        return pltpu.make_async_copy(
            src_hbm.at[pl.ds(c * WEIGHT_CHUNK_ROWS, WEIGHT_CHUNK_ROWS), :],
            stage_ref.at[slot, :, pl.ds(0, cols)], sem.at[slot])

    chunk_copy(0).start()
    for c in range(n_chunks):
        if c + 1 < n_chunks:
            chunk_copy(c + 1).start()
        chunk_copy(c).wait()
        dst_ref[pl.ds(c * WEIGHT_CHUNK_ROWS, WEIGHT_CHUNK_ROWS), :] = (
            stage_ref[c % 2, :, pl.ds(0, cols)].astype(jnp.bfloat16))


def _ffn_kernel(x_ref, pre_g_ref, wg_hbm, wu_hbm, wd_hbm, post_g_ref, o_ref,
                wg_ref, wu_ref, wd_ref, stage_ref, sem):
    @pl.when(pl.program_id(0) == 0)
    def _():
        for src, dst in ((wg_hbm, wg_ref), (wu_hbm, wu_ref), (wd_hbm, wd_ref)):
            _stage_cast(src, dst, stage_ref, sem)

    subs = [slice(i * FFN_SUB_TILE, (i + 1) * FFN_SUB_TILE)
            for i in range(x_ref.shape[0] // FFN_SUB_TILE)]
    xs = [x_ref[rows, :] for rows in subs]
    hs = [_rmsnorm(x, pre_g_ref[...]).astype(jnp.bfloat16) for x in xs]
    gates = [_dot(h, wg_ref[...]) for h in hs]
    ups = [_dot(h, wu_ref[...]) for h in hs]
    acts = [(g * jax.nn.sigmoid(g) * u).astype(jnp.bfloat16) for g, u in zip(gates, ups)]
    fs = [_dot(a, wd_ref[...]) for a in acts]
    for rows, x, f in zip(subs, xs, fs):
        o_ref[rows, :] = x + MACARON_WEIGHT * _rmsnorm(f, post_g_ref[...])


def _ffn(x, pre_g, wg, wu, wd, post_g):
    n, d = x.shape
    d_ff = wg.shape[1]
    assert d % WEIGHT_CHUNK_ROWS == 0 and d_ff % WEIGHT_CHUNK_ROWS == 0
    tile = pl.BlockSpec((TOKEN_TILE, d), lambda i: (i, 0))
    hbm = pl.BlockSpec(memory_space=pl.ANY)
    return pl.pallas_call(
        _ffn_kernel,
        grid=(n // TOKEN_TILE,),
        in_specs=[tile, _const_spec((1, d)), hbm, hbm, hbm, _const_spec((1, d))],
        out_specs=tile,
        out_shape=jax.ShapeDtypeStruct((n, d), jnp.float32),
        scratch_shapes=[
            pltpu.VMEM((d, d_ff), jnp.bfloat16), pltpu.VMEM((d, d_ff), jnp.bfloat16),
            pltpu.VMEM((d_ff, d), jnp.bfloat16),
            pltpu.VMEM((2, WEIGHT_CHUNK_ROWS, max(d, d_ff)), jnp.float32),
            pltpu.SemaphoreType.DMA((2,)),
        ],
        compiler_params=_params("arbitrary"),
        name="ffn",
    )(x, pre_g, wg, wu, wd, post_g)


def _proj_kernel(x_ref, g_ref, wqt_ref, wk_ref, wvt_ref, wp_ref,
                 qt_ref, k_ref, kmean_ref, vt_ref, p_ref):
    blocks = [slice(i * MOBA_BLOCK, (i + 1) * MOBA_BLOCK) for i in range(PROJ_TOKEN_TILE // MOBA_BLOCK)]
    hs = [_rmsnorm(x_ref[0, rows, :], g_ref[...]).astype(jnp.bfloat16) for rows in blocks]
    qts = [lax.dot_general(wqt_ref[...], h, _NT, preferred_element_type=jnp.float32) for h in hs]
    ks = [_dot(h, wk_ref[...]) for h in hs]
    vts = [lax.dot_general(wvt_ref[...], h, _NT, preferred_element_type=jnp.float32) for h in hs]
    ps = [_dot(h, wp_ref[...]) for h in hs]
    for blk, rows in enumerate(blocks):
        qt_ref[0, blk] = (qts[blk] * (ATTN_HEAD_DIM ** -0.5 * LOG2_E)).astype(jnp.bfloat16)
        vt_ref[0, blk] = vts[blk].astype(jnp.bfloat16)
        k_ref[0, blk] = ks[blk].astype(jnp.bfloat16)
        kmean_ref[0, blk] = jnp.sum(ks[blk], axis=0, keepdims=True) * (1.0 / MOBA_BLOCK)
        p_ref[0, rows, :] = ps[blk]


def _proj(x, g, wqt, wk, wvt, wp):
    b, s, d = x.shape
    nb = s // MOBA_BLOCK
    bpt = PROJ_TOKEN_TILE // MOBA_BLOCK
    return pl.pallas_call(
        _proj_kernel,
        grid=(b, s // PROJ_TOKEN_TILE),
        in_specs=[pl.BlockSpec((1, PROJ_TOKEN_TILE, d), lambda bi, t: (bi, t, 0)),
                  _const_spec((1, d)), _const_spec((D_ATTN, d)),
                  _const_spec((d, D_ATTN)), _const_spec((D_ATTN, d)),
                  _const_spec((d, D_POOL))],
        out_specs=[
            pl.BlockSpec((1, bpt, D_ATTN, MOBA_BLOCK), lambda bi, t: (bi, t, 0, 0)),
            pl.BlockSpec((1, bpt, MOBA_BLOCK, D_ATTN), lambda bi, t: (bi, t, 0, 0)),
            pl.BlockSpec((1, bpt, 1, D_ATTN), lambda bi, t: (bi, t, 0, 0)),
            pl.BlockSpec((1, bpt, D_ATTN, MOBA_BLOCK), lambda bi, t: (bi, t, 0, 0)),
            pl.BlockSpec((1, PROJ_TOKEN_TILE, D_POOL), lambda bi, t: (bi, t, 0)),
        ],
        out_shape=[
            jax.ShapeDtypeStruct((b, nb, D_ATTN, MOBA_BLOCK), jnp.bfloat16),
            jax.ShapeDtypeStruct((b, nb, MOBA_BLOCK, D_ATTN), jnp.bfloat16),
            jax.ShapeDtypeStruct((b, nb, 1, D_ATTN), jnp.float32),
            jax.ShapeDtypeStruct((b, nb, D_ATTN, MOBA_BLOCK), jnp.bfloat16),
            jax.ShapeDtypeStruct((b, s, D_POOL), jnp.float32),
        ],
        compiler_params=_params("parallel", "parallel"),
        name="mixer_proj",
    )(x, g, wqt, wk, wvt, wp)


MOBA_TILE = 2 * MOBA_BLOCK
SELECT_UNROLL = 4
ONES_ROWS = 16


def _moba_items(n_tiles):
    tiles, pairs = [], []
    for c in range(n_tiles):
        for pr in range(c, -1, -1):
            tiles.append(c)
            pairs.append(pr)
    return np.asarray(tiles, np.int32), np.asarray(pairs, np.int32)


def _moba_kernel(slopes_ref, tile_tab_ref, pair_tab_ref, qt_ref, k_ref, kmean_ref, vt_ref,
                 o_ref, sel_ref, bias_ref, s_ref, p_ref, acc_ref):
    hp = pl.program_id(1)
    L, T = MOBA_BLOCK, MOBA_TILE
    nb = vt_ref.shape[1]
    n_tiles = nb // 2
    n_items = tile_tab_ref.shape[0]
    heads = range(HEADS_PER_STEP)
    slopes = [slopes_ref[hp * HEADS_PER_STEP + hh] for hh in heads]
    head_rows = [slice(hh * ATTN_HEAD_DIM, (hh + 1) * ATTN_HEAD_DIM) for hh in heads]
    head_row = lax.broadcasted_iota(jnp.int32, (HEADS_PER_STEP * ATTN_HEAD_DIM, T), 0) // ATTN_HEAD_DIM

    def query_tile(c):
        qt = jnp.concatenate([qt_ref[0, 2 * c], qt_ref[0, 2 * c + 1]], axis=1)
        return [jnp.where(head_row == hh, qt, jnp.zeros_like(qt)) for hh in heads]

    key = lax.broadcasted_iota(jnp.int32, (T, T), 0)
    query = lax.broadcasted_iota(jnp.int32, (T, T), 1)
    q_minus_k = (query - key).astype(jnp.float32)
    non_causal = jnp.where(key > query, jnp.inf, 0.0)
    for hh in heads:
        base = slopes[hh] * q_minus_k
        bias_ref[hh, 0] = base
        bias_ref[hh, 1] = base + non_causal

    kmean = kmean_ref[0].astype(jnp.bfloat16)
    blk_id = lax.broadcasted_iota(jnp.int32, (nb, T), 0)
    second_half = lax.broadcasted_iota(jnp.int32, (nb, T), 1) // L

    def select_tile(c):
        qt_heads = query_tile(c)
        own = 2 * c + second_half
        for hh in heads:
            gate = jnp.where(blk_id < own, _dot(kmean, qt_heads[hh]), -jnp.inf)
            sel = blk_id == own
            for _ in range(MOBA_TOPK):
                best = jnp.max(gate, axis=0, keepdims=True)
                first = jnp.min(jnp.where(gate == best, blk_id, nb), axis=0, keepdims=True)
                taken = blk_id == first
                sel = sel | (taken & (blk_id < own))
                gate = jnp.where(taken, -jnp.inf, gate)
            sel_ref[hh, c] = sel.astype(jnp.float32)

    def select(it, _):
        for u in range(SELECT_UNROLL):
            select_tile(SELECT_UNROLL * it + u)
        return 0

    lax.fori_loop(0, n_tiles // SELECT_UNROLL, select, 0)

    blk_rows = [slice(blk * L, (blk + 1) * L) for blk in range(2)]
    ones_rows = jnp.ones((ONES_ROWS, L), jnp.bfloat16)

    def biased_scores(e, slot):
        c, pr = tile_tab_ref[e], pair_tab_ref[e]
        variant = jnp.where(pr == c, 1, 0)
        k_pair = k_ref[0, pl.ds(pl.multiple_of(pr * T, T), T), :]
        col_max = []
        for hh, qt_h in enumerate(query_tile(c)):
            t = _dot(k_pair, qt_h) - bias_ref[hh, variant]
            s_ref[slot, hh] = t
            col_max.append(tuple(jnp.max(t[rows, :], axis=0, keepdims=True) for rows in blk_rows))
        return tuple(col_max)

    def value_product(pr, slot):
        out = []
        for hh in heads:
            lhs = [jnp.concatenate([vt_ref[0, 2 * pr + blk, head_rows[hh], :], ones_rows], axis=0)
                   for blk in range(2)]
            out.append(_dot(lhs[0], p_ref[slot, hh, blk_rows[0], :])
                       + _dot(lhs[1], p_ref[slot, hh, blk_rows[1], :]))
        return out

    def denominator(hh):
        d = acc_ref[hh, ATTN_HEAD_DIM:ATTN_HEAD_DIM + 1, :]
        return jnp.where(d == 0.0, 1.0, d)

    def write_tile(c):
        o_t = jnp.concatenate(
            [acc_ref[hh, :ATTN_HEAD_DIM, :] / denominator(hh) for hh in heads], axis=0)
        o_ref[0, pl.ds(pl.multiple_of(c * T, T), T), :] = o_t.T.astype(o_ref.dtype)

    def item(e, slot, state):
        m, alpha_lag1, alpha_lag2, col_max = state
        other = 1 - slot
        c, pr = tile_tab_ref[e], pair_tab_ref[e]
        e_lag2 = jnp.maximum(e - 2, 0)

        pv_lag2 = value_product(pair_tab_ref[e_lag2], slot)
        col_max_next = biased_scores(jnp.minimum(e + 1, n_items - 1), other)

        opens = pr == c
        m_out, alpha_out = [], []
        for hh in heads:
            m_old = jnp.where(opens, -jnp.inf, m[hh])
            offset = slopes[hh] * ((c - pr) * T).astype(jnp.float32)
            keep = [sel_ref[hh, c, pl.ds(2 * pr + blk, 1), :] > 0.5 for blk in range(2)]
            m_blk = [jnp.where(keep[blk], col_max[hh][blk], -jnp.inf) for blk in range(2)]
            m_new = jnp.maximum(m_old, jnp.maximum(m_blk[0], m_blk[1]) - offset)
            for blk in range(2):
                shift = jnp.where(keep[blk], m_new + offset, jnp.inf)
                p = jnp.exp2(s_ref[slot, hh, blk_rows[blk], :] - shift)
                p_ref[slot, hh, blk_rows[blk], :] = p.astype(jnp.bfloat16)
            m_out.append(m_new)
            alpha_out.append(jnp.exp2(m_old - m_new))

        fold(pv_lag2, alpha_lag2)
        write_tile(tile_tab_ref[e_lag2])

        return tuple(m_out), tuple(alpha_out), alpha_lag1, col_max_next

    def fold(pv, alpha):
        for hh in heads:
            acc_ref[hh] = alpha[hh] * acc_ref[hh] + pv[hh]

    col_max = biased_scores(0, 0)
    for hh in heads:
        acc_ref[hh] = jnp.zeros(acc_ref.shape[1:], jnp.float32)
    p_ref[...] = jnp.zeros(p_ref.shape, p_ref.dtype)
    row = lambda v: tuple(jnp.full((1, T), v, jnp.float32) for _ in heads)
    state = (row(-jnp.inf), row(0.0), row(0.0), col_max)

    def two_items(it, state):
        state = item(2 * it, 0, state)
        return item(2 * it + 1, 1, state)

    _, alpha_lag1, alpha_lag2, _ = lax.fori_loop(0, n_items // 2, two_items, state)

    for e, alpha in ((n_items - 2, alpha_lag2), (n_items - 1, alpha_lag1)):
        fold(value_product(pair_tab_ref[e], e % 2), alpha)
        write_tile(tile_tab_ref[e])


def _moba(slopes, qt, k, kmean, vt):
    b, nb, _, L = qt.shape
    lanes = HEADS_PER_STEP * ATTN_HEAD_DIM
    T = MOBA_TILE
    tile_tab, pair_tab = _moba_items(nb // 2)
    assert len(tile_tab) % 2 == 0
    smem = pl.BlockSpec(memory_space=pltpu.SMEM)
    return pl.pallas_call(
        _moba_kernel,
        grid=(b, N_ATTN_HEADS // HEADS_PER_STEP),
        in_specs=[
            smem, smem, smem,
            pl.BlockSpec((1, nb, lanes, L), lambda bi, hp: (bi, 0, hp, 0)),
            pl.BlockSpec((1, nb * L, lanes), lambda bi, hp: (bi, 0, hp)),
            pl.BlockSpec((1, nb, lanes), lambda bi, hp: (bi, 0, hp)),
            pl.BlockSpec((1, nb, lanes, L), lambda bi, hp: (bi, 0, hp, 0)),
        ],
        out_specs=pl.BlockSpec((1, nb * L, lanes), lambda bi, hp: (bi, 0, hp)),
        out_shape=jax.ShapeDtypeStruct((b, nb * L, D_ATTN), jnp.bfloat16),
        scratch_shapes=[
            pltpu.VMEM((HEADS_PER_STEP, nb // 2, nb, T), jnp.float32),
            pltpu.VMEM((HEADS_PER_STEP, 2, T, T), jnp.float32),
            pltpu.VMEM((2, HEADS_PER_STEP, T, T), jnp.float32),
            pltpu.VMEM((2, HEADS_PER_STEP, T, T), jnp.bfloat16),
            pltpu.VMEM((HEADS_PER_STEP, ATTN_HEAD_DIM + ONES_ROWS, T), jnp.float32),
        ],
        compiler_params=_params("parallel", "parallel"),
        name="moba_attention",
    )(slopes, jnp.asarray(tile_tab), jnp.asarray(pair_tab), qt, k, kmean, vt)


def _memkv_kernel(mem_ref, g_ref, wkv_ref, k_ref, v_ref):
    d = mem_ref.shape[2]
    mem_n = _rmsnorm(mem_ref[0], g_ref[...]).astype(jnp.bfloat16)
    kv = _dot(mem_n, wkv_ref[...])
    k_ref[0] = kv[:, :d].astype(jnp.bfloat16)
    v_ref[0] = kv[:, d:].astype(jnp.bfloat16)


def _memkv(mem, g, wkv):
    b, m, d = mem.shape
    blk = pl.BlockSpec((1, m, d), lambda bi: (bi, 0, 0))
    return pl.pallas_call(
        _memkv_kernel,
        grid=(b,),
        in_specs=[blk, _const_spec((1, d)), _const_spec(wkv.shape)],
        out_specs=[blk, blk],
        out_shape=[jax.ShapeDtypeStruct((b, m, d), jnp.bfloat16)] * 2,
        compiler_params=_params("parallel"),
        name="mem_kv",
    )(mem, g, wkv)


def _pooling_mixer(ext, p, first_pos, pool_w_ref, pool_scale_ref):
    rows = p.shape[0]
    pos = first_pos + lax.broadcasted_iota(jnp.int32, (rows, 1), 0)
    pooled = []
    for g, w in enumerate(POOL_WINDOWS):
        cols = slice(g * POOL_GROUP_DIM, (g + 1) * POOL_GROUP_DIM)
        acc = ext[:, cols]
        n_valid = acc.shape[0]
        span = 1
        while span < w:
            acc = acc[span:, :] + acc[:n_valid - span, :]
            n_valid -= span
            span *= 2
        win_sum = acc[n_valid - rows:, :]
        cnt = jnp.minimum(pos + 1, w).astype(jnp.float32)
        d = win_sum / cnt - p[:, cols]
        y = _dot(d.astype(jnp.bfloat16), pool_w_ref[g])
        pooled.append(y * pool_scale_ref[:, cols])
    return jnp.concatenate(pooled, axis=-1).astype(jnp.bfloat16)


def _mix_xattn_kernel(x_ref, attn_ref, p_ref, halo_ref, pool_w_ref, pool_scale_ref,
                      w_out_ref, mix_g_ref, pre_g_ref, wq_ref, k_ref, v_ref, wo_ref,
                      post_g_ref, o_ref):
    t = pl.program_id(1)
    tm, d = x_ref.shape[1], x_ref.shape[2]
    dh = d // XA_HEADS
    subs = [slice(i * MIX_SUB_TILE, (i + 1) * MIX_SUB_TILE) for i in range(tm // MIX_SUB_TILE)]
    halo = jnp.where(t > 0, halo_ref[0], 0.0)
    ext = jnp.concatenate([halo, p_ref[0]], axis=0)
    pools = [_pooling_mixer(ext[rows.start:rows.stop + POOL_HALO, :], p_ref[0, rows, :],
                            t * tm + rows.start, pool_w_ref, pool_scale_ref) for rows in subs]
    ys = [_dot(jnp.concatenate([attn_ref[0, rows, :], pool], axis=-1), w_out_ref[...])
          for rows, pool in zip(subs, pools)]
    xs = [x_ref[0, rows, :] + _rmsnorm(y, mix_g_ref[...]) for rows, y in zip(subs, ys)]

    hs = [_rmsnorm(x, pre_g_ref[...]).astype(jnp.bfloat16) for x in xs]
    qs = [_dot(h, wq_ref[...]).astype(jnp.bfloat16) for h in hs]
    heads = [[] for _ in subs]
    for hd in range(XA_HEADS):
        cols = slice(hd * dh, (hd + 1) * dh)
        ss = [lax.dot_general(q[:, cols], k_ref[0, :, cols], _NT,
                              preferred_element_type=jnp.float32) * (dh ** -0.5) for q in qs]
        for i, s in enumerate(ss):
            e = jnp.exp(s - jnp.max(s, axis=-1, keepdims=True))
            p = e / jnp.sum(e, axis=-1, keepdims=True)
            heads[i].append(_dot(p.astype(jnp.bfloat16), v_ref[0, :, cols]))
    os = [jnp.concatenate(hl, axis=-1).astype(jnp.bfloat16) for hl in heads]
    cs = [_dot(o, wo_ref[...]) for o in os]
    for rows, x, c in zip(subs, xs, cs):
        o_ref[0, rows, :] = x + _rmsnorm(c, post_g_ref[...])


def _mix_xattn(x, attn, p_in, pool_w, pool_scale, w_out, mix_g, pre_g, wq, k_mem, v_mem, wo,
               post_g):
    b, s, d = x.shape
    m = k_mem.shape[1]
    halo_blocks = MIX_TOKEN_TILE // POOL_HALO
    tile = lambda width: pl.BlockSpec((1, MIX_TOKEN_TILE, width), lambda bi, t: (bi, t, 0))
    mem_blk = pl.BlockSpec((1, m, d), lambda bi, t: (bi, 0, 0))
    return pl.pallas_call(
        _mix_xattn_kernel,
        grid=(b, s // MIX_TOKEN_TILE),
        in_specs=[
            tile(d), tile(D_ATTN), tile(D_POOL),
            pl.BlockSpec((1, POOL_HALO, D_POOL),
                         lambda bi, t: (bi, jnp.maximum(t * halo_blocks - 1, 0), 0)),
            _const_spec(pool_w.shape), _const_spec((1, D_POOL)),
            _const_spec(w_out.shape), _const_spec((1, d)),
            _const_spec((1, d)), _const_spec(wq.shape), mem_blk, mem_blk,
            _const_spec(wo.shape), _const_spec((1, d)),
        ],
        out_specs=tile(d),
        out_shape=jax.ShapeDtypeStruct((b, s, d), jnp.float32),
        compiler_params=_params("parallel", "parallel"),
        name="mixer_out_mem_xattn",
    )(x, attn, p_in, p_in, pool_w, pool_scale, w_out, mix_g, pre_g, wq, k_mem, v_mem, wo, post_g)


def _alibi_slopes(n_heads):
    return jnp.asarray(2.0 ** (-8.0 * np.arange(1, n_heads + 1) / n_heads), jnp.float32)


def kernel(x, mem, ffn1_pre_g, ffn1_w_gate, ffn1_w_up, ffn1_w_down, ffn1_post_g, mix_pre_g, w_in, pool_w, pool_scale, w_out, mix_post_g, xa_pre_g, mem_g, xa_wq, xa_wkv, xa_wo, xa_post_g, ffn2_pre_g, ffn2_w_gate, ffn2_w_up, ffn2_w_down, ffn2_post_g):
    b, s, d = x.shape
    depth = ffn1_pre_g.shape[0]
    bf = lambda w: w.astype(jnp.bfloat16)
    row = lambda g: g.reshape(1, -1)
    slopes = _alibi_slopes(N_ATTN_HEADS) * LOG2_E
    for l in range(depth):
        x = _ffn(x.reshape(b * s, d), row(ffn1_pre_g[l]), ffn1_w_gate[l], ffn1_w_up[l],
                 ffn1_w_down[l], row(ffn1_post_g[l])).reshape(b, s, d)

        w = bf(w_in[l])
        qt, k, kmean, vt, p_in = _proj(
            x, row(mix_pre_g[l]), w[:, :D_ATTN].T, w[:, D_ATTN:2 * D_ATTN],
            w[:, 2 * D_ATTN:3 * D_ATTN].T, w[:, 3 * D_ATTN:])
        attn = _moba(slopes, qt, k.reshape(b, s, D_ATTN), kmean.reshape(b, -1, D_ATTN), vt)
        k_mem, v_mem = _memkv(mem, row(mem_g[l]), bf(xa_wkv[l]))
        x = _mix_xattn(x, attn, p_in, bf(pool_w[l]), row(pool_scale[l]), bf(w_out[l]),
                       row(mix_post_g[l]), row(xa_pre_g[l]), bf(xa_wq[l]), k_mem, v_mem,
                       bf(xa_wo[l]), row(xa_post_g[l]))

        x = _ffn(x.reshape(b * s, d), row(ffn2_pre_g[l]), ffn2_w_gate[l], ffn2_w_up[l],
                 ffn2_w_down[l], row(ffn2_post_g[l])).reshape(b, s, d)
    return x
```

```python
import functools

import numpy as np
import jax
import jax.numpy as jnp
from jax import lax
from jax.experimental import pallas as pl
from jax.experimental.pallas import tpu as pltpu

N_ATTN_HEADS = 8
ATTN_HEAD_DIM = 64
D_ATTN = N_ATTN_HEADS * ATTN_HEAD_DIM
POOL_WINDOWS = (2, 4, 8, 16)
POOL_GROUP_DIM = 128
D_POOL = len(POOL_WINDOWS) * POOL_GROUP_DIM
MOBA_BLOCK = 256
MOBA_TOPK = 3
XA_HEADS = 4
MACARON_WEIGHT = 0.5
RMS_EPS = 1e-6
LOG2_E = 1.4426950408889634

V7X_LANES = 128
V7X_VMEM_BYTES = 64 * 1024 * 1024
VMEM_LIMIT_BYTES = 56 * 1024 * 1024

TOKEN_TILE = 512
HEADS_PER_STEP = V7X_LANES // ATTN_HEAD_DIM
PROJ_TOKEN_TILE = 1024
MIX_TOKEN_TILE = 1024
MIX_SUB_TILE = 256
WEIGHT_CHUNK_ROWS = 128
STAGE_SLOTS = 4
FFN_SUB_TILE = 256
POOL_HALO = max(POOL_WINDOWS)

_NT = (((1,), (1,)), ((), ()))


def _rmsnorm(x, g):
    r = lax.rsqrt(jnp.mean(x * x, axis=-1, keepdims=True) + RMS_EPS)
    return (x * r) * g


def _dot(a, b):
    return jnp.dot(a, b, preferred_element_type=jnp.float32)


def _params(*semantics):
    return pltpu.CompilerParams(dimension_semantics=semantics,
                                vmem_limit_bytes=VMEM_LIMIT_BYTES)


def _const_spec(shape):
    return pl.BlockSpec(shape, lambda *_: (0,) * len(shape))


def _stage_cast(src_hbm, dst_ref, stage_ref, sem):
    rows, cols = src_hbm.shape
    n_chunks = rows // WEIGHT_CHUNK_ROWS

    def chunk_copy(c):
        slot = c % STAGE_SLOTS
        return pltpu.make_async_copy(
            src_hbm.at[pl.ds(c * WEIGHT_CHUNK_ROWS, WEIGHT_CHUNK_ROWS), :],
            stage_ref.at[slot, :, pl.ds(0, cols)], sem.at[slot])

    ahead = STAGE_SLOTS - 1
    for c in range(min(ahead, n_chunks)):
        chunk_copy(c).start()
    for c in range(n_chunks):
        if c + ahead < n_chunks:
            chunk_copy(c + ahead).start()
        chunk_copy(c).wait()
        dst_ref[pl.ds(c * WEIGHT_CHUNK_ROWS, WEIGHT_CHUNK_ROWS), :] = (
            stage_ref[c % STAGE_SLOTS, :, pl.ds(0, cols)].astype(jnp.bfloat16))


def _ffn_kernel(x_ref, pre_g_ref, wg_hbm, wu_hbm, wd_hbm, post_g_ref, o_ref,
                wg_ref, wu_ref, wd_ref, stage_ref, sem):
    @pl.when(pl.program_id(0) == 0)
    def _():
        for src, dst in ((wg_hbm, wg_ref), (wu_hbm, wu_ref), (wd_hbm, wd_ref)):
            _stage_cast(src, dst, stage_ref, sem)

    subs = [slice(i * FFN_SUB_TILE, (i + 1) * FFN_SUB_TILE)
            for i in range(x_ref.shape[0] // FFN_SUB_TILE)]
    xs = [x_ref[rows, :] for rows in subs]
    hs = [_rmsnorm(x, pre_g_ref[...]).astype(jnp.bfloat16) for x in xs]
    gates = [_dot(h, wg_ref[...]) for h in hs]
    ups = [_dot(h, wu_ref[...]) for h in hs]
    acts = [(g * jax.nn.sigmoid(g) * u).astype(jnp.bfloat16) for g, u in zip(gates, ups)]
    fs = [_dot(a, wd_ref[...]) for a in acts]
    for rows, x, f in zip(subs, xs, fs):
        o_ref[rows, :] = x + MACARON_WEIGHT * _rmsnorm(f, post_g_ref[...])


def _ffn(x, pre_g, wg, wu, wd, post_g):
    n, d = x.shape
    d_ff = wg.shape[1]
    assert d % WEIGHT_CHUNK_ROWS == 0 and d_ff % WEIGHT_CHUNK_ROWS == 0
    tile = pl.BlockSpec((TOKEN_TILE, d), lambda i: (i, 0))
    hbm = pl.BlockSpec(memory_space=pl.ANY)
    return pl.pallas_call(
        _ffn_kernel,
        grid=(n // TOKEN_TILE,),
        in_specs=[tile, _const_spec((1, d)), hbm, hbm, hbm, _const_spec((1, d))],
        out_specs=tile,
        out_shape=jax.ShapeDtypeStruct((n, d), jnp.float32),
        scratch_shapes=[
            pltpu.VMEM((d, d_ff), jnp.bfloat16), pltpu.VMEM((d, d_ff), jnp.bfloat16),
            pltpu.VMEM((d_ff, d), jnp.bfloat16),
            pltpu.VMEM((STAGE_SLOTS, WEIGHT_CHUNK_ROWS, max(d, d_ff)), jnp.float32),
            pltpu.SemaphoreType.DMA((STAGE_SLOTS,)),
        ],
        compiler_params=_params("arbitrary"),
        name="ffn",
    )(x, pre_g, wg, wu, wd, post_g)


def _proj_kernel(x_ref, g_ref, wqt_ref, wk_ref, wvt_ref, wp_ref,
                 qt_ref, k_ref, kmean_ref, vt_ref, p_ref):
    blocks = [slice(i * MOBA_BLOCK, (i + 1) * MOBA_BLOCK) for i in range(PROJ_TOKEN_TILE // MOBA_BLOCK)]
    hs = [_rmsnorm(x_ref[0, rows, :], g_ref[...]).astype(jnp.bfloat16) for rows in blocks]
    qts = [lax.dot_general(wqt_ref[...], h, _NT, preferred_element_type=jnp.float32) for h in hs]
    ks = [_dot(h, wk_ref[...]) for h in hs]
    vts = [lax.dot_general(wvt_ref[...], h, _NT, preferred_element_type=jnp.float32) for h in hs]
    ps = [_dot(h, wp_ref[...]) for h in hs]
    for blk, rows in enumerate(blocks):
        qt_ref[0, blk] = (qts[blk] * (ATTN_HEAD_DIM ** -0.5 * LOG2_E)).astype(jnp.bfloat16)
        vt_ref[0, blk] = vts[blk].astype(jnp.bfloat16)
        k_ref[0, blk] = ks[blk].astype(jnp.bfloat16)
        kmean_ref[0, blk] = jnp.sum(ks[blk], axis=0, keepdims=True) * (1.0 / MOBA_BLOCK)
        p_ref[0, rows, :] = ps[blk]


def _proj(x, g, wqt, wk, wvt, wp):
    b, s, d = x.shape
    nb = s // MOBA_BLOCK
    bpt = PROJ_TOKEN_TILE // MOBA_BLOCK
    return pl.pallas_call(
        _proj_kernel,
        grid=(b, s // PROJ_TOKEN_TILE),
        in_specs=[pl.BlockSpec((1, PROJ_TOKEN_TILE, d), lambda bi, t: (bi, t, 0)),
                  _const_spec((1, d)), _const_spec((D_ATTN, d)),
                  _const_spec((d, D_ATTN)), _const_spec((D_ATTN, d)),
                  _const_spec((d, D_POOL))],
        out_specs=[
            pl.BlockSpec((1, bpt, D_ATTN, MOBA_BLOCK), lambda bi, t: (bi, t, 0, 0)),
            pl.BlockSpec((1, bpt, MOBA_BLOCK, D_ATTN), lambda bi, t: (bi, t, 0, 0)),
            pl.BlockSpec((1, bpt, 1, D_ATTN), lambda bi, t: (bi, t, 0, 0)),
            pl.BlockSpec((1, bpt, D_ATTN, MOBA_BLOCK), lambda bi, t: (bi, t, 0, 0)),
            pl.BlockSpec((1, PROJ_TOKEN_TILE, D_POOL), lambda bi, t: (bi, t, 0)),
        ],
        out_shape=[
            jax.ShapeDtypeStruct((b, nb, D_ATTN, MOBA_BLOCK), jnp.bfloat16),
            jax.ShapeDtypeStruct((b, nb, MOBA_BLOCK, D_ATTN), jnp.bfloat16),
            jax.ShapeDtypeStruct((b, nb, 1, D_ATTN), jnp.float32),
            jax.ShapeDtypeStruct((b, nb, D_ATTN, MOBA_BLOCK), jnp.bfloat16),
            jax.ShapeDtypeStruct((b, s, D_POOL), jnp.float32),
        ],
        compiler_params=_params("parallel", "parallel"),
        name="mixer_proj",
    )(x, g, wqt, wk, wvt, wp)


MOBA_TILE = 2 * MOBA_BLOCK
SELECT_UNROLL = 4
ONES_ROWS = 16


def _moba_items(n_tiles):
    tiles, pairs = [], []
    for c in range(n_tiles):
        for pr in range(c, -1, -1):
            tiles.append(c)
            pairs.append(pr)
    return np.asarray(tiles, np.int32), np.asarray(pairs, np.int32)


def _moba_kernel(slopes_ref, tile_tab_ref, pair_tab_ref, qt_ref, k_ref, kmean_ref, vt_ref,
                 o_ref, sel_ref, bias_ref, s_ref, p_ref, acc_ref):
    hp = pl.program_id(1)
    L, T = MOBA_BLOCK, MOBA_TILE
    nb = vt_ref.shape[1]
    n_tiles = nb // 2
    n_items = tile_tab_ref.shape[0]
    heads = range(HEADS_PER_STEP)
    slopes = [slopes_ref[hp * HEADS_PER_STEP + hh] for hh in heads]
    head_rows = [slice(hh * ATTN_HEAD_DIM, (hh + 1) * ATTN_HEAD_DIM) for hh in heads]
    head_row = lax.broadcasted_iota(jnp.int32, (HEADS_PER_STEP * ATTN_HEAD_DIM, T), 0) // ATTN_HEAD_DIM

    def query_tile(c):
        qt = jnp.concatenate([qt_ref[0, 2 * c], qt_ref[0, 2 * c + 1]], axis=1)
        return [jnp.where(head_row == hh, qt, jnp.zeros_like(qt)) for hh in heads]

    key = lax.broadcasted_iota(jnp.int32, (T, T), 0)
    query = lax.broadcasted_iota(jnp.int32, (T, T), 1)
    q_minus_k = (query - key).astype(jnp.float32)
    non_causal = jnp.where(key > query, jnp.inf, 0.0)
    for hh in heads:
        base = slopes[hh] * q_minus_k
        bias_ref[hh, 0] = base
        bias_ref[hh, 1] = base + non_causal

    kmean = kmean_ref[0].astype(jnp.bfloat16)
    blk_id = lax.broadcasted_iota(jnp.int32, (nb, T), 0)
    second_half = lax.broadcasted_iota(jnp.int32, (nb, T), 1) // L

    def select_tile(c):
        qt_heads = query_tile(c)
        own = 2 * c + second_half
        for hh in heads:
            gate = jnp.where(blk_id < own, _dot(kmean, qt_heads[hh]), -jnp.inf)
            sel = blk_id == own
            for _ in range(MOBA_TOPK):
                best = jnp.max(gate, axis=0, keepdims=True)
                first = jnp.min(jnp.where(gate == best, blk_id, nb), axis=0, keepdims=True)
                taken = blk_id == first
                sel = sel | (taken & (blk_id < own))
                gate = jnp.where(taken, -jnp.inf, gate)
            sel_ref[hh, c] = sel.astype(jnp.float32)

    def select(it, _):
        for u in range(SELECT_UNROLL):
            select_tile(SELECT_UNROLL * it + u)
        return 0

    lax.fori_loop(0, n_tiles // SELECT_UNROLL, select, 0)

    blk_rows = [slice(blk * L, (blk + 1) * L) for blk in range(2)]
    ones_rows = jnp.ones((ONES_ROWS, L), jnp.bfloat16)

    def biased_scores(e, slot):
        c, pr = tile_tab_ref[e], pair_tab_ref[e]
        variant = jnp.where(pr == c, 1, 0)
        k_pair = k_ref[0, pl.ds(pl.multiple_of(pr * T, T), T), :]
        col_max = []
        for hh, qt_h in enumerate(query_tile(c)):
            t = _dot(k_pair, qt_h) - bias_ref[hh, variant]
            s_ref[slot, hh] = t
            col_max.append(tuple(jnp.max(t[rows, :], axis=0, keepdims=True) for rows in blk_rows))
        return tuple(col_max)

    def value_product(pr, slot):
        out = []
        for hh in heads:
            lhs = [jnp.concatenate([vt_ref[0, 2 * pr + blk, head_rows[hh], :], ones_rows], axis=0)
                   for blk in range(2)]
            out.append(_dot(lhs[0], p_ref[slot, hh, blk_rows[0], :])
                       + _dot(lhs[1], p_ref[slot, hh, blk_rows[1], :]))
        return out

    def denominator(hh):
        d = acc_ref[hh, ATTN_HEAD_DIM:ATTN_HEAD_DIM + 1, :]
        return jnp.where(d == 0.0, 1.0, d)

    def write_tile(c):
        o_t = jnp.concatenate(
            [acc_ref[hh, :ATTN_HEAD_DIM, :] / denominator(hh) for hh in heads], axis=0)
        o_ref[0, pl.ds(pl.multiple_of(c * T, T), T), :] = o_t.T.astype(o_ref.dtype)

    def item(e, slot, state):
        m, alpha_lag1, alpha_lag2, col_max = state
        other = 1 - slot
        c, pr = tile_tab_ref[e], pair_tab_ref[e]
        e_lag2 = jnp.maximum(e - 2, 0)

        pv_lag2 = value_product(pair_tab_ref[e_lag2], slot)
        col_max_next = biased_scores(jnp.minimum(e + 1, n_items - 1), other)

        opens = pr == c
        m_out, alpha_out = [], []
        for hh in heads:
            m_old = jnp.where(opens, -jnp.inf, m[hh])
            offset = slopes[hh] * ((c - pr) * T).astype(jnp.float32)
            keep = [sel_ref[hh, c, pl.ds(2 * pr + blk, 1), :] > 0.5 for blk in range(2)]
            m_blk = [jnp.where(keep[blk], col_max[hh][blk], -jnp.inf) for blk in range(2)]
            m_new = jnp.maximum(m_old, jnp.maximum(m_blk[0], m_blk[1]) - offset)
            for blk in range(2):
                shift = jnp.where(keep[blk], m_new + offset, jnp.inf)
                p = jnp.exp2(s_ref[slot, hh, blk_rows[blk], :] - shift)
                p_ref[slot, hh, blk_rows[blk], :] = p.astype(jnp.bfloat16)
            m_out.append(m_new)
            alpha_out.append(jnp.exp2(m_old - m_new))

        fold(pv_lag2, alpha_lag2)
        write_tile(tile_tab_ref[e_lag2])

        return tuple(m_out), tuple(alpha_out), alpha_lag1, col_max_next

    def fold(pv, alpha):
        for hh in heads:
            acc_ref[hh] = alpha[hh] * acc_ref[hh] + pv[hh]

    col_max = biased_scores(0, 0)
    for hh in heads:
        acc_ref[hh] = jnp.zeros(acc_ref.shape[1:], jnp.float32)
    p_ref[...] = jnp.zeros(p_ref.shape, p_ref.dtype)
    row = lambda v: tuple(jnp.full((1, T), v, jnp.float32) for _ in heads)
    state = (row(-jnp.inf), row(0.0), row(0.0), col_max)

    def two_items(it, state):
        state = item(2 * it, 0, state)
        return item(2 * it + 1, 1, state)

    _, alpha_lag1, alpha_lag2, _ = lax.fori_loop(0, n_items // 2, two_items, state)

    for e, alpha in ((n_items - 2, alpha_lag2), (n_items - 1, alpha_lag1)):
        fold(value_product(pair_tab_ref[e], e % 2), alpha)
        write_tile(tile_tab_ref[e])


def _moba(slopes, qt, k, kmean, vt):
    b, nb, _, L = qt.shape
    lanes = HEADS_PER_STEP * ATTN_HEAD_DIM
    T = MOBA_TILE
    tile_tab, pair_tab = _moba_items(nb // 2)
    assert len(tile_tab) % 2 == 0
    smem = pl.BlockSpec(memory_space=pltpu.SMEM)
    return pl.pallas_call(
        _moba_kernel,
        grid=(b, N_ATTN_HEADS // HEADS_PER_STEP),
        in_specs=[
            smem, smem, smem,
            pl.BlockSpec((1, nb, lanes, L), lambda bi, hp: (bi, 0, hp, 0)),
            pl.BlockSpec((1, nb * L, lanes), lambda bi, hp: (bi, 0, hp)),
            pl.BlockSpec((1, nb, lanes), lambda bi, hp: (bi, 0, hp)),
            pl.BlockSpec((1, nb, lanes, L), lambda bi, hp: (bi, 0, hp, 0)),
        ],
        out_specs=pl.BlockSpec((1, nb * L, lanes), lambda bi, hp: (bi, 0, hp)),
        out_shape=jax.ShapeDtypeStruct((b, nb * L, D_ATTN), jnp.bfloat16),
        scratch_shapes=[
            pltpu.VMEM((HEADS_PER_STEP, nb // 2, nb, T), jnp.float32),
            pltpu.VMEM((HEADS_PER_STEP, 2, T, T), jnp.float32),
            pltpu.VMEM((2, HEADS_PER_STEP, T, T), jnp.float32),
            pltpu.VMEM((2, HEADS_PER_STEP, T, T), jnp.bfloat16),
            pltpu.VMEM((HEADS_PER_STEP, ATTN_HEAD_DIM + ONES_ROWS, T), jnp.float32),
        ],
        compiler_params=_params("parallel", "parallel"),
        name="moba_attention",
    )(slopes, jnp.asarray(tile_tab), jnp.asarray(pair_tab), qt, k, kmean, vt)


def _memkv_kernel(mem_ref, g_ref, wkv_ref, k_ref, v_ref):
    d = mem_ref.shape[2]
    mem_n = _rmsnorm(mem_ref[0], g_ref[...]).astype(jnp.bfloat16)
    kv = _dot(mem_n, wkv_ref[...])
    k_ref[0] = kv[:, :d].astype(jnp.bfloat16)
    v_ref[0] = kv[:, d:].astype(jnp.bfloat16)


def _memkv(mem, g, wkv):
    b, m, d = mem.shape
    blk = pl.BlockSpec((1, m, d), lambda bi: (bi, 0, 0))
    return pl.pallas_call(
        _memkv_kernel,
        grid=(b,),
        in_specs=[blk, _const_spec((1, d)), _const_spec(wkv.shape)],
        out_specs=[blk, blk],
        out_shape=[jax.ShapeDtypeStruct((b, m, d), jnp.bfloat16)] * 2,
        compiler_params=_params("parallel"),
        name="mem_kv",
    )(mem, g, wkv)


def _pooling_mixer(ext, p, first_pos, pool_w_ref, pool_scale_ref):
    rows = p.shape[0]
    pos = first_pos + lax.broadcasted_iota(jnp.int32, (rows, 1), 0)
    pooled = []
    for g, w in enumerate(POOL_WINDOWS):
        cols = slice(g * POOL_GROUP_DIM, (g + 1) * POOL_GROUP_DIM)
        acc = ext[:, cols]
        n_valid = acc.shape[0]
        span = 1
        while span < w:
            acc = acc[span:, :] + acc[:n_valid - span, :]
            n_valid -= span
            span *= 2
        win_sum = acc[n_valid - rows:, :]
        cnt = jnp.minimum(pos + 1, w).astype(jnp.float32)
        d = win_sum / cnt - p[:, cols]
        y = _dot(d.astype(jnp.bfloat16), pool_w_ref[g])
        pooled.append(y * pool_scale_ref[:, cols])
    return jnp.concatenate(pooled, axis=-1).astype(jnp.bfloat16)


def _mix_xattn_kernel(x_ref, attn_ref, p_ref, halo_ref, pool_w_ref, pool_scale_ref,
                      w_out_ref, mix_g_ref, pre_g_ref, wq_ref, k_ref, v_ref, wo_ref,
                      post_g_ref, o_ref):
    t = pl.program_id(1)
    tm, d = x_ref.shape[1], x_ref.shape[2]
    dh = d // XA_HEADS
    subs = [slice(i * MIX_SUB_TILE, (i + 1) * MIX_SUB_TILE) for i in range(tm // MIX_SUB_TILE)]
    halo = jnp.where(t > 0, halo_ref[0], 0.0)
    ext = jnp.concatenate([halo, p_ref[0]], axis=0)
    pools = [_pooling_mixer(ext[rows.start:rows.stop + POOL_HALO, :], p_ref[0, rows, :],
                            t * tm + rows.start, pool_w_ref, pool_scale_ref) for rows in subs]
    ys = [_dot(jnp.concatenate([attn_ref[0, rows, :], pool], axis=-1), w_out_ref[...])
          for rows, pool in zip(subs, pools)]
    xs = [x_ref[0, rows, :] + _rmsnorm(y, mix_g_ref[...]) for rows, y in zip(subs, ys)]

    hs = [_rmsnorm(x, pre_g_ref[...]).astype(jnp.bfloat16) for x in xs]
    qs = [_dot(h, wq_ref[...]).astype(jnp.bfloat16) for h in hs]
    heads = [[] for _ in subs]
    for hd in range(XA_HEADS):
        cols = slice(hd * dh, (hd + 1) * dh)
        ss = [lax.dot_general(q[:, cols], k_ref[0, :, cols], _NT,
                              preferred_element_type=jnp.float32) * (dh ** -0.5) for q in qs]
        for i, s in enumerate(ss):
            e = jnp.exp(s - jnp.max(s, axis=-1, keepdims=True))
            p = e / jnp.sum(e, axis=-1, keepdims=True)
            heads[i].append(_dot(p.astype(jnp.bfloat16), v_ref[0, :, cols]))
    os = [jnp.concatenate(hl, axis=-1).astype(jnp.bfloat16) for hl in heads]
    cs = [_dot(o, wo_ref[...]) for o in os]
    for rows, x, c in zip(subs, xs, cs):
        o_ref[0, rows, :] = x + _rmsnorm(c, post_g_ref[...])


def _mix_xattn(x, attn, p_in, pool_w, pool_scale, w_out, mix_g, pre_g, wq, k_mem, v_mem, wo,
               post_g):
    b, s, d = x.shape
    m = k_mem.shape[1]
    halo_blocks = MIX_TOKEN_TILE // POOL_HALO
    tile = lambda width: pl.BlockSpec((1, MIX_TOKEN_TILE, width), lambda bi, t: (bi, t, 0))
    mem_blk = pl.BlockSpec((1, m, d), lambda bi, t: (bi, 0, 0))
    return pl.pallas_call(
        _mix_xattn_kernel,
        grid=(b, s // MIX_TOKEN_TILE),
        in_specs=[
            tile(d), tile(D_ATTN), tile(D_POOL),
            pl.BlockSpec((1, POOL_HALO, D_POOL),
                         lambda bi, t: (bi, jnp.maximum(t * halo_blocks - 1, 0), 0)),
            _const_spec(pool_w.shape), _const_spec((1, D_POOL)),
            _const_spec(w_out.shape), _const_spec((1, d)),
            _const_spec((1, d)), _const_spec(wq.shape), mem_blk, mem_blk,
            _const_spec(wo.shape), _const_spec((1, d)),
        ],
        out_specs=tile(d),
        out_shape=jax.ShapeDtypeStruct((b, s, d), jnp.float32),
        compiler_params=_params("parallel", "parallel"),
        name="mixer_out_mem_xattn",
    )(x, attn, p_in, p_in, pool_w, pool_scale, w_out, mix_g, pre_g, wq, k_mem, v_mem, wo, post_g)


def _alibi_slopes(n_heads):
    return jnp.asarray(2.0 ** (-8.0 * np.arange(1, n_heads + 1) / n_heads), jnp.float32)


def kernel(x, mem, ffn1_pre_g, ffn1_w_gate, ffn1_w_up, ffn1_w_down, ffn1_post_g, mix_pre_g, w_in, pool_w, pool_scale, w_out, mix_post_g, xa_pre_g, mem_g, xa_wq, xa_wkv, xa_wo, xa_post_g, ffn2_pre_g, ffn2_w_gate, ffn2_w_up, ffn2_w_down, ffn2_post_g):
    b, s, d = x.shape
    depth = ffn1_pre_g.shape[0]
    bf = lambda w: w.astype(jnp.bfloat16)
    row = lambda g: g.reshape(1, -1)
    slopes = _alibi_slopes(N_ATTN_HEADS) * LOG2_E
    for l in range(depth):
        x = _ffn(x.reshape(b * s, d), row(ffn1_pre_g[l]), ffn1_w_gate[l], ffn1_w_up[l],
                 ffn1_w_down[l], row(ffn1_post_g[l])).reshape(b, s, d)

        w = bf(w_in[l])
        qt, k, kmean, vt, p_in = _proj(
            x, row(mix_pre_g[l]), w[:, :D_ATTN].T, w[:, D_ATTN:2 * D_ATTN],
            w[:, 2 * D_ATTN:3 * D_ATTN].T, w[:, 3 * D_ATTN:])
        attn = _moba(slopes, qt, k.reshape(b, s, D_ATTN), kmean.reshape(b, -1, D_ATTN), vt)
        k_mem, v_mem = _memkv(mem, row(mem_g[l]), bf(xa_wkv[l]))
        x = _mix_xattn(x, attn, p_in, bf(pool_w[l]), row(pool_scale[l]), bf(w_out[l]),
                       row(mix_post_g[l]), row(xa_pre_g[l]), bf(xa_wq[l]), k_mem, v_mem,
                       bf(xa_wo[l]), row(xa_post_g[l]))

        x = _ffn(x.reshape(b * s, d), row(ffn2_pre_g[l]), ffn2_w_gate[l], ffn2_w_up[l],
                 ffn2_w_down[l], row(ffn2_post_g[l])).reshape(b, s, d)
    return x
```

```python
import functools

import numpy as np
import jax
import jax.numpy as jnp
from jax import lax
from jax.experimental import pallas as pl
from jax.experimental.pallas import tpu as pltpu

N_ATTN_HEADS = 8
ATTN_HEAD_DIM = 64
D_ATTN = N_ATTN_HEADS * ATTN_HEAD_DIM
POOL_WINDOWS = (2, 4, 8, 16)
POOL_GROUP_DIM = 128
D_POOL = len(POOL_WINDOWS) * POOL_GROUP_DIM
MOBA_BLOCK = 256
MOBA_TOPK = 3
XA_HEADS = 4
MACARON_WEIGHT = 0.5
RMS_EPS = 1e-6
LOG2_E = 1.4426950408889634

V7X_LANES = 128
V7X_VMEM_BYTES = 64 * 1024 * 1024
VMEM_LIMIT_BYTES = 56 * 1024 * 1024

TOKEN_TILE = 512
HEADS_PER_STEP = V7X_LANES // ATTN_HEAD_DIM
PROJ_TOKEN_TILE = 1024
MIX_TOKEN_TILE = 1024
MIX_SUB_TILE = 256
WEIGHT_CHUNK_ROWS = 128
STAGE_SLOTS = 4
FFN_SUB_TILE = 256
POOL_HALO = max(POOL_WINDOWS)

_NT = (((1,), (1,)), ((), ()))


def _rmsnorm(x, g):
    r = lax.rsqrt(jnp.mean(x * x, axis=-1, keepdims=True) + RMS_EPS)
    return (x * r) * g


def _dot(a, b):
    return jnp.dot(a, b, preferred_element_type=jnp.float32)


def _params(*semantics):
    return pltpu.CompilerParams(dimension_semantics=semantics,
                                vmem_limit_bytes=VMEM_LIMIT_BYTES)


def _const_spec(shape):
    return pl.BlockSpec(shape, lambda *_: (0,) * len(shape))


def _stage_cast(src_hbm, dst_ref, stage_ref, sem):
    rows, cols = src_hbm.shape
    n_chunks = rows // WEIGHT_CHUNK_ROWS

    def chunk_copy(c):
        slot = c % STAGE_SLOTS
        return pltpu.make_async_copy(
            src_hbm.at[pl.ds(c * WEIGHT_CHUNK_ROWS, WEIGHT_CHUNK_ROWS), :],
            stage_ref.at[slot, :, pl.ds(0, cols)], sem.at[slot])

    ahead = STAGE_SLOTS - 1
    for c in range(min(ahead, n_chunks)):
        chunk_copy(c).start(priority=c % 2)
    for c in range(n_chunks):
        if c + ahead < n_chunks:
            chunk_copy(c + ahead).start(priority=(c + ahead) % 2)
        chunk_copy(c).wait()
        dst_ref[pl.ds(c * WEIGHT_CHUNK_ROWS, WEIGHT_CHUNK_ROWS), :] = (
            stage_ref[c % STAGE_SLOTS, :, pl.ds(0, cols)].astype(jnp.bfloat16))


def _ffn_kernel(x_ref, pre_g_ref, wg_hbm, wu_hbm, wd_hbm, post_g_ref, o_ref,
                wg_ref, wu_ref, wd_ref, stage_ref, sem):
    @pl.when(pl.program_id(0) == 0)
    def _():
        for src, dst in ((wg_hbm, wg_ref), (wu_hbm, wu_ref), (wd_hbm, wd_ref)):
            _stage_cast(src, dst, stage_ref, sem)

    subs = [slice(i * FFN_SUB_TILE, (i + 1) * FFN_SUB_TILE)
            for i in range(x_ref.shape[0] // FFN_SUB_TILE)]
    xs = [x_ref[rows, :] for rows in subs]
    hs = [_rmsnorm(x, pre_g_ref[...]).astype(jnp.bfloat16) for x in xs]
    gates = [_dot(h, wg_ref[...]) for h in hs]
    ups = [_dot(h, wu_ref[...]) for h in hs]
    acts = [(g * jax.nn.sigmoid(g) * u).astype(jnp.bfloat16) for g, u in zip(gates, ups)]
    fs = [_dot(a, wd_ref[...]) for a in acts]
    for rows, x, f in zip(subs, xs, fs):
        o_ref[rows, :] = x + MACARON_WEIGHT * _rmsnorm(f, post_g_ref[...])


def _ffn(x, pre_g, wg, wu, wd, post_g):
    n, d = x.shape
    d_ff = wg.shape[1]
    assert d % WEIGHT_CHUNK_ROWS == 0 and d_ff % WEIGHT_CHUNK_ROWS == 0
    tile = pl.BlockSpec((TOKEN_TILE, d), lambda i: (i, 0))
    hbm = pl.BlockSpec(memory_space=pl.ANY)
    return pl.pallas_call(
        _ffn_kernel,
        grid=(n // TOKEN_TILE,),
        in_specs=[tile, _const_spec((1, d)), hbm, hbm, hbm, _const_spec((1, d))],
        out_specs=tile,
        out_shape=jax.ShapeDtypeStruct((n, d), jnp.float32),
        scratch_shapes=[
            pltpu.VMEM((d, d_ff), jnp.bfloat16), pltpu.VMEM((d, d_ff), jnp.bfloat16),
            pltpu.VMEM((d_ff, d), jnp.bfloat16),
            pltpu.VMEM((STAGE_SLOTS, WEIGHT_CHUNK_ROWS, max(d, d_ff)), jnp.float32),
            pltpu.SemaphoreType.DMA((STAGE_SLOTS,)),
        ],
        compiler_params=_params("arbitrary"),
        name="ffn",
    )(x, pre_g, wg, wu, wd, post_g)


def _proj_kernel(x_ref, g_ref, wqt_ref, wk_ref, wvt_ref, wp_ref,
                 qt_ref, k_ref, kmean_ref, vt_ref, p_ref):
    blocks = [slice(i * MOBA_BLOCK, (i + 1) * MOBA_BLOCK) for i in range(PROJ_TOKEN_TILE // MOBA_BLOCK)]
    hs = [_rmsnorm(x_ref[0, rows, :], g_ref[...]).astype(jnp.bfloat16) for rows in blocks]
    qts = [lax.dot_general(wqt_ref[...], h, _NT, preferred_element_type=jnp.float32) for h in hs]
    ks = [_dot(h, wk_ref[...]) for h in hs]
    vts = [lax.dot_general(wvt_ref[...], h, _NT, preferred_element_type=jnp.float32) for h in hs]
    ps = [_dot(h, wp_ref[...]) for h in hs]
    for blk, rows in enumerate(blocks):
        qt_ref[0, blk] = (qts[blk] * (ATTN_HEAD_DIM ** -0.5 * LOG2_E)).astype(jnp.bfloat16)
        vt_ref[0, blk] = vts[blk].astype(jnp.bfloat16)
        k_ref[0, blk] = ks[blk].astype(jnp.bfloat16)
        kmean_ref[0, blk] = jnp.sum(ks[blk], axis=0, keepdims=True) * (1.0 / MOBA_BLOCK)
        p_ref[0, rows, :] = ps[blk]


def _proj(x, g, wqt, wk, wvt, wp):
    b, s, d = x.shape
    nb = s // MOBA_BLOCK
    bpt = PROJ_TOKEN_TILE // MOBA_BLOCK
    return pl.pallas_call(
        _proj_kernel,
        grid=(b, s // PROJ_TOKEN_TILE),
        in_specs=[pl.BlockSpec((1, PROJ_TOKEN_TILE, d), lambda bi, t: (bi, t, 0)),
                  _const_spec((1, d)), _const_spec((D_ATTN, d)),
                  _const_spec((d, D_ATTN)), _const_spec((D_ATTN, d)),
                  _const_spec((d, D_POOL))],
        out_specs=[
            pl.BlockSpec((1, bpt, D_ATTN, MOBA_BLOCK), lambda bi, t: (bi, t, 0, 0)),
            pl.BlockSpec((1, bpt, MOBA_BLOCK, D_ATTN), lambda bi, t: (bi, t, 0, 0)),
            pl.BlockSpec((1, bpt, 1, D_ATTN), lambda bi, t: (bi, t, 0, 0)),
            pl.BlockSpec((1, bpt, D_ATTN, MOBA_BLOCK), lambda bi, t: (bi, t, 0, 0)),
            pl.BlockSpec((1, PROJ_TOKEN_TILE, D_POOL), lambda bi, t: (bi, t, 0)),
        ],
        out_shape=[
            jax.ShapeDtypeStruct((b, nb, D_ATTN, MOBA_BLOCK), jnp.bfloat16),
            jax.ShapeDtypeStruct((b, nb, MOBA_BLOCK, D_ATTN), jnp.bfloat16),
            jax.ShapeDtypeStruct((b, nb, 1, D_ATTN), jnp.float32),
            jax.ShapeDtypeStruct((b, nb, D_ATTN, MOBA_BLOCK), jnp.bfloat16),
            jax.ShapeDtypeStruct((b, s, D_POOL), jnp.float32),
        ],
        compiler_params=_params("parallel", "parallel"),
        name="mixer_proj",
    )(x, g, wqt, wk, wvt, wp)


MOBA_TILE = 2 * MOBA_BLOCK
SELECT_UNROLL = 4
ONES_ROWS = 16


def _moba_items(n_tiles):
    tiles, pairs = [], []
    for c in range(n_tiles):
        for pr in range(c, -1, -1):
            tiles.append(c)
            pairs.append(pr)
    return np.asarray(tiles, np.int32), np.asarray(pairs, np.int32)


def _moba_kernel(slopes_ref, tile_tab_ref, pair_tab_ref, qt_ref, k_ref, kmean_ref, vt_ref,
                 o_ref, sel_ref, bias_ref, s_ref, p_ref, acc_ref):
    hp = pl.program_id(1)
    L, T = MOBA_BLOCK, MOBA_TILE
    nb = vt_ref.shape[1]
    n_tiles = nb // 2
    n_items = tile_tab_ref.shape[0]
    heads = range(HEADS_PER_STEP)
    slopes = [slopes_ref[hp * HEADS_PER_STEP + hh] for hh in heads]
    head_rows = [slice(hh * ATTN_HEAD_DIM, (hh + 1) * ATTN_HEAD_DIM) for hh in heads]
    head_row = lax.broadcasted_iota(jnp.int32, (HEADS_PER_STEP * ATTN_HEAD_DIM, T), 0) // ATTN_HEAD_DIM

    def query_tile(c):
        qt = jnp.concatenate([qt_ref[0, 2 * c], qt_ref[0, 2 * c + 1]], axis=1)
        return [jnp.where(head_row == hh, qt, jnp.zeros_like(qt)) for hh in heads]

    key = lax.broadcasted_iota(jnp.int32, (T, T), 0)
    query = lax.broadcasted_iota(jnp.int32, (T, T), 1)
    q_minus_k = (query - key).astype(jnp.float32)
    non_causal = jnp.where(key > query, jnp.inf, 0.0)
    for hh in heads:
        base = slopes[hh] * q_minus_k
        bias_ref[hh, 0] = base
        bias_ref[hh, 1] = base + non_causal

    kmean = kmean_ref[0].astype(jnp.bfloat16)
    blk_id = lax.broadcasted_iota(jnp.int32, (nb, T), 0)
    second_half = lax.broadcasted_iota(jnp.int32, (nb, T), 1) // L

    def select_tile(c):
        qt_heads = query_tile(c)
        own = 2 * c + second_half
        for hh in heads:
            gate = jnp.where(blk_id < own, _dot(kmean, qt_heads[hh]), -jnp.inf)
            sel = blk_id == own
            for _ in range(MOBA_TOPK):
                best = jnp.max(gate, axis=0, keepdims=True)
                first = jnp.min(jnp.where(gate == best, blk_id, nb), axis=0, keepdims=True)
                taken = blk_id == first
                sel = sel | (taken & (blk_id < own))
                gate = jnp.where(taken, -jnp.inf, gate)
            sel_ref[hh, c] = sel.astype(jnp.float32)

    def select(it, _):
        for u in range(SELECT_UNROLL):
            select_tile(SELECT_UNROLL * it + u)
        return 0

    lax.fori_loop(0, n_tiles // SELECT_UNROLL, select, 0)

    blk_rows = [slice(blk * L, (blk + 1) * L) for blk in range(2)]
    ones_rows = jnp.ones((ONES_ROWS, L), jnp.bfloat16)

    def biased_scores(e, slot):
        c, pr = tile_tab_ref[e], pair_tab_ref[e]
        variant = jnp.where(pr == c, 1, 0)
        k_pair = k_ref[0, pl.ds(pl.multiple_of(pr * T, T), T), :]
        col_max = []
        for hh, qt_h in enumerate(query_tile(c)):
            t = _dot(k_pair, qt_h) - bias_ref[hh, variant]
            s_ref[slot, hh] = t
            col_max.append(tuple(jnp.max(t[rows, :], axis=0, keepdims=True) for rows in blk_rows))
        return tuple(col_max)

    def value_product(pr, slot):
        out = []
        for hh in heads:
            lhs = [jnp.concatenate([vt_ref[0, 2 * pr + blk, head_rows[hh], :], ones_rows], axis=0)
                   for blk in range(2)]
            out.append(_dot(lhs[0], p_ref[slot, hh, blk_rows[0], :])
                       + _dot(lhs[1], p_ref[slot, hh, blk_rows[1], :]))
        return out

    def denominator(hh):
        d = acc_ref[hh, ATTN_HEAD_DIM:ATTN_HEAD_DIM + 1, :]
        return jnp.where(d == 0.0, 1.0, d)

    def write_tile(c):
        o_t = jnp.concatenate(
            [acc_ref[hh, :ATTN_HEAD_DIM, :] / denominator(hh) for hh in heads], axis=0)
        o_ref[0, pl.ds(pl.multiple_of(c * T, T), T), :] = o_t.T.astype(o_ref.dtype)

    def item(e, slot, state):
        m, alpha_lag1, alpha_lag2, col_max = state
        other = 1 - slot
        c, pr = tile_tab_ref[e], pair_tab_ref[e]
        e_lag2 = jnp.maximum(e - 2, 0)

        pv_lag2 = value_product(pair_tab_ref[e_lag2], slot)
        col_max_next = biased_scores(jnp.minimum(e + 1, n_items - 1), other)

        opens = pr == c
        m_out, alpha_out = [], []
        for hh in heads:
            m_old = jnp.where(opens, -jnp.inf, m[hh])
            offset = slopes[hh] * ((c - pr) * T).astype(jnp.float32)
            keep = [sel_ref[hh, c, pl.ds(2 * pr + blk, 1), :] > 0.5 for blk in range(2)]
            m_blk = [jnp.where(keep[blk], col_max[hh][blk], -jnp.inf) for blk in range(2)]
            m_new = jnp.maximum(m_old, jnp.maximum(m_blk[0], m_blk[1]) - offset)
            for blk in range(2):
                shift = jnp.where(keep[blk], m_new + offset, jnp.inf)
                p = jnp.exp2(s_ref[slot, hh, blk_rows[blk], :] - shift)
                p_ref[slot, hh, blk_rows[blk], :] = p.astype(jnp.bfloat16)
            m_out.append(m_new)
            alpha_out.append(jnp.exp2(m_old - m_new))

        fold(pv_lag2, alpha_lag2)
        write_tile(tile_tab_ref[e_lag2])

        return tuple(m_out), tuple(alpha_out), alpha_lag1, col_max_next

    def fold(pv, alpha):
        for hh in heads:
            acc_ref[hh] = alpha[hh] * acc_ref[hh] + pv[hh]

    col_max = biased_scores(0, 0)
    for hh in heads:
        acc_ref[hh] = jnp.zeros(acc_ref.shape[1:], jnp.float32)
    p_ref[...] = jnp.zeros(p_ref.shape, p_ref.dtype)
    row = lambda v: tuple(jnp.full((1, T), v, jnp.float32) for _ in heads)
    state = (row(-jnp.inf), row(0.0), row(0.0), col_max)

    def two_items(it, state):
        state = item(2 * it, 0, state)
        return item(2 * it + 1, 1, state)

    _, alpha_lag1, alpha_lag2, _ = lax.fori_loop(0, n_items // 2, two_items, state)

    for e, alpha in ((n_items - 2, alpha_lag2), (n_items - 1, alpha_lag1)):
        fold(value_product(pair_tab_ref[e], e % 2), alpha)
        write_tile(tile_tab_ref[e])


def _moba(slopes, qt, k, kmean, vt):
    b, nb, _, L = qt.shape
    lanes = HEADS_PER_STEP * ATTN_HEAD_DIM
    T = MOBA_TILE
    tile_tab, pair_tab = _moba_items(nb // 2)
    assert len(tile_tab) % 2 == 0
    smem = pl.BlockSpec(memory_space=pltpu.SMEM)
    return pl.pallas_call(
        _moba_kernel,
        grid=(b, N_ATTN_HEADS // HEADS_PER_STEP),
        in_specs=[
            smem, smem, smem,
            pl.BlockSpec((1, nb, lanes, L), lambda bi, hp: (bi, 0, hp, 0)),
            pl.BlockSpec((1, nb * L, lanes), lambda bi, hp: (bi, 0, hp)),
            pl.BlockSpec((1, nb, lanes), lambda bi, hp: (bi, 0, hp)),
            pl.BlockSpec((1, nb, lanes, L), lambda bi, hp: (bi, 0, hp, 0)),
        ],
        out_specs=pl.BlockSpec((1, nb * L, lanes), lambda bi, hp: (bi, 0, hp)),
        out_shape=jax.ShapeDtypeStruct((b, nb * L, D_ATTN), jnp.bfloat16),
        scratch_shapes=[
            pltpu.VMEM((HEADS_PER_STEP, nb // 2, nb, T), jnp.float32),
            pltpu.VMEM((HEADS_PER_STEP, 2, T, T), jnp.float32),
            pltpu.VMEM((2, HEADS_PER_STEP, T, T), jnp.float32),
            pltpu.VMEM((2, HEADS_PER_STEP, T, T), jnp.bfloat16),
            pltpu.VMEM((HEADS_PER_STEP, ATTN_HEAD_DIM + ONES_ROWS, T), jnp.float32),
        ],
        compiler_params=_params("parallel", "parallel"),
        name="moba_attention",
    )(slopes, jnp.asarray(tile_tab), jnp.asarray(pair_tab), qt, k, kmean, vt)


def _memkv_kernel(mem_ref, g_ref, wkv_ref, k_ref, v_ref):
    d = mem_ref.shape[2]
    mem_n = _rmsnorm(mem_ref[0], g_ref[...]).astype(jnp.bfloat16)
    kv = _dot(mem_n, wkv_ref[...])
    k_ref[0] = kv[:, :d].astype(jnp.bfloat16)
    v_ref[0] = kv[:, d:].astype(jnp.bfloat16)


def _memkv(mem, g, wkv):
    b, m, d = mem.shape
    blk = pl.BlockSpec((1, m, d), lambda bi: (bi, 0, 0))
    return pl.pallas_call(
        _memkv_kernel,
        grid=(b,),
        in_specs=[blk, _const_spec((1, d)), _const_spec(wkv.shape)],
        out_specs=[blk, blk],
        out_shape=[jax.ShapeDtypeStruct((b, m, d), jnp.bfloat16)] * 2,
        compiler_params=_params("parallel"),
        name="mem_kv",
    )(mem, g, wkv)


def _pooling_mixer(ext, p, first_pos, pool_w_ref, pool_scale_ref):
    rows = p.shape[0]
    pos = first_pos + lax.broadcasted_iota(jnp.int32, (rows, 1), 0)
    pooled = []
    for g, w in enumerate(POOL_WINDOWS):
        cols = slice(g * POOL_GROUP_DIM, (g + 1) * POOL_GROUP_DIM)
        acc = ext[:, cols]
        n_valid = acc.shape[0]
        span = 1
        while span < w:
            acc = acc[span:, :] + acc[:n_valid - span, :]
            n_valid -= span
            span *= 2
        win_sum = acc[n_valid - rows:, :]
        cnt = jnp.minimum(pos + 1, w).astype(jnp.float32)
        d = win_sum / cnt - p[:, cols]
        y = _dot(d.astype(jnp.bfloat16), pool_w_ref[g])
        pooled.append(y * pool_scale_ref[:, cols])
    return jnp.concatenate(pooled, axis=-1).astype(jnp.bfloat16)


def _mix_xattn_kernel(x_ref, attn_ref, p_ref, halo_ref, pool_w_ref, pool_scale_ref,
                      w_out_ref, mix_g_ref, pre_g_ref, wq_ref, k_ref, v_ref, wo_ref,
                      post_g_ref, o_ref):
    t = pl.program_id(1)
    tm, d = x_ref.shape[1], x_ref.shape[2]
    dh = d // XA_HEADS
    subs = [slice(i * MIX_SUB_TILE, (i + 1) * MIX_SUB_TILE) for i in range(tm // MIX_SUB_TILE)]
    halo = jnp.where(t > 0, halo_ref[0], 0.0)
    ext = jnp.concatenate([halo, p_ref[0]], axis=0)
    pools = [_pooling_mixer(ext[rows.start:rows.stop + POOL_HALO, :], p_ref[0, rows, :],
                            t * tm + rows.start, pool_w_ref, pool_scale_ref) for rows in subs]
    ys = [_dot(jnp.concatenate([attn_ref[0, rows, :], pool], axis=-1), w_out_ref[...])
          for rows, pool in zip(subs, pools)]
    xs = [x_ref[0, rows, :] + _rmsnorm(y, mix_g_ref[...]) for rows, y in zip(subs, ys)]

    hs = [_rmsnorm(x, pre_g_ref[...]).astype(jnp.bfloat16) for x in xs]
    qs = [_dot(h, wq_ref[...]).astype(jnp.bfloat16) for h in hs]
    heads, cs = [[] for _ in subs], []
    head_cols = [slice(hd * dh, (hd + 1) * dh) for hd in range(XA_HEADS)]
    ss = [[lax.dot_general(q[:, cols], k_ref[0, :, cols], _NT,
                           preferred_element_type=jnp.float32) * (dh ** -0.5) for q in qs]
          for cols in head_cols]
    for i in range(len(subs)):
        for hd, cols in enumerate(head_cols):
            s = ss[hd][i]
            e = jnp.exp(s - jnp.max(s, axis=-1, keepdims=True))
            p = e / jnp.sum(e, axis=-1, keepdims=True)
            heads[i].append(_dot(p.astype(jnp.bfloat16), v_ref[0, :, cols]))
        cs.append(_dot(jnp.concatenate(heads[i], axis=-1).astype(jnp.bfloat16), wo_ref[...]))
    for rows, x, c in zip(subs, xs, cs):
        o_ref[0, rows, :] = x + _rmsnorm(c, post_g_ref[...])


def _mix_xattn(x, attn, p_in, pool_w, pool_scale, w_out, mix_g, pre_g, wq, k_mem, v_mem, wo,
               post_g):
    b, s, d = x.shape
    m = k_mem.shape[1]
    halo_blocks = MIX_TOKEN_TILE // POOL_HALO
    tile = lambda width: pl.BlockSpec((1, MIX_TOKEN_TILE, width), lambda bi, t: (bi, t, 0))
    mem_blk = pl.BlockSpec((1, m, d), lambda bi, t: (bi, 0, 0))
    return pl.pallas_call(
        _mix_xattn_kernel,
        grid=(b, s // MIX_TOKEN_TILE),
        in_specs=[
            tile(d), tile(D_ATTN), tile(D_POOL),
            pl.BlockSpec((1, POOL_HALO, D_POOL),
                         lambda bi, t: (bi, jnp.maximum(t * halo_blocks - 1, 0), 0)),
            _const_spec(pool_w.shape), _const_spec((1, D_POOL)),
            _const_spec(w_out.shape), _const_spec((1, d)),
            _const_spec((1, d)), _const_spec(wq.shape), mem_blk, mem_blk,
            _const_spec(wo.shape), _const_spec((1, d)),
        ],
        out_specs=tile(d),
        out_shape=jax.ShapeDtypeStruct((b, s, d), jnp.float32),
        compiler_params=_params("parallel", "parallel"),
        name="mixer_out_mem_xattn",
    )(x, attn, p_in, p_in, pool_w, pool_scale, w_out, mix_g, pre_g, wq, k_mem, v_mem, wo, post_g)


def _alibi_slopes(n_heads):
    return jnp.asarray(2.0 ** (-8.0 * np.arange(1, n_heads + 1) / n_heads), jnp.float32)


def kernel(x, mem, ffn1_pre_g, ffn1_w_gate, ffn1_w_up, ffn1_w_down, ffn1_post_g, mix_pre_g, w_in, pool_w, pool_scale, w_out, mix_post_g, xa_pre_g, mem_g, xa_wq, xa_wkv, xa_wo, xa_post_g, ffn2_pre_g, ffn2_w_gate, ffn2_w_up, ffn2_w_down, ffn2_post_g):
    b, s, d = x.shape
    depth = ffn1_pre_g.shape[0]
    bf = lambda w: w.astype(jnp.bfloat16)
    row = lambda g: g.reshape(1, -1)
    slopes = _alibi_slopes(N_ATTN_HEADS) * LOG2_E
    for l in range(depth):
        x = _ffn(x.reshape(b * s, d), row(ffn1_pre_g[l]), ffn1_w_gate[l], ffn1_w_up[l],
                 ffn1_w_down[l], row(ffn1_post_g[l])).reshape(b, s, d)

        w = bf(w_in[l])
        qt, k, kmean, vt, p_in = _proj(
            x, row(mix_pre_g[l]), w[:, :D_ATTN].T, w[:, D_ATTN:2 * D_ATTN],
            w[:, 2 * D_ATTN:3 * D_ATTN].T, w[:, 3 * D_ATTN:])
        attn = _moba(slopes, qt, k.reshape(b, s, D_ATTN), kmean.reshape(b, -1, D_ATTN), vt)
        k_mem, v_mem = _memkv(mem, row(mem_g[l]), bf(xa_wkv[l]))
        x = _mix_xattn(x, attn, p_in, bf(pool_w[l]), row(pool_scale[l]), bf(w_out[l]),
                       row(mix_post_g[l]), row(xa_pre_g[l]), bf(xa_wq[l]), k_mem, v_mem,
                       bf(xa_wo[l]), row(xa_post_g[l]))

        x = _ffn(x.reshape(b * s, d), row(ffn2_pre_g[l]), ffn2_w_gate[l], ffn2_w_up[l],
                 ffn2_w_down[l], row(ffn2_post_g[l])).reshape(b, s, d)
    return x
```

```python
import functools

import numpy as np
import jax
import jax.numpy as jnp
from jax import lax
from jax.experimental import pallas as pl
from jax.experimental.pallas import tpu as pltpu

N_ATTN_HEADS = 8
ATTN_HEAD_DIM = 64
D_ATTN = N_ATTN_HEADS * ATTN_HEAD_DIM
POOL_WINDOWS = (2, 4, 8, 16)
POOL_GROUP_DIM = 128
D_POOL = len(POOL_WINDOWS) * POOL_GROUP_DIM
MOBA_BLOCK = 256
MOBA_TOPK = 3
XA_HEADS = 4
MACARON_WEIGHT = 0.5
RMS_EPS = 1e-6
LOG2_E = 1.4426950408889634

V7X_LANES = 128
V7X_VMEM_BYTES = 64 * 1024 * 1024
VMEM_LIMIT_BYTES = 56 * 1024 * 1024

TOKEN_TILE = 512
HEADS_PER_STEP = V7X_LANES // ATTN_HEAD_DIM
PROJ_TOKEN_TILE = 1024
MIX_TOKEN_TILE = 1024
MIX_SUB_TILE = 256
WEIGHT_CHUNK_ROWS = 128
STAGE_SLOTS = 4
FFN_SUB_TILE = 256
POOL_HALO = max(POOL_WINDOWS)

_NT = (((1,), (1,)), ((), ()))


def _rmsnorm(x, g):
    r = lax.rsqrt(jnp.mean(x * x, axis=-1, keepdims=True) + RMS_EPS)
    return (x * r) * g


def _dot(a, b):
    return jnp.dot(a, b, preferred_element_type=jnp.float32)


def _params(*semantics):
    return pltpu.CompilerParams(dimension_semantics=semantics,
                                vmem_limit_bytes=VMEM_LIMIT_BYTES)


def _const_spec(shape):
    return pl.BlockSpec(shape, lambda *_: (0,) * len(shape))


def _stage_cast(src_hbm, dst_ref, stage_ref, sem):
    rows, cols = src_hbm.shape
    n_chunks = rows // WEIGHT_CHUNK_ROWS

    def chunk_copy(c):
        slot = c % STAGE_SLOTS
        return pltpu.make_async_copy(
            src_hbm.at[pl.ds(c * WEIGHT_CHUNK_ROWS, WEIGHT_CHUNK_ROWS), :],
            stage_ref.at[slot, :, pl.ds(0, cols)], sem.at[slot])

    ahead = STAGE_SLOTS - 1
    for c in range(min(ahead, n_chunks)):
        chunk_copy(c).start()
    for c in range(n_chunks):
        if c + ahead < n_chunks:
            chunk_copy(c + ahead).start()
        chunk_copy(c).wait()
        dst_ref[pl.ds(c * WEIGHT_CHUNK_ROWS, WEIGHT_CHUNK_ROWS), :] = (
            stage_ref[c % STAGE_SLOTS, :, pl.ds(0, cols)].astype(jnp.bfloat16))


def _ffn_kernel(x_ref, pre_g_ref, wg_hbm, wu_hbm, wd_hbm, post_g_ref, o_ref,
                wg_ref, wu_ref, wd_ref, stage_ref, sem):
    @pl.when(pl.program_id(0) == 0)
    def _():
        for src, dst in ((wg_hbm, wg_ref), (wu_hbm, wu_ref), (wd_hbm, wd_ref)):
            _stage_cast(src, dst, stage_ref, sem)

    subs = [slice(i * FFN_SUB_TILE, (i + 1) * FFN_SUB_TILE)
            for i in range(x_ref.shape[0] // FFN_SUB_TILE)]
    xs = [x_ref[rows, :] for rows in subs]
    hs = [_rmsnorm(x, pre_g_ref[...]).astype(jnp.bfloat16) for x in xs]
    gates = [_dot(h, wg_ref[...]) for h in hs]
    ups = [_dot(h, wu_ref[...]) for h in hs]
    acts = [(g * jax.nn.sigmoid(g) * u).astype(jnp.bfloat16) for g, u in zip(gates, ups)]
    fs = [_dot(a, wd_ref[...]) for a in acts]
    for rows, x, f in zip(subs, xs, fs):
        o_ref[rows, :] = x + MACARON_WEIGHT * _rmsnorm(f, post_g_ref[...])


def _ffn(x, pre_g, wg, wu, wd, post_g):
    n, d = x.shape
    d_ff = wg.shape[1]
    assert d % WEIGHT_CHUNK_ROWS == 0 and d_ff % WEIGHT_CHUNK_ROWS == 0
    tile = pl.BlockSpec((TOKEN_TILE, d), lambda i: (i, 0))
    hbm = pl.BlockSpec(memory_space=pl.ANY)
    return pl.pallas_call(
        _ffn_kernel,
        grid=(n // TOKEN_TILE,),
        in_specs=[tile, _const_spec((1, d)), hbm, hbm, hbm, _const_spec((1, d))],
        out_specs=tile,
        out_shape=jax.ShapeDtypeStruct((n, d), jnp.float32),
        scratch_shapes=[
            pltpu.VMEM((d, d_ff), jnp.bfloat16), pltpu.VMEM((d, d_ff), jnp.bfloat16),
            pltpu.VMEM((d_ff, d), jnp.bfloat16),
            pltpu.VMEM((STAGE_SLOTS, WEIGHT_CHUNK_ROWS, max(d, d_ff)), jnp.float32),
            pltpu.SemaphoreType.DMA((STAGE_SLOTS,)),
        ],
        compiler_params=_params("arbitrary"),
        name="ffn",
    )(x, pre_g, wg, wu, wd, post_g)


def _proj_kernel(x_ref, g_ref, wqt_ref, wk_ref, wvt_ref, wp_ref,
                 qt_ref, k_ref, kmean_ref, vt_ref, p_ref):
    blocks =[slice(i * MOBA_BLOCK, (i + 1) * MOBA_BLOCK) for i in range(PROJ_TOKEN_TILE // MOBA_BLOCK)]
    hs = [_rmsnorm(x_ref[0, rows, :], g_ref[...]).astype(jnp.bfloat16) for rows in blocks]
    qts, ks, vts, ps = [], [], [], []
    for h in hs:
        qts.append(lax.dot_general(wqt_ref[...], h, _NT, preferred_element_type=jnp.float32))
        ks.append(_dot(h, wk_ref[...]))
        vts.append(lax.dot_general(wvt_ref[...], h, _NT, preferred_element_type=jnp.float32))
        ps.append(_dot(h, wp_ref[...]))
    for blk, rows in enumerate(blocks):
        qt_ref[0, blk] = (qts[blk] * (ATTN_HEAD_DIM ** -0.5 * LOG2_E)).astype(jnp.bfloat16)
        vt_ref[0, blk] = vts[blk].astype(jnp.bfloat16)
        k_ref[0, blk] = ks[blk].astype(jnp.bfloat16)
        kmean_ref[0, blk] = jnp.sum(ks[blk], axis=0, keepdims=True) * (1.0 / MOBA_BLOCK)
        p_ref[0, rows, :] = ps[blk]


def _proj(x, g, wqt, wk, wvt, wp):
    b, s, d = x.shape
    nb = s // MOBA_BLOCK
    bpt = PROJ_TOKEN_TILE // MOBA_BLOCK
    return pl.pallas_call(
        _proj_kernel,
        grid=(b, s // PROJ_TOKEN_TILE),
        in_specs=[pl.BlockSpec((1, PROJ_TOKEN_TILE, d), lambda bi, t: (bi, t, 0)),
                  _const_spec((1, d)), _const_spec((D_ATTN, d)),
                  _const_spec((d, D_ATTN)), _const_spec((D_ATTN, d)),
                  _const_spec((d, D_POOL))],
        out_specs=[
            pl.BlockSpec((1, bpt, D_ATTN, MOBA_BLOCK), lambda bi, t: (bi, t, 0, 0)),
            pl.BlockSpec((1, bpt, MOBA_BLOCK, D_ATTN), lambda bi, t: (bi, t, 0, 0)),
            pl.BlockSpec((1, bpt, 1, D_ATTN), lambda bi, t: (bi, t, 0, 0)),
            pl.BlockSpec((1, bpt, D_ATTN, MOBA_BLOCK), lambda bi, t: (bi, t, 0, 0)),
            pl.BlockSpec((1, PROJ_TOKEN_TILE, D_POOL), lambda bi, t: (bi, t, 0)),
        ],
        out_shape=[
            jax.ShapeDtypeStruct((b, nb, D_ATTN, MOBA_BLOCK), jnp.bfloat16),
            jax.ShapeDtypeStruct((b, nb, MOBA_BLOCK, D_ATTN), jnp.bfloat16),
            jax.ShapeDtypeStruct((b, nb, 1, D_ATTN), jnp.float32),
            jax.ShapeDtypeStruct((b, nb, D_ATTN, MOBA_BLOCK), jnp.bfloat16),
            jax.ShapeDtypeStruct((b, s, D_POOL), jnp.float32),
        ],
        compiler_params=_params("parallel", "parallel"),
        name="mixer_proj",
    )(x, g, wqt, wk, wvt, wp)


MOBA_TILE = 2 * MOBA_BLOCK
SELECT_UNROLL = 4
ONES_ROWS = 16


def _moba_items(n_tiles):
    tiles, pairs = [], []
    for c in range(n_tiles):
        for pr in range(c, -1, -1):
            tiles.append(c)
            pairs.append(pr)
    return np.asarray(tiles, np.int32), np.asarray(pairs, np.int32)


def _moba_kernel(slopes_ref, tile_tab_ref, pair_tab_ref, qt_ref, k_ref, kmean_ref, vt_ref,
                 o_ref, sel_ref, bias_ref, s_ref, p_ref, acc_ref):
    hp = pl.program_id(1)
    L, T = MOBA_BLOCK, MOBA_TILE
    nb = vt_ref.shape[1]
    n_tiles = nb // 2
    n_items = tile_tab_ref.shape[0]
    heads = range(HEADS_PER_STEP)
    slopes = [slopes_ref[hp * HEADS_PER_STEP + hh] for hh in heads]
    head_rows = [slice(hh * ATTN_HEAD_DIM, (hh + 1) * ATTN_HEAD_DIM) for hh in heads]
    head_row = lax.broadcasted_iota(jnp.int32, (HEADS_PER_STEP * ATTN_HEAD_DIM, T), 0) // ATTN_HEAD_DIM

    def query_tile(c):
        qt = jnp.concatenate([qt_ref[0, 2 * c], qt_ref[0, 2 * c + 1]], axis=1)
        return [jnp.where(head_row == hh, qt, jnp.zeros_like(qt)) for hh in heads]

    key = lax.broadcasted_iota(jnp.int32, (T, T), 0)
    query = lax.broadcasted_iota(jnp.int32, (T, T), 1)
    q_minus_k = (query - key).astype(jnp.float32)
    non_causal = jnp.where(key > query, jnp.inf, 0.0)
    for hh in heads:
        base = slopes[hh] * q_minus_k
        bias_ref[hh, 0] = base
        bias_ref[hh, 1] = base + non_causal

    kmean = kmean_ref[0].astype(jnp.bfloat16)
    blk_id = lax.broadcasted_iota(jnp.int32, (nb, T), 0)
    second_half = lax.broadcasted_iota(jnp.int32, (nb, T), 1) // L

    def select_tile(c):
        qt_heads = query_tile(c)
        own = 2 * c + second_half
        for hh in heads:
            gate = jnp.where(blk_id < own, _dot(kmean, qt_heads[hh]), -jnp.inf)
            sel = blk_id == own
            for _ in range(MOBA_TOPK):
                best = jnp.max(gate, axis=0, keepdims=True)
                first = jnp.min(jnp.where(gate == best, blk_id, nb), axis=0, keepdims=True)
                taken = blk_id == first
                sel = sel | (taken & (blk_id < own))
                gate = jnp.where(taken, -jnp.inf, gate)
            sel_ref[hh, c] = sel.astype(jnp.float32)

    def select(it, _):
        for u in range(SELECT_UNROLL):
            select_tile(SELECT_UNROLL * it + u)
        return 0

    lax.fori_loop(0, n_tiles // SELECT_UNROLL, select, 0)

    blk_rows = [slice(blk * L, (blk + 1) * L) for blk in range(2)]
    ones_rows = jnp.ones((ONES_ROWS, L), jnp.bfloat16)

    def biased_scores(e, slot, hh):
        c, pr = tile_tab_ref[e], pair_tab_ref[e]
        variant = jnp.where(pr == c, 1, 0)
        k_pair = k_ref[0, pl.ds(pl.multiple_of(pr * T, T), T), :]
        t = _dot(k_pair, query_tile(c)[hh]) - bias_ref[hh, variant]
        s_ref[slot, hh] = t
        return tuple(jnp.max(t[rows, :], axis=0, keepdims=True) for rows in blk_rows)

    def value_product(pr, slot, hh):
        lhs = [jnp.concatenate([vt_ref[0, 2 * pr + blk, head_rows[hh], :], ones_rows], axis=0)
               for blk in range(2)]
        return (_dot(lhs[0], p_ref[slot, hh, blk_rows[0], :])
                + _dot(lhs[1], p_ref[slot, hh, blk_rows[1], :]))

    def denominator(hh):
        d = acc_ref[hh, ATTN_HEAD_DIM:ATTN_HEAD_DIM + 1, :]
        return jnp.where(d == 0.0, 1.0, d)

    def write_tile(c):
        o_t = jnp.concatenate(
            [acc_ref[hh, :ATTN_HEAD_DIM, :] / denominator(hh) for hh in heads], axis=0)
        o_ref[0, pl.ds(pl.multiple_of(c * T, T), T), :] = o_t.T.astype(o_ref.dtype)

    def item(e, slot, state):
        m, alpha_lag1, alpha_lag2, col_max = state
        other = 1 - slot
        c, pr = tile_tab_ref[e], pair_tab_ref[e]
        e_lag2 = jnp.maximum(e - 2, 0)

        pv_lag2, col_max_next = [], []
        for hh in heads:
            pv_lag2.append(value_product(pair_tab_ref[e_lag2], slot, hh))
            col_max_next.append(biased_scores(jnp.minimum(e + 1, n_items - 1), other, hh))
        col_max_next = tuple(col_max_next)

        opens = pr == c
        m_out, alpha_out = [], []
        for hh in heads:
            m_old = jnp.where(opens, -jnp.inf, m[hh])
            offset = slopes[hh] * ((c - pr) * T).astype(jnp.float32)
            keep = [sel_ref[hh, c, pl.ds(2 * pr + blk, 1), :] > 0.5 for blk in range(2)]
            m_blk = [jnp.where(keep[blk], col_max[hh][blk], -jnp.inf) for blk in range(2)]
            m_new = jnp.maximum(m_old, jnp.maximum(m_blk[0], m_blk[1]) - offset)
            for blk in range(2):
                shift = jnp.where(keep[blk], m_new + offset, jnp.inf)
                p = jnp.exp2(s_ref[slot, hh, blk_rows[blk], :] - shift)
                p_ref[slot, hh, blk_rows[blk], :] = p.astype(jnp.bfloat16)
            m_out.append(m_new)
            alpha_out.append(jnp.exp2(m_old - m_new))

        fold(pv_lag2, alpha_lag2)
        write_tile(tile_tab_ref[e_lag2])

        return tuple(m_out), tuple(alpha_out), alpha_lag1, col_max_next

    def fold(pv, alpha):
        for hh in heads:
            acc_ref[hh] = alpha[hh] * acc_ref[hh] + pv[hh]

    col_max = tuple(biased_scores(0, 0, hh) for hh in heads)
    for hh in heads:
        acc_ref[hh] = jnp.zeros(acc_ref.shape[1:], jnp.float32)
    p_ref[...] = jnp.zeros(p_ref.shape, p_ref.dtype)
    row = lambda v: tuple(jnp.full((1, T), v, jnp.float32) for _ in heads)
    state = (row(-jnp.inf), row(0.0), row(0.0), col_max)

    def two_items(it, state):
        state = item(2 * it, 0, state)
        return item(2 * it + 1, 1, state)

    _, alpha_lag1, alpha_lag2, _ = lax.fori_loop(0, n_items // 2, two_items, state)

    for e, alpha in ((n_items - 2, alpha_lag2), (n_items - 1, alpha_lag1)):
        fold([value_product(pair_tab_ref[e], e % 2, hh) for hh in heads], alpha)
        write_tile(tile_tab_ref[e])


def _moba(slopes, qt, k, kmean, vt):
    b, nb, _, L = qt.shape
    lanes = HEADS_PER_STEP * ATTN_HEAD_DIM
    T = MOBA_TILE
    tile_tab, pair_tab = _moba_items(nb // 2)
    assert len(tile_tab) % 2 == 0
    smem = pl.BlockSpec(memory_space=pltpu.SMEM)
    return pl.pallas_call(
        _moba_kernel,
        grid=(b, N_ATTN_HEADS // HEADS_PER_STEP),
        in_specs=[
            smem, smem, smem,
            pl.BlockSpec((1, nb, lanes, L), lambda bi, hp: (bi, 0, hp, 0)),
            pl.BlockSpec((1, nb * L, lanes), lambda bi, hp: (bi, 0, hp)),
            pl.BlockSpec((1, nb, lanes), lambda bi, hp: (bi, 0, hp)),
            pl.BlockSpec((1, nb, lanes, L), lambda bi, hp: (bi, 0, hp, 0)),
        ],
        out_specs=pl.BlockSpec((1, nb * L, lanes), lambda bi, hp: (bi, 0, hp)),
        out_shape=jax.ShapeDtypeStruct((b, nb * L, D_ATTN), jnp.bfloat16),
        scratch_shapes=[
            pltpu.VMEM((HEADS_PER_STEP, nb // 2, nb, T), jnp.float32),
            pltpu.VMEM((HEADS_PER_STEP, 2, T, T), jnp.float32),
            pltpu.VMEM((2, HEADS_PER_STEP, T, T), jnp.float32),
            pltpu.VMEM((2, HEADS_PER_STEP, T, T), jnp.bfloat16),
            pltpu.VMEM((HEADS_PER_STEP, ATTN_HEAD_DIM + ONES_ROWS, T), jnp.float32),
        ],
        compiler_params=_params("parallel", "parallel"),
        name="moba_attention",
    )(slopes, jnp.asarray(tile_tab), jnp.asarray(pair_tab), qt, k, kmean, vt)


def _memkv_kernel(mem_ref, g_ref, wkv_ref, k_ref, v_ref):
    d = mem_ref.shape[2]
    mem_n = _rmsnorm(mem_ref[0], g_ref[...]).astype(jnp.bfloat16)
    kv = _dot(mem_n, wkv_ref[...])
    k_ref[0] = kv[:, :d].astype(jnp.bfloat16)
    v_ref[0] = kv[:, d:].astype(jnp.bfloat16)


def _memkv(mem, g, wkv):
    b, m, d = mem.shape
    blk = pl.BlockSpec((1, m, d), lambda bi: (bi, 0, 0))
    return pl.pallas_call(
        _memkv_kernel,
        grid=(b,),
        in_specs=[blk, _const_spec((1, d)), _const_spec(wkv.shape)],
        out_specs=[blk, blk],
        out_shape=[jax.ShapeDtypeStruct((b, m, d), jnp.bfloat16)] * 2,
        compiler_params=_params("parallel"),
        name="mem_kv",
    )(mem, g, wkv)


def _pooling_mixer(ext, p, first_pos, pool_w_ref, pool_scale_ref):
    rows = p.shape[0]
    pos = first_pos + lax.broadcasted_iota(jnp.int32, (rows, 1), 0)
    pooled = []
    for g, w in enumerate(POOL_WINDOWS):
        cols = slice(g * POOL_GROUP_DIM, (g + 1) * POOL_GROUP_DIM)
        acc = ext[:, cols]
        n_valid = acc.shape[0]
        span = 1
        while span < w:
            acc = acc[span:, :] + acc[:n_valid - span, :]
            n_valid -= span
            span *= 2
        win_sum = acc[n_valid - rows:, :]
        cnt = jnp.minimum(pos + 1, w).astype(jnp.float32)
        d = win_sum / cnt - p[:, cols]
        y = _dot(d.astype(jnp.bfloat16), pool_w_ref[g])
        pooled.append(y * pool_scale_ref[:, cols])
    return jnp.concatenate(pooled, axis=-1).astype(jnp.bfloat16)


def _mix_xattn_kernel(x_ref, attn_ref, p_ref, halo_ref, pool_w_ref, pool_scale_ref,
                      w_out_ref, mix_g_ref, pre_g_ref, wq_ref, k_ref, v_ref, wo_ref,
                      post_g_ref, o_ref):
    t = pl.program_id(1)
    tm, d = x_ref.shape[1], x_ref.shape[2]
    dh = d // XA_HEADS
    subs = [slice(i * MIX_SUB_TILE, (i + 1) * MIX_SUB_TILE) for i in range(tm // MIX_SUB_TILE)]
    halo = jnp.where(t > 0, halo_ref[0], 0.0)
    ext = jnp.concatenate([halo, p_ref[0]], axis=0)
    pools = [_pooling_mixer(ext[rows.start:rows.stop + POOL_HALO, :], p_ref[0, rows, :],
                            t * tm + rows.start, pool_w_ref, pool_scale_ref) for rows in subs]
    ys = [_dot(jnp.concatenate([attn_ref[0, rows, :], pool], axis=-1), w_out_ref[...])
          for rows, pool in zip(subs, pools)]
    xs = [x_ref[0, rows, :] + _rmsnorm(y, mix_g_ref[...]) for rows, y in zip(subs, ys)]

    hs = [_rmsnorm(x, pre_g_ref[...]).astype(jnp.bfloat16) for x in xs]
    qs = [_dot(h, wq_ref[...]).astype(jnp.bfloat16) for h in hs]
    heads, cs = [[] for _ in subs], []
    head_cols = [slice(hd * dh, (hd + 1) * dh) for hd in range(XA_HEADS)]
    ss = [[lax.dot_general(q[:, cols], k_ref[0, :, cols], _NT,
                           preferred_element_type=jnp.float32) * (dh ** -0.5) for q in qs]
          for cols in head_cols]
    for i in range(len(subs)):
        for hd, cols in enumerate(head_cols):
            s = ss[hd][i]
            e = jnp.exp(s - jnp.max(s, axis=-1, keepdims=True))
            p = e / jnp.sum(e, axis=-1, keepdims=True)
            heads[i].append(_dot(p.astype(jnp.bfloat16), v_ref[0, :, cols]))
        cs.append(_dot(jnp.concatenate(heads[i], axis=-1).astype(jnp.bfloat16), wo_ref[...]))
    for rows, x, c in zip(subs, xs, cs):
        o_ref[0, rows, :] = x + _rmsnorm(c, post_g_ref[...])


def _mix_xattn(x, attn, p_in, pool_w, pool_scale, w_out, mix_g, pre_g, wq, k_mem, v_mem, wo,
               post_g):
    b, s, d = x.shape
    m = k_mem.shape[1]
    halo_blocks = MIX_TOKEN_TILE // POOL_HALO
    tile = lambda width: pl.BlockSpec((1, MIX_TOKEN_TILE, width), lambda bi, t: (bi, t, 0))
    mem_blk = pl.BlockSpec((1, m, d), lambda bi, t: (bi, 0, 0))
    return pl.pallas_call(
        _mix_xattn_kernel,
        grid=(b, s // MIX_TOKEN_TILE),
        in_specs=[
            tile(d), tile(D_ATTN), tile(D_POOL),
            pl.BlockSpec((1, POOL_HALO, D_POOL),
                         lambda bi, t: (bi, jnp.maximum(t * halo_blocks - 1, 0), 0)),
            _const_spec(pool_w.shape), _const_spec((1, D_POOL)),
            _const_spec(w_out.shape), _const_spec((1, d)),
            _const_spec((1, d)), _const_spec(wq.shape), mem_blk, mem_blk,
            _const_spec(wo.shape), _const_spec((1, d)),
        ],
        out_specs=tile(d),
        out_shape=jax.ShapeDtypeStruct((b, s, d), jnp.float32),
        compiler_params=_params("parallel", "parallel"),
        name="mixer_out_mem_xattn",
    )(x, attn, p_in, p_in, pool_w, pool_scale, w_out, mix_g, pre_g, wq, k_mem, v_mem, wo, post_g)


def _alibi_slopes(n_heads):
    return jnp.asarray(2.0 ** (-8.0 * np.arange(1, n_heads + 1) / n_heads), jnp.float32)


def kernel(x, mem, ffn1_pre_g, ffn1_w_gate, ffn1_w_up, ffn1_w_down, ffn1_post_g, mix_pre_g, w_in, pool_w, pool_scale, w_out, mix_post_g, xa_pre_g, mem_g, xa_wq, xa_wkv, xa_wo, xa_post_g, ffn2_pre_g, ffn2_w_gate, ffn2_w_up, ffn2_w_down, ffn2_post_g):
    b, s, d = x.shape
    depth = ffn1_pre_g.shape[0]
    bf = lambda w: w.astype(jnp.bfloat16)
    row = lambda g: g.reshape(1, -1)
    slopes = _alibi_slopes(N_ATTN_HEADS) * LOG2_E
    for l in range(depth):
        x = _ffn(x.reshape(b * s, d), row(ffn1_pre_g[l]), ffn1_w_gate[l], ffn1_w_up[l],
                 ffn1_w_down[l], row(ffn1_post_g[l])).reshape(b, s, d)

        w = bf(w_in[l])
        qt, k, kmean, vt, p_in = _proj(
            x, row(mix_pre_g[l]), w[:, :D_ATTN].T, w[:, D_ATTN:2 * D_ATTN],
            w[:, 2 * D_ATTN:3 * D_ATTN].T, w[:, 3 * D_ATTN:])
        attn = _moba(slopes, qt, k.reshape(b, s, D_ATTN), kmean.reshape(b, -1, D_ATTN), vt)
        k_mem, v_mem = _memkv(mem, row(mem_g[l]), bf(xa_wkv[l]))
        x = _mix_xattn(x, attn, p_in, bf(pool_w[l]), row(pool_scale[l]), bf(w_out[l]),
                       row(mix_post_g[l]), row(xa_pre_g[l]), bf(xa_wq[l]), k_mem, v_mem,
                       bf(xa_wo[l]), row(xa_post_g[l]))

        x = _ffn(x.reshape(b * s, d), row(ffn2_pre_g[l]), ffn2_w_gate[l], ffn2_w_up[l],
                 ffn2_w_down[l], row(ffn2_post_g[l])).reshape(b, s, d)
    return x
```

```python
import functools

import numpy as np
import jax
import jax.numpy as jnp
from jax import lax
from jax.experimental import pallas as pl
from jax.experimental.pallas import tpu as pltpu

N_ATTN_HEADS = 8
ATTN_HEAD_DIM = 64
D_ATTN = N_ATTN_HEADS * ATTN_HEAD_DIM
POOL_WINDOWS = (2, 4, 8, 16)
POOL_GROUP_DIM = 128
D_POOL = len(POOL_WINDOWS) * POOL_GROUP_DIM
MOBA_BLOCK = 256
MOBA_TOPK = 3
XA_HEADS = 4
MACARON_WEIGHT = 0.5
RMS_EPS = 1e-6
LOG2_E = 1.4426950408889634

V7X_LANES = 128
V7X_VMEM_BYTES = 64 * 1024 * 1024
VMEM_LIMIT_BYTES = 56 * 1024 * 1024

TOKEN_TILE = 512
HEADS_PER_STEP = V7X_LANES // ATTN_HEAD_DIM
PROJ_TOKEN_TILE = 1024
MIX_TOKEN_TILE = 1024
MIX_SUB_TILE = 256
WEIGHT_CHUNK_ROWS = 128
STAGE_SLOTS = 4
FFN_SUB_TILE = 256
POOL_HALO = max(POOL_WINDOWS)

_NT = (((1,), (1,)), ((), ()))


def _rmsnorm(x, g):
    r = lax.rsqrt(jnp.mean(x * x, axis=-1, keepdims=True) + RMS_EPS)
    return (x * r) * g


def _dot(a, b):
    return jnp.dot(a, b, preferred_element_type=jnp.float32)


def _params(*semantics):
    return pltpu.CompilerParams(dimension_semantics=semantics,
                                vmem_limit_bytes=VMEM_LIMIT_BYTES)


def _const_spec(shape):
    return pl.BlockSpec(shape, lambda *_: (0,) * len(shape))


def _stage_cast(pairs, stage_ref, sem):
    chunks = [(src, dst, r) for src, dst in pairs
              for r in range(0, src.shape[0], WEIGHT_CHUNK_ROWS)]

    def chunk_copy(c):
        src, _, r = chunks[c]
        slot = c % STAGE_SLOTS
        return pltpu.make_async_copy(
            src.at[pl.ds(r, WEIGHT_CHUNK_ROWS), :],
            stage_ref.at[slot, :, pl.ds(0, src.shape[1])], sem.at[slot])

    ahead = STAGE_SLOTS - 1
    for c in range(min(ahead, len(chunks))):
        chunk_copy(c).start()
    for c, (src, dst, r) in enumerate(chunks):
        if c + ahead < len(chunks):
            chunk_copy(c + ahead).start()
        chunk_copy(c).wait()
        dst[pl.ds(r, WEIGHT_CHUNK_ROWS), :] = (
            stage_ref[c % STAGE_SLOTS, :, pl.ds(0, src.shape[1])].astype(jnp.bfloat16))


def _ffn_kernel(x_ref, pre_g_ref, wg_hbm, wu_hbm, wd_hbm, post_g_ref, o_ref,
                wg_ref, wu_ref, wd_ref, stage_ref, sem):
    @pl.when(pl.program_id(0) == 0)
    def _():
        _stage_cast(((wg_hbm, wg_ref), (wu_hbm, wu_ref), (wd_hbm, wd_ref)), stage_ref, sem)

    subs = [slice(i * FFN_SUB_TILE, (i + 1) * FFN_SUB_TILE)
            for i in range(x_ref.shape[0] // FFN_SUB_TILE)]
    xs = [x_ref[rows, :] for rows in subs]
    hs = [_rmsnorm(x, pre_g_ref[...]).astype(jnp.bfloat16) for x in xs]
    gates = [_dot(h, wg_ref[...]) for h in hs]
    ups = [_dot(h, wu_ref[...]) for h in hs]
    acts = [(g * jax.nn.sigmoid(g) * u).astype(jnp.bfloat16) for g, u in zip(gates, ups)]
    fs = [_dot(a, wd_ref[...]) for a in acts]
    for rows, x, f in zip(subs, xs, fs):
        o_ref[rows, :] = x + MACARON_WEIGHT * _rmsnorm(f, post_g_ref[...])


def _ffn(x, pre_g, wg, wu, wd, post_g):
    n, d = x.shape
    d_ff = wg.shape[1]
    assert d % WEIGHT_CHUNK_ROWS == 0 and d_ff % WEIGHT_CHUNK_ROWS == 0
    tile = pl.BlockSpec((TOKEN_TILE, d), lambda i: (i, 0))
    hbm = pl.BlockSpec(memory_space=pl.ANY)
    return pl.pallas_call(
        _ffn_kernel,
        grid=(n // TOKEN_TILE,),
        in_specs=[tile, _const_spec((1, d)), hbm, hbm, hbm, _const_spec((1, d))],
        out_specs=tile,
        out_shape=jax.ShapeDtypeStruct((n, d), jnp.float32),
        scratch_shapes=[
            pltpu.VMEM((d, d_ff), jnp.bfloat16), pltpu.VMEM((d, d_ff), jnp.bfloat16),
            pltpu.VMEM((d_ff, d), jnp.bfloat16),
            pltpu.VMEM((STAGE_SLOTS, WEIGHT_CHUNK_ROWS, max(d, d_ff)), jnp.float32),
            pltpu.SemaphoreType.DMA((STAGE_SLOTS,)),
        ],
        compiler_params=_params("arbitrary"),
        name="ffn",
    )(x, pre_g, wg, wu, wd, post_g)


def _proj_kernel(x_ref, g_ref, wqt_ref, wk_ref, wvt_ref, wp_ref,
                 qt_ref, k_ref, kmean_ref, vt_ref, p_ref):
    blocks = [slice(i * MOBA_BLOCK, (i + 1) * MOBA_BLOCK) for i in range(PROJ_TOKEN_TILE // MOBA_BLOCK)]
    hs = [_rmsnorm(x_ref[0, rows, :], g_ref[...]).astype(jnp.bfloat16) for rows in blocks]
    qts, ks, vts, ps = [], [], [], []
    for h in hs:
        qts.append(lax.dot_general(wqt_ref[...], h, _NT, preferred_element_type=jnp.float32))
        ks.append(_dot(h, wk_ref[...]))
        vts.append(lax.dot_general(wvt_ref[...], h, _NT, preferred_element_type=jnp.float32))
        ps.append(_dot(h, wp_ref[...]))
    for blk, rows in enumerate(blocks):
        qt_ref[0, blk] = (qts[blk] * (ATTN_HEAD_DIM ** -0.5 * LOG2_E)).astype(jnp.bfloat16)
        vt_ref[0, blk] = vts[blk].astype(jnp.bfloat16)
        k_ref[0, blk] = ks[blk].astype(jnp.bfloat16)
        kmean_ref[0, blk] = jnp.sum(ks[blk], axis=0, keepdims=True) * (1.0 / MOBA_BLOCK)
        p_ref[0, rows, :] = ps[blk]


def _proj(x, g, wqt, wk, wvt, wp):
    b, s, d = x.shape
    nb = s // MOBA_BLOCK
    bpt = PROJ_TOKEN_TILE // MOBA_BLOCK
    return pl.pallas_call(
        _proj_kernel,
        grid=(b, s // PROJ_TOKEN_TILE),
        in_specs=[pl.BlockSpec((1, PROJ_TOKEN_TILE, d), lambda bi, t: (bi, t, 0)),
                  _const_spec((1, d)), _const_spec((D_ATTN, d)),
                  _const_spec((d, D_ATTN)), _const_spec((D_ATTN, d)),
                  _const_spec((d, D_POOL))],
        out_specs=[
            pl.BlockSpec((1, bpt, D_ATTN, MOBA_BLOCK), lambda bi, t: (bi, t, 0, 0)),
            pl.BlockSpec((1, bpt, MOBA_BLOCK, D_ATTN), lambda bi, t: (bi, t, 0, 0)),
            pl.BlockSpec((1, bpt, 1, D_ATTN), lambda bi, t: (bi, t, 0, 0)),
            pl.BlockSpec((1, bpt, D_ATTN, MOBA_BLOCK), lambda bi, t: (bi, t, 0, 0)),
            pl.BlockSpec((1, PROJ_TOKEN_TILE, D_POOL), lambda bi, t: (bi, t, 0)),
        ],
        out_shape=[
            jax.ShapeDtypeStruct((b, nb, D_ATTN, MOBA_BLOCK), jnp.bfloat16),
            jax.ShapeDtypeStruct((b, nb, MOBA_BLOCK, D_ATTN), jnp.bfloat16),
            jax.ShapeDtypeStruct((b, nb, 1, D_ATTN), jnp.float32),
            jax.ShapeDtypeStruct((b, nb, D_ATTN, MOBA_BLOCK), jnp.bfloat16),
            jax.ShapeDtypeStruct((b, s, D_POOL), jnp.float32),
        ],
        compiler_params=_params("parallel", "parallel"),
        name="mixer_proj",
    )(x, g, wqt, wk, wvt, wp)


MOBA_TILE = 2 * MOBA_BLOCK
SELECT_UNROLL = 4
ONES_ROWS = 16


def _moba_items(n_tiles):
    tiles, pairs = [], []
    for c in range(n_tiles):
        for pr in range(c, -1, -1):
            tiles.append(c)
            pairs.append(pr)
    return np.asarray(tiles, np.int32), np.asarray(pairs, np.int32)


def _moba_kernel(slopes_ref, tile_tab_ref, pair_tab_ref, qt_ref, k_ref, kmean_ref, vt_ref,
                 o_ref, sel_ref, bias_ref, s_ref, p_ref, acc_ref):
    hp = pl.program_id(1)
    L, T = MOBA_BLOCK, MOBA_TILE
    nb = vt_ref.shape[1]
    n_tiles = nb // 2
    n_items = tile_tab_ref.shape[0]
    heads = range(HEADS_PER_STEP)
    slopes = [slopes_ref[hp * HEADS_PER_STEP + hh] for hh in heads]
    head_rows = [slice(hh * ATTN_HEAD_DIM, (hh + 1) * ATTN_HEAD_DIM) for hh in heads]
    head_row = lax.broadcasted_iota(jnp.int32, (HEADS_PER_STEP * ATTN_HEAD_DIM, T), 0) // ATTN_HEAD_DIM

    def query_tile(c):
        qt = jnp.concatenate([qt_ref[0, 2 * c], qt_ref[0, 2 * c + 1]], axis=1)
        return [jnp.where(head_row == hh, qt, jnp.zeros_like(qt)) for hh in heads]

    key = lax.broadcasted_iota(jnp.int32, (T, T), 0)
    query = lax.broadcasted_iota(jnp.int32, (T, T), 1)
    q_minus_k = (query - key).astype(jnp.float32)
    non_causal = jnp.where(key > query, jnp.inf, 0.0)
    for hh in heads:
        base = slopes[hh] * q_minus_k
        bias_ref[hh, 0] = base
        bias_ref[hh, 1] = base + non_causal

    kmean = kmean_ref[0].astype(jnp.bfloat16)
    blk_id = lax.broadcasted_iota(jnp.int32, (nb, T), 0)
    second_half = lax.broadcasted_iota(jnp.int32, (nb, T), 1) // L

    def select_tile(c):
        qt_heads = query_tile(c)
        own = 2 * c + second_half
        for hh in heads:
            gate = jnp.where(blk_id < own, _dot(kmean, qt_heads[hh]), -jnp.inf)
            sel = blk_id == own
            for _ in range(MOBA_TOPK):
                best = jnp.max(gate, axis=0, keepdims=True)
                first = jnp.min(jnp.where(gate == best, blk_id, nb), axis=0, keepdims=True)
                taken = blk_id == first
                sel = sel | (taken & (blk_id < own))
                gate = jnp.where(taken, -jnp.inf, gate)
            sel_ref[hh, c] = sel.astype(jnp.float32)

    def select(it, _):
        for u in range(SELECT_UNROLL):
            select_tile(SELECT_UNROLL * it + u)
        return 0

    lax.fori_loop(0, n_tiles // SELECT_UNROLL, select, 0)

    blk_rows = [slice(blk * L, (blk + 1) * L) for blk in range(2)]
    ones_rows = jnp.ones((ONES_ROWS, L), jnp.bfloat16)

    def biased_scores(e, slot):
        c, pr = tile_tab_ref[e], pair_tab_ref[e]
        variant = jnp.where(pr == c, 1, 0)
        k_pair = k_ref[0, pl.ds(pl.multiple_of(pr * T, T), T), :]
        col_max = []
        for hh, qt_h in enumerate(query_tile(c)):
            t = _dot(k_pair, qt_h) - bias_ref[hh, variant]
            s_ref[slot, hh] = t
            col_max.append(tuple(jnp.max(t[rows, :], axis=0, keepdims=True) for rows in blk_rows))
        return tuple(col_max)

    def value_product(pr, slot):
        out = []
        for hh in heads:
            lhs = [jnp.concatenate([vt_ref[0, 2 * pr + blk, head_rows[hh], :], ones_rows], axis=0)
                   for blk in range(2)]
            out.append(_dot(lhs[0], p_ref[slot, hh, blk_rows[0], :])
                       + _dot(lhs[1], p_ref[slot, hh, blk_rows[1], :]))
        return out

    def denominator(hh):
        d = acc_ref[hh, ATTN_HEAD_DIM:ATTN_HEAD_DIM + 1, :]
        return jnp.where(d == 0.0, 1.0, d)

    def write_tile(c):
        o_t = jnp.concatenate(
            [acc_ref[hh, :ATTN_HEAD_DIM, :] / denominator(hh) for hh in heads], axis=0)
        o_ref[0, pl.ds(pl.multiple_of(c * T, T), T), :] = o_t.T.astype(o_ref.dtype)

    def item(e, slot, state):
        m, alpha_lag1, alpha_lag2, col_max = state
        other = 1 - slot
        c, pr = tile_tab_ref[e], pair_tab_ref[e]
        e_lag2 = jnp.maximum(e - 2, 0)

        pv_lag2 = value_product(pair_tab_ref[e_lag2], slot)
        col_max_next = biased_scores(jnp.minimum(e + 1, n_items - 1), other)

        opens = pr == c
        m_out, alpha_out = [], []
        for hh in heads:
            m_old = jnp.where(opens, -jnp.inf, m[hh])
            offset = slopes[hh] * ((c - pr) * T).astype(jnp.float32)
            keep = [sel_ref[hh, c, pl.ds(2 * pr + blk, 1), :] > 0.5 for blk in range(2)]
            m_blk = [jnp.where(keep[blk], col_max[hh][blk], -jnp.inf) for blk in range(2)]
            m_new = jnp.maximum(m_old, jnp.maximum(m_blk[0], m_blk[1]) - offset)
            for blk in range(2):
                shift = jnp.where(keep[blk], m_new + offset, jnp.inf)
                p = jnp.exp2(s_ref[slot, hh, blk_rows[blk], :] - shift)
                p_ref[slot, hh, blk_rows[blk], :] = p.astype(jnp.bfloat16)
            m_out.append(m_new)
            alpha_out.append(jnp.exp2(m_old - m_new))

        fold(pv_lag2, alpha_lag2)
        write_tile(tile_tab_ref[e_lag2])

        return tuple(m_out), tuple(alpha_out), alpha_lag1, col_max_next

    def fold(pv, alpha):
        for hh in heads:
            acc_ref[hh] = alpha[hh] * acc_ref[hh] + pv[hh]

    col_max = biased_scores(0, 0)
    for hh in heads:
        acc_ref[hh] = jnp.zeros(acc_ref.shape[1:], jnp.float32)
    p_ref[...] = jnp.zeros(p_ref.shape, p_ref.dtype)
    row = lambda v: tuple(jnp.full((1, T), v, jnp.float32) for _ in heads)
    state = (row(-jnp.inf), row(0.0), row(0.0), col_max)

    def two_items(it, state):
        state = item(2 * it, 0, state)
        return item(2 * it + 1, 1, state)

    _, alpha_lag1, alpha_lag2, _ = lax.fori_loop(0, n_items // 2, two_items, state)

    for e, alpha in ((n_items - 2, alpha_lag2), (n_items - 1, alpha_lag1)):
        fold(value_product(pair_tab_ref[e], e % 2), alpha)
        write_tile(tile_tab_ref[e])


def _moba(slopes, qt, k, kmean, vt):
    b, nb, _, L = qt.shape
    lanes = HEADS_PER_STEP * ATTN_HEAD_DIM
    T = MOBA_TILE
    tile_tab, pair_tab = _moba_items(nb // 2)
    assert len(tile_tab) % 2 == 0
    smem = pl.BlockSpec(memory_space=pltpu.SMEM)
    return pl.pallas_call(
        _moba_kernel,
        grid=(b, N_ATTN_HEADS // HEADS_PER_STEP),
        in_specs=[
            smem, smem, smem,
            pl.BlockSpec((1, nb, lanes, L), lambda bi, hp: (bi, 0, hp, 0)),
            pl.BlockSpec((1, nb * L, lanes), lambda bi, hp: (bi, 0, hp)),
            pl.BlockSpec((1, nb, lanes), lambda bi, hp: (bi, 0, hp)),
            pl.BlockSpec((1, nb, lanes, L), lambda bi, hp: (bi, 0, hp, 0)),
        ],
        out_specs=pl.BlockSpec((1, nb * L, lanes), lambda bi, hp: (bi, 0, hp)),
        out_shape=jax.ShapeDtypeStruct((b, nb * L, D_ATTN), jnp.bfloat16),
        scratch_shapes=[
            pltpu.VMEM((HEADS_PER_STEP, nb // 2, nb, T), jnp.float32),
            pltpu.VMEM((HEADS_PER_STEP, 2, T, T), jnp.float32),
            pltpu.VMEM((2, HEADS_PER_STEP, T, T), jnp.float32),
            pltpu.VMEM((2, HEADS_PER_STEP, T, T), jnp.bfloat16),
            pltpu.VMEM((HEADS_PER_STEP, ATTN_HEAD_DIM + ONES_ROWS, T), jnp.float32),
        ],
        compiler_params=_params("parallel", "parallel"),
        name="moba_attention",
    )(slopes, jnp.asarray(tile_tab), jnp.asarray(pair_tab), qt, k, kmean, vt)


def _memkv_kernel(mem_ref, g_ref, wkv_ref, k_ref, v_ref):
    d = mem_ref.shape[2]
    mem_n = _rmsnorm(mem_ref[0], g_ref[...]).astype(jnp.bfloat16)
    kv = _dot(mem_n, wkv_ref[...])
    k_ref[0] = kv[:, :d].astype(jnp.bfloat16)
    v_ref[0] = kv[:, d:].astype(jnp.bfloat16)


def _memkv(mem, g, wkv):
    b, m, d = mem.shape
    blk = pl.BlockSpec((1, m, d), lambda bi: (bi, 0, 0))
    return pl.pallas_call(
        _memkv_kernel,
        grid=(b,),
        in_specs=[blk, _const_spec((1, d)), _const_spec(wkv.shape)],
        out_specs=[blk, blk],
        out_shape=[jax.ShapeDtypeStruct((b, m, d), jnp.bfloat16)] * 2,
        compiler_params=_params("parallel"),
        name="mem_kv",
    )(mem, g, wkv)


def _pooling_mixer(ext, p, first_pos, pool_w_ref, pool_scale_ref):
    rows = p.shape[0]
    pos = first_pos + lax.broadcasted_iota(jnp.int32, (rows, 1), 0)
    pooled = []
    for g, w in enumerate(POOL_WINDOWS):
        cols = slice(g * POOL_GROUP_DIM, (g + 1) * POOL_GROUP_DIM)
        acc = ext[:, cols]
        n_valid = acc.shape[0]
        span = 1
        while span < w:
            acc = acc[span:, :] + acc[:n_valid - span, :]
            n_valid -= span
            span *= 2
        win_sum = acc[n_valid - rows:, :]
        cnt = jnp.minimum(pos + 1, w).astype(jnp.float32)
        d = win_sum / cnt - p[:, cols]
        y = _dot(d.astype(jnp.bfloat16), pool_w_ref[g])
        pooled.append(y * pool_scale_ref[:, cols])
    return jnp.concatenate(pooled, axis=-1).astype(jnp.bfloat16)


def _mix_xattn_kernel(x_ref, attn_ref, p_ref, halo_ref, pool_w_ref, pool_scale_ref,
                      w_out_ref, mix_g_ref, pre_g_ref, wq_ref, k_ref, v_ref, wo_ref,
                      post_g_ref, o_ref):
    t = pl.program_id(1)
    tm, d = x_ref.shape[1], x_ref.shape[2]
    dh = d // XA_HEADS
    subs = [slice(i * MIX_SUB_TILE, (i + 1) * MIX_SUB_TILE) for i in range(tm // MIX_SUB_TILE)]
    halo = jnp.where(t > 0, halo_ref[0], 0.0)
    ext = jnp.concatenate([halo, p_ref[0]], axis=0)
    pools = [_pooling_mixer(ext[rows.start:rows.stop + POOL_HALO, :], p_ref[0, rows, :],
                            t * tm + rows.start, pool_w_ref, pool_scale_ref) for rows in subs]
    ys = [_dot(jnp.concatenate([attn_ref[0, rows, :], pool], axis=-1), w_out_ref[...])
          for rows, pool in zip(subs, pools)]
    xs = [x_ref[0, rows, :] + _rmsnorm(y, mix_g_ref[...]) for rows, y in zip(subs, ys)]

    hs = [_rmsnorm(x, pre_g_ref[...]).astype(jnp.bfloat16) for x in xs]
    qs = [_dot(h, wq_ref[...]).astype(jnp.bfloat16) for h in hs]
    heads, cs = [[] for _ in subs], []
    head_cols = [slice(hd * dh, (hd + 1) * dh) for hd in range(XA_HEADS)]
    ss = [[lax.dot_general(q[:, cols], k_ref[0, :, cols], _NT,
                           preferred_element_type=jnp.float32) * (dh ** -0.5) for q in qs]
          for cols in head_cols]
    for i in range(len(subs)):
        for hd, cols in enumerate(head_cols):
            s = ss[hd][i]
            e = jnp.exp(s - jnp.max(s, axis=-1, keepdims=True))
            p = e / jnp.sum(e, axis=-1, keepdims=True)
            heads[i].append(_dot(p.astype(jnp.bfloat16), v_ref[0, :, cols]))
        cs.append(_dot(jnp.concatenate(heads[i], axis=-1).astype(jnp.bfloat16), wo_ref[...]))
    for rows, x, c in zip(subs, xs, cs):
        o_ref[0, rows, :] = x + _rmsnorm(c, post_g_ref[...])


def _mix_xattn(x, attn, p_in, pool_w, pool_scale, w_out, mix_g, pre_g, wq, k_mem, v_mem, wo,
               post_g):
    b, s, d = x.shape
    m = k_mem.shape[1]
    halo_blocks = MIX_TOKEN_TILE // POOL_HALO
    tile = lambda width: pl.BlockSpec((1, MIX_TOKEN_TILE, width), lambda bi, t: (bi, t, 0))
    mem_blk = pl.BlockSpec((1, m, d), lambda bi, t: (bi, 0, 0))
    return pl.pallas_call(
        _mix_xattn_kernel,
        grid=(b, s // MIX_TOKEN_TILE),
        in_specs=[
            tile(d), tile(D_ATTN), tile(D_POOL),
            pl.BlockSpec((1, POOL_HALO, D_POOL),
                         lambda bi, t: (bi, jnp.maximum(t * halo_blocks - 1, 0), 0)),
            _const_spec(pool_w.shape), _const_spec((1, D_POOL)),
            _const_spec(w_out.shape), _const_spec((1, d)),
            _const_spec((1, d)), _const_spec(wq.shape), mem_blk, mem_blk,
            _const_spec(wo.shape), _const_spec((1, d)),
        ],
        out_specs=tile(d),
        out_shape=jax.ShapeDtypeStruct((b, s, d), jnp.float32),
        compiler_params=_params("parallel", "parallel"),
        name="mixer_out_mem_xattn",
    )(x, attn, p_in, p_in, pool_w, pool_scale, w_out, mix_g, pre_g, wq, k_mem, v_mem, wo, post_g)


def _alibi_slopes(n_heads):
    return jnp.asarray(2.0 ** (-8.0 * np.arange(1, n_heads + 1) / n_heads), jnp.float32)


def kernel(x, mem, ffn1_pre_g, ffn1_w_gate, ffn1_w_up, ffn1_w_down, ffn1_post_g, mix_pre_g, w_in, pool_w, pool_scale, w_out, mix_post_g, xa_pre_g, mem_g, xa_wq, xa_wkv, xa_wo, xa_post_g, ffn2_pre_g, ffn2_w_gate, ffn2_w_up, ffn2_w_down, ffn2_post_g):
    b, s, d = x.shape
    depth = ffn1_pre_g.shape[0]
    bf = lambda w: w.astype(jnp.bfloat16)
    row = lambda g: g.reshape(1, -1)
    slopes = _alibi_slopes(N_ATTN_HEADS) * LOG2_E
    for l in range(depth):
        x = _ffn(x.reshape(b * s, d), row(ffn1_pre_g[l]), ffn1_w_gate[l], ffn1_w_up[l],
                 ffn1_w_down[l], row(ffn1_post_g[l])).reshape(b, s, d)

        w = bf(w_in[l])
        qt, k, kmean, vt, p_in = _proj(
            x, row(mix_pre_g[l]), w[:, :D_ATTN].T, w[:, D_ATTN:2 * D_ATTN],
            w[:, 2 * D_ATTN:3 * D_ATTN].T, w[:, 3 * D_ATTN:])
        attn = _moba(slopes, qt, k.reshape(b, s, D_ATTN), kmean.reshape(b, -1, D_ATTN), vt)
        k_mem, v_mem = _memkv(mem, row(mem_g[l]), bf(xa_wkv[l]))
        x = _mix_xattn(x, attn, p_in, bf(pool_w[l]), row(pool_scale[l]), bf(w_out[l]),
                       row(mix_post_g[l]), row(xa_pre_g[l]), bf(xa_wq[l]), k_mem, v_mem,
                       bf(xa_wo[l]), row(xa_post_g[l]))

        x = _ffn(x.reshape(b * s, d), row(ffn2_pre_g[l]), ffn2_w_gate[l], ffn2_w_up[l],
                 ffn2_w_down[l], row(ffn2_post_g[l])).reshape(b, s, d)
    return x
```

```python
import numpy as np
import jax
import jax.numpy as jnp
from jax import lax
from jax.experimental import pallas as pl
from jax.experimental.pallas import tpu as pltpu

N_ATTN_HEADS = 8
ATTN_HEAD_DIM = 64
D_ATTN = N_ATTN_HEADS * ATTN_HEAD_DIM
POOL_WINDOWS = (2, 4, 8, 16)
POOL_GROUP_DIM = 128
D_POOL = len(POOL_WINDOWS) * POOL_GROUP_DIM
MOBA_BLOCK = 256
MOBA_TOPK = 3
XA_HEADS = 4
MACARON_WEIGHT = 0.5
RMS_EPS = 1e-6
LOG2_E = 1.4426950408889634

V7X_LANES = 128
V7X_VMEM_BYTES = 64 * 1024 * 1024
VMEM_LIMIT_BYTES = V7X_VMEM_BYTES * 7 // 8

TOKEN_TILE = 512
FFN_SUB_TILE = 256
WEIGHT_CHUNK_ROWS = 128
STAGE_SLOTS = 4
PROJ_TOKEN_TILE = 1024
MIX_TOKEN_TILE = 1024
MIX_SUB_TILE = 256
HEADS_PER_STEP = V7X_LANES // ATTN_HEAD_DIM
POOL_HALO = max(POOL_WINDOWS)

_NT = (((1,), (1,)), ((), ()))


def _rmsnorm(x, g):
    r = lax.rsqrt(jnp.mean(x * x, axis=-1, keepdims=True) + RMS_EPS)
    return (x * r) * g


def _silu(x):
    h = 0.5 * x
    return h + h * jnp.tanh(h)


def _dot(a, b):
    return jnp.dot(a, b, preferred_element_type=jnp.float32)


def _params(*semantics):
    return pltpu.CompilerParams(dimension_semantics=semantics,
                                vmem_limit_bytes=VMEM_LIMIT_BYTES)


def _const_spec(shape):
    return pl.BlockSpec(shape, lambda *_: (0,) * len(shape))


def _stage_cast(pairs, stage_ref, sem):
    chunks = [(src, dst, r) for src, dst in pairs
              for r in range(0, src.shape[0], WEIGHT_CHUNK_ROWS)]

    def chunk_copy(c):
        src, _, r = chunks[c]
        slot = c % STAGE_SLOTS
        return pltpu.make_async_copy(
            src.at[pl.ds(r, WEIGHT_CHUNK_ROWS), :],
            stage_ref.at[slot, :, pl.ds(0, src.shape[1])], sem.at[slot])

    ahead = STAGE_SLOTS - 1
    for c in range(min(ahead, len(chunks))):
        chunk_copy(c).start()
    for c, (src, dst, r) in enumerate(chunks):
        if c + ahead < len(chunks):
            chunk_copy(c + ahead).start()
        chunk_copy(c).wait()
        dst[pl.ds(r, WEIGHT_CHUNK_ROWS), :] = (
            stage_ref[c % STAGE_SLOTS, :, pl.ds(0, src.shape[1])].astype(jnp.bfloat16))


def _ffn_kernel(x_ref, pre_g_ref, wg_hbm, wu_hbm, wd_hbm, post_g_ref, o_ref,
                wg_ref, wu_ref, wd_ref, stage_ref, sem):
    @pl.when(pl.program_id(0) == 0)
    def _():
        _stage_cast(((wg_hbm, wg_ref), (wu_hbm, wu_ref), (wd_hbm, wd_ref)), stage_ref, sem)

    subs = [slice(i * FFN_SUB_TILE, (i + 1) * FFN_SUB_TILE)
            for i in range(x_ref.shape[0] // FFN_SUB_TILE)]
    xs = [x_ref[rows, :] for rows in subs]
    hs = [_rmsnorm(x, pre_g_ref[...]).astype(jnp.bfloat16) for x in xs]
    gates = [_dot(h, wg_ref[...]) for h in hs]
    ups = [_dot(h, wu_ref[...]) for h in hs]
    acts = [(_silu(g) * u).astype(jnp.bfloat16) for g, u in zip(gates, ups)]
    fs = [_dot(a, wd_ref[...]) for a in acts]
    for rows, x, f in zip(subs, xs, fs):
        o_ref[rows, :] = x + MACARON_WEIGHT * _rmsnorm(f, post_g_ref[...])


def _ffn(x, pre_g, wg, wu, wd, post_g):
    n, d = x.shape
    d_ff = wg.shape[1]
    assert d % WEIGHT_CHUNK_ROWS == 0 and d_ff % WEIGHT_CHUNK_ROWS == 0
    tile = pl.BlockSpec((TOKEN_TILE, d), lambda i: (i, 0))
    hbm = pl.BlockSpec(memory_space=pl.ANY)
    return pl.pallas_call(
        _ffn_kernel,
        grid=(n // TOKEN_TILE,),
        in_specs=[tile, _const_spec((1, d)), hbm, hbm, hbm, _const_spec((1, d))],
        out_specs=tile,
        out_shape=jax.ShapeDtypeStruct((n, d), jnp.float32),
        scratch_shapes=[
            pltpu.VMEM((d, d_ff), jnp.bfloat16), pltpu.VMEM((d, d_ff), jnp.bfloat16),
            pltpu.VMEM((d_ff, d), jnp.bfloat16),
            pltpu.VMEM((STAGE_SLOTS, WEIGHT_CHUNK_ROWS, max(d, d_ff)), jnp.float32),
            pltpu.SemaphoreType.DMA((STAGE_SLOTS,)),
        ],
        compiler_params=_params("arbitrary"),
        name="ffn",
    )(x, pre_g, wg, wu, wd, post_g)


def _proj_kernel(x_ref, g_ref, wqt_ref, wk_ref, wvt_ref, wp_ref,
                 qt_ref, k_ref, kmean_ref, vt_ref, p_ref):
    blocks = [slice(i * MOBA_BLOCK, (i + 1) * MOBA_BLOCK) for i in range(PROJ_TOKEN_TILE // MOBA_BLOCK)]
    hs = [_rmsnorm(x_ref[0, rows, :], g_ref[...]).astype(jnp.bfloat16) for rows in blocks]
    qts, ks, vts, ps = [], [], [], []
    for h in hs:
        qts.append(lax.dot_general(wqt_ref[...], h, _NT, preferred_element_type=jnp.float32))
        ks.append(_dot(h, wk_ref[...]))
        vts.append(lax.dot_general(wvt_ref[...], h, _NT, preferred_element_type=jnp.float32))
        ps.append(_dot(h, wp_ref[...]))
    for blk, rows in enumerate(blocks):
        qt_ref[0, blk] = (qts[blk] * (ATTN_HEAD_DIM ** -0.5 * LOG2_E)).astype(jnp.bfloat16)
        vt_ref[0, blk] = vts[blk].astype(jnp.bfloat16)
        k_ref[0, blk] = ks[blk].astype(jnp.bfloat16)
        kmean_ref[0, blk] = jnp.sum(ks[blk], axis=0, keepdims=True) * (1.0 / MOBA_BLOCK)
        p_ref[0, rows, :] = ps[blk]


def _proj(x, g, wqt, wk, wvt, wp):
    b, s, d = x.shape
    nb = s // MOBA_BLOCK
    bpt = PROJ_TOKEN_TILE // MOBA_BLOCK
    return pl.pallas_call(
        _proj_kernel,
        grid=(b, s // PROJ_TOKEN_TILE),
        in_specs=[pl.BlockSpec((1, PROJ_TOKEN_TILE, d), lambda bi, t: (bi, t, 0)),
                  _const_spec((1, d)), _const_spec((D_ATTN, d)),
                  _const_spec((d, D_ATTN)), _const_spec((D_ATTN, d)),
                  _const_spec((d, D_POOL))],
        out_specs=[
            pl.BlockSpec((1, bpt, D_ATTN, MOBA_BLOCK), lambda bi, t: (bi, t, 0, 0)),
            pl.BlockSpec((1, bpt, MOBA_BLOCK, D_ATTN), lambda bi, t: (bi, t, 0, 0)),
            pl.BlockSpec((1, bpt, 1, D_ATTN), lambda bi, t: (bi, t, 0, 0)),
            pl.BlockSpec((1, bpt, D_ATTN, MOBA_BLOCK), lambda bi, t: (bi, t, 0, 0)),
            pl.BlockSpec((1, PROJ_TOKEN_TILE, D_POOL), lambda bi, t: (bi, t, 0)),
        ],
        out_shape=[
            jax.ShapeDtypeStruct((b, nb, D_ATTN, MOBA_BLOCK), jnp.bfloat16),
            jax.ShapeDtypeStruct((b, nb, MOBA_BLOCK, D_ATTN), jnp.bfloat16),
            jax.ShapeDtypeStruct((b, nb, 1, D_ATTN), jnp.float32),
            jax.ShapeDtypeStruct((b, nb, D_ATTN, MOBA_BLOCK), jnp.bfloat16),
            jax.ShapeDtypeStruct((b, s, D_POOL), jnp.float32),
        ],
        compiler_params=_params("parallel", "parallel"),
        name="mixer_proj",
    )(x, g, wqt, wk, wvt, wp)


MOBA_TILE = 2 * MOBA_BLOCK
SELECT_UNROLL = 4
ONES_ROWS = 16


def _moba_items(n_tiles):
    tiles, pairs = [], []
    for c in range(n_tiles):
        for pr in range(c, -1, -1):
            tiles.append(c)
            pairs.append(pr)
    return np.asarray(tiles, np.int32), np.asarray(pairs, np.int32)


def _moba_kernel(slopes_ref, tile_tab_ref, pair_tab_ref, qt_ref, k_ref, kmean_ref, vt_ref,
                 o_ref, sel_ref, bias_ref, s_ref, p_ref, acc_ref):
    hp = pl.program_id(1)
    L, T = MOBA_BLOCK, MOBA_TILE
    nb = vt_ref.shape[1]
    n_tiles = nb // 2
    n_items = tile_tab_ref.shape[0]
    heads = range(HEADS_PER_STEP)
    slopes = [slopes_ref[hp * HEADS_PER_STEP + hh] for hh in heads]
    head_rows = [slice(hh * ATTN_HEAD_DIM, (hh + 1) * ATTN_HEAD_DIM) for hh in heads]
    head_row = lax.broadcasted_iota(jnp.int32, (HEADS_PER_STEP * ATTN_HEAD_DIM, T), 0) // ATTN_HEAD_DIM

    def query_tile(c):
        qt = jnp.concatenate([qt_ref[0, 2 * c], qt_ref[0, 2 * c + 1]], axis=1)
        return [jnp.where(head_row == hh, qt, jnp.zeros_like(qt)) for hh in heads]

    key = lax.broadcasted_iota(jnp.int32, (T, T), 0)
    query = lax.broadcasted_iota(jnp.int32, (T, T), 1)
    q_minus_k = (query - key).astype(jnp.float32)
    non_causal = jnp.where(key > query, jnp.inf, 0.0)
    for hh in heads:
        base = slopes[hh] * q_minus_k
        bias_ref[hh, 0] = base
        bias_ref[hh, 1] = base + non_causal

    kmean = kmean_ref[0].astype(jnp.bfloat16)
    blk_id = lax.broadcasted_iota(jnp.int32, (nb, T), 0)
    second_half = lax.broadcasted_iota(jnp.int32, (nb, T), 1) // L

    def select_tile(c):
        qt_heads = query_tile(c)
        own = 2 * c + second_half
        for hh in heads:
            gate = jnp.where(blk_id < own, _dot(kmean, qt_heads[hh]), -jnp.inf)
            sel = blk_id == own
            for _ in range(MOBA_TOPK):
                best = jnp.max(gate, axis=0, keepdims=True)
                first = jnp.min(jnp.where(gate == best, blk_id, nb), axis=0, keepdims=True)
                taken = blk_id == first
                sel = sel | (taken & (blk_id < own))
                gate = jnp.where(taken, -jnp.inf, gate)
            sel_ref[hh, c] = sel.astype(jnp.float32)

    def select(it, _):
        for u in range(SELECT_UNROLL):
            select_tile(SELECT_UNROLL * it + u)
        return 0

    lax.fori_loop(0, n_tiles // SELECT_UNROLL, select, 0)

    blk_rows = [slice(blk * L, (blk + 1) * L) for blk in range(2)]
    ones_rows = jnp.ones((ONES_ROWS, L), jnp.bfloat16)

    def biased_scores(e, slot):
        c, pr = tile_tab_ref[e], pair_tab_ref[e]
        variant = jnp.where(pr == c, 1, 0)
        k_pair = k_ref[0, pl.ds(pl.multiple_of(pr * T, T), T), :]
        col_max = []
        for hh, qt_h in enumerate(query_tile(c)):
            t = _dot(k_pair, qt_h) - bias_ref[hh, variant]
            s_ref[slot, hh] = t
            col_max.append(tuple(jnp.max(t[rows, :], axis=0, keepdims=True) for rows in blk_rows))
        return tuple(col_max)

    def value_product(pr, slot):
        out = []
        for hh in heads:
            lhs = [jnp.concatenate([vt_ref[0, 2 * pr + blk, head_rows[hh], :], ones_rows], axis=0)
                   for blk in range(2)]
            out.append(_dot(lhs[0], p_ref[slot, hh, blk_rows[0], :])
                       + _dot(lhs[1], p_ref[slot, hh, blk_rows[1], :]))
        return out

    def write_tile(c):
        o_t = jnp.concatenate(
            [acc_ref[hh, :ATTN_HEAD_DIM, :] / acc_ref[hh, ATTN_HEAD_DIM:ATTN_HEAD_DIM + 1, :]
             for hh in heads], axis=0)
        o_ref[0, pl.ds(pl.multiple_of(c * T, T), T), :] = o_t.T.astype(o_ref.dtype)

    def item(e, slot, state):
        m, alpha_lag1, alpha_lag2, col_max = state
        other = 1 - slot
        c, pr = tile_tab_ref[e], pair_tab_ref[e]
        e_lag2 = jnp.maximum(e - 2, 0)

        pv_lag2 = value_product(pair_tab_ref[e_lag2], slot)
        col_max_next = biased_scores(jnp.minimum(e + 1, n_items - 1), other)

        opens = pr == c
        m_out, alpha_out = [], []
        for hh in heads:
            m_old = jnp.where(opens, -jnp.inf, m[hh])
            offset = slopes[hh] * ((c - pr) * T).astype(jnp.float32)
            keep = [sel_ref[hh, c, pl.ds(2 * pr + blk, 1), :] > 0.5 for blk in range(2)]
            m_blk = [jnp.where(keep[blk], col_max[hh][blk], -jnp.inf) for blk in range(2)]
            m_new = jnp.maximum(m_old, jnp.maximum(m_blk[0], m_blk[1]) - offset)
            for blk in range(2):
                shift = jnp.where(keep[blk], m_new + offset, jnp.inf)
                p = jnp.exp2(s_ref[slot, hh, blk_rows[blk], :] - shift)
                p_ref[slot, hh, blk_rows[blk], :] = p.astype(jnp.bfloat16)
            m_out.append(m_new)
            alpha_out.append(jnp.exp2(m_old - m_new))

        fold(pv_lag2, alpha_lag2)

        @pl.when((pair_tab_ref[e_lag2] == 0) & (e >= 2))
        def _():
            write_tile(tile_tab_ref[e_lag2])

        return tuple(m_out), tuple(alpha_out), alpha_lag1, col_max_next

    def fold(pv, alpha):
        for hh in heads:
            acc_ref[hh] = alpha[hh] * acc_ref[hh] + pv[hh]

    col_max = biased_scores(0, 0)
    for hh in heads:
        acc_ref[hh] = jnp.zeros(acc_ref.shape[1:], jnp.float32)
    p_ref[...] = jnp.zeros(p_ref.shape, p_ref.dtype)
    row = lambda v: tuple(jnp.full((1, T), v, jnp.float32) for _ in heads)
    state = (row(-jnp.inf), row(0.0), row(0.0), col_max)

    def two_items(it, state):
        state = item(2 * it, 0, state)
        return item(2 * it + 1, 1, state)

    _, alpha_lag1, alpha_lag2, _ = lax.fori_loop(0, n_items // 2, two_items, state)

    for e, alpha in ((n_items - 2, alpha_lag2), (n_items - 1, alpha_lag1)):
        fold(value_product(pair_tab_ref[e], e % 2), alpha)
    write_tile(tile_tab_ref[n_items - 1])


def _moba(slopes, qt, k, kmean, vt):
    b, nb, _, L = qt.shape
    lanes = HEADS_PER_STEP * ATTN_HEAD_DIM
    T = MOBA_TILE
    tile_tab, pair_tab = _moba_items(nb // 2)
    assert len(tile_tab) % 2 == 0
    smem = pl.BlockSpec(memory_space=pltpu.SMEM)
    return pl.pallas_call(
        _moba_kernel,
        grid=(b, N_ATTN_HEADS // HEADS_PER_STEP),
        in_specs=[
            smem, smem, smem,
            pl.BlockSpec((1, nb, lanes, L), lambda bi, hp: (bi, 0, hp, 0)),
            pl.BlockSpec((1, nb * L, lanes), lambda bi, hp: (bi, 0, hp)),
            pl.BlockSpec((1, nb, lanes), lambda bi, hp: (bi, 0, hp)),
            pl.BlockSpec((1, nb, lanes, L), lambda bi, hp: (bi, 0, hp, 0)),
        ],
        out_specs=pl.BlockSpec((1, nb * L, lanes), lambda bi, hp: (bi, 0, hp)),
        out_shape=jax.ShapeDtypeStruct((b, nb * L, D_ATTN), jnp.bfloat16),
        scratch_shapes=[
            pltpu.VMEM((HEADS_PER_STEP, nb // 2, nb, T), jnp.float32),
            pltpu.VMEM((HEADS_PER_STEP, 2, T, T), jnp.float32),
            pltpu.VMEM((2, HEADS_PER_STEP, T, T), jnp.float32),
            pltpu.VMEM((2, HEADS_PER_STEP, T, T), jnp.bfloat16),
            pltpu.VMEM((HEADS_PER_STEP, ATTN_HEAD_DIM + ONES_ROWS, T), jnp.float32),
        ],
        compiler_params=_params("parallel", "parallel"),
        name="moba_attention",
    )(slopes, jnp.asarray(tile_tab), jnp.asarray(pair_tab), qt, k, kmean, vt)


def _memkv_kernel(mem_ref, g_ref, wkv_ref, k_ref, v_ref):
    d = mem_ref.shape[2]
    mem_n = _rmsnorm(mem_ref[0], g_ref[...]).astype(jnp.bfloat16)
    kv = _dot(mem_n, wkv_ref[...])
    k_ref[0] = kv[:, :d].astype(jnp.bfloat16)
    v_ref[0] = kv[:, d:].astype(jnp.bfloat16)


def _memkv(mem, g, wkv):
    b, m, d = mem.shape
    blk = pl.BlockSpec((1, m, d), lambda bi: (bi, 0, 0))
    return pl.pallas_call(
        _memkv_kernel,
        grid=(b,),
        in_specs=[blk, _const_spec((1, d)), _const_spec(wkv.shape)],
        out_specs=[blk, blk],
        out_shape=[jax.ShapeDtypeStruct((b, m, d), jnp.bfloat16)] * 2,
        compiler_params=_params("parallel"),
        name="mem_kv",
    )(mem, g, wkv)


def _pooling_mixer(ext, p, first_pos, pool_w_ref, pool_scale_ref):
    rows = p.shape[0]
    pos = first_pos + lax.broadcasted_iota(jnp.int32, (rows, 1), 0)
    pooled = []
    for g, w in enumerate(POOL_WINDOWS):
        cols = slice(g * POOL_GROUP_DIM, (g + 1) * POOL_GROUP_DIM)
        acc = ext[:, cols]
        n_valid = acc.shape[0]
        span = 1
        while span < w:
            acc = acc[span:, :] + acc[:n_valid - span, :]
            n_valid -= span
            span *= 2
        win_sum = acc[n_valid - rows:, :]
        cnt = jnp.minimum(pos + 1, w).astype(jnp.float32)
        d = win_sum / cnt - p[:, cols]
        y = _dot(d.astype(jnp.bfloat16), pool_w_ref[g])
        pooled.append(y * pool_scale_ref[:, cols])
    return jnp.concatenate(pooled, axis=-1).astype(jnp.bfloat16)


def _mix_xattn_kernel(x_ref, attn_ref, p_ref, halo_ref, pool_w_ref, pool_scale_ref,
                      w_out_ref, mix_g_ref, pre_g_ref, wq_ref, k_ref, v_ref, wo_ref,
                      post_g_ref, o_ref):
    t = pl.program_id(1)
    tm, d = x_ref.shape[1], x_ref.shape[2]
    dh = d // XA_HEADS
    subs = [slice(i * MIX_SUB_TILE, (i + 1) * MIX_SUB_TILE) for i in range(tm // MIX_SUB_TILE)]
    halo = jnp.where(t > 0, halo_ref[0], 0.0)
    ext = jnp.concatenate([halo, p_ref[0]], axis=0)
    pools = [_pooling_mixer(ext[rows.start:rows.stop + POOL_HALO, :], p_ref[0, rows, :],
                            t * tm + rows.start, pool_w_ref, pool_scale_ref) for rows in subs]
    ys = [_dot(jnp.concatenate([attn_ref[0, rows, :], pool], axis=-1), w_out_ref[...])
          for rows, pool in zip(subs, pools)]
    xs = [x_ref[0, rows, :] + _rmsnorm(y, mix_g_ref[...]) for rows, y in zip(subs, ys)]

    hs = [_rmsnorm(x, pre_g_ref[...]).astype(jnp.bfloat16) for x in xs]
    qs = [_dot(h, wq_ref[...]).astype(jnp.bfloat16) for h in hs]
    heads, cs = [[] for _ in subs], []
    head_cols = [slice(hd * dh, (hd + 1) * dh) for hd in range(XA_HEADS)]
    ss = [[lax.dot_general(q[:, cols], k_ref[0, :, cols], _NT,
                           preferred_element_type=jnp.float32) * (dh ** -0.5) for q in qs]
          for cols in head_cols]
    for i in range(len(subs)):
        for hd, cols in enumerate(head_cols):
            s = ss[hd][i]
            e = jnp.exp(s - jnp.max(s, axis=-1, keepdims=True))
            p = e / jnp.sum(e, axis=-1, keepdims=True)
            heads[i].append(_dot(p.astype(jnp.bfloat16), v_ref[0, :, cols]))
        cs.append(_dot(jnp.concatenate(heads[i], axis=-1).astype(jnp.bfloat16), wo_ref[...]))
    for rows, x, c in zip(subs, xs, cs):
        o_ref[0, rows, :] = x + _rmsnorm(c, post_g_ref[...])


def _mix_xattn(x, attn, p_in, pool_w, pool_scale, w_out, mix_g, pre_g, wq, k_mem, v_mem, wo,
               post_g):
    b, s, d = x.shape
    m = k_mem.shape[1]
    halo_blocks = MIX_TOKEN_TILE // POOL_HALO
    tile = lambda width: pl.BlockSpec((1, MIX_TOKEN_TILE, width), lambda bi, t: (bi, t, 0))
    mem_blk = pl.BlockSpec((1, m, d), lambda bi, t: (bi, 0, 0))
    return pl.pallas_call(
        _mix_xattn_kernel,
        grid=(b, s // MIX_TOKEN_TILE),
        in_specs=[
            tile(d), tile(D_ATTN), tile(D_POOL),
            pl.BlockSpec((1, POOL_HALO, D_POOL),
                         lambda bi, t: (bi, jnp.maximum(t * halo_blocks - 1, 0), 0)),
            _const_spec(pool_w.shape), _const_spec((1, D_POOL)),
            _const_spec(w_out.shape), _const_spec((1, d)),
            _const_spec((1, d)), _const_spec(wq.shape), mem_blk, mem_blk,
            _const_spec(wo.shape), _const_spec((1, d)),
        ],
        out_specs=tile(d),
        out_shape=jax.ShapeDtypeStruct((b, s, d), jnp.float32),
        compiler_params=_params("parallel", "parallel"),
        name="mixer_out_mem_xattn",
    )(x, attn, p_in, p_in, pool_w, pool_scale, w_out, mix_g, pre_g, wq, k_mem, v_mem, wo, post_g)


def _alibi_slopes(n_heads):
    return jnp.asarray(2.0 ** (-8.0 * np.arange(1, n_heads + 1) / n_heads), jnp.float32)


def kernel(x, mem, ffn1_pre_g, ffn1_w_gate, ffn1_w_up, ffn1_w_down, ffn1_post_g, mix_pre_g, w_in, pool_w, pool_scale, w_out, mix_post_g, xa_pre_g, mem_g, xa_wq, xa_wkv, xa_wo, xa_post_g, ffn2_pre_g, ffn2_w_gate, ffn2_w_up, ffn2_w_down, ffn2_post_g):
    b, s, d = x.shape
    depth = ffn1_pre_g.shape[0]
    bf = lambda w: w.astype(jnp.bfloat16)
    row = lambda g: g.reshape(1, -1)
    slopes = _alibi_slopes(N_ATTN_HEADS) * LOG2_E
    for l in range(depth):
        x = _ffn(x.reshape(b * s, d), row(ffn1_pre_g[l]), ffn1_w_gate[l], ffn1_w_up[l],
                 ffn1_w_down[l], row(ffn1_post_g[l])).reshape(b, s, d)

        w = bf(w_in[l])
        qt, k, kmean, vt, p_in = _proj(
            x, row(mix_pre_g[l]), w[:, :D_ATTN].T, w[:, D_ATTN:2 * D_ATTN],
            w[:, 2 * D_ATTN:3 * D_ATTN].T, w[:, 3 * D_ATTN:])
        attn = _moba(slopes, qt, k.reshape(b, s, D_ATTN), kmean.reshape(b, -1, D_ATTN), vt)
        k_mem, v_mem = _memkv(mem, row(mem_g[l]), bf(xa_wkv[l]))
        x = _mix_xattn(x, attn, p_in, bf(pool_w[l]), row(pool_scale[l]), bf(w_out[l]),
                       row(mix_post_g[l]), row(xa_pre_g[l]), bf(xa_wq[l]), k_mem, v_mem,
                       bf(xa_wo[l]), row(xa_post_g[l]))

        x = _ffn(x.reshape(b * s, d), row(ffn2_pre_g[l]), ffn2_w_gate[l], ffn2_w_up[l],
                 ffn2_w_down[l], row(ffn2_post_g[l])).reshape(b, s, d)
    return x
```

```python
import numpy as np
import jax
import jax.numpy as jnp
from jax import lax
from jax.experimental import pallas as pl
from jax.experimental.pallas import tpu as pltpu

N_ATTN_HEADS = 8
ATTN_HEAD_DIM = 64
D_ATTN = N_ATTN_HEADS * ATTN_HEAD_DIM
POOL_WINDOWS = (2, 4, 8, 16)
POOL_GROUP_DIM = 128
D_POOL = len(POOL_WINDOWS) * POOL_GROUP_DIM
MOBA_BLOCK = 256
MOBA_TOPK = 3
XA_HEADS = 4
MACARON_WEIGHT = 0.5
RMS_EPS = 1e-6
LOG2_E = 1.4426950408889634

V7X_LANES = 128
V7X_VMEM_BYTES = 64 * 1024 * 1024
VMEM_LIMIT_BYTES = V7X_VMEM_BYTES * 7 // 8

TOKEN_TILE = 512
FFN_SUB_TILE = 256
WEIGHT_CHUNK_ROWS = 128
STAGE_SLOTS = 4
PROJ_TOKEN_TILE = 1024
MIX_TOKEN_TILE = 1024
MIX_SUB_TILE = 256
HEADS_PER_STEP = V7X_LANES // ATTN_HEAD_DIM
POOL_HALO = max(POOL_WINDOWS)

_NT = (((1,), (1,)), ((), ()))


def _rmsnorm(x, g):
    r = lax.rsqrt(jnp.mean(x * x, axis=-1, keepdims=True) + RMS_EPS)
    return (x * r) * g


def _dot(a, b):
    return jnp.dot(a, b, preferred_element_type=jnp.float32)


def _params(*semantics):
    return pltpu.CompilerParams(dimension_semantics=semantics,
                                vmem_limit_bytes=VMEM_LIMIT_BYTES)


def _const_spec(shape):
    return pl.BlockSpec(shape, lambda *_: (0,) * len(shape))


def _stage_cast(pairs, stage_ref, sem):
    chunks = [(src, dst, r) for src, dst in pairs
              for r in range(0, src.shape[0], WEIGHT_CHUNK_ROWS)]

    def chunk_copy(c):
        src, _, r = chunks[c]
        slot = c % STAGE_SLOTS
        return pltpu.make_async_copy(
            src.at[pl.ds(r, WEIGHT_CHUNK_ROWS), :],
            stage_ref.at[slot, :, pl.ds(0, src.shape[1])], sem.at[slot])

    ahead = STAGE_SLOTS - 1
    for c in range(min(ahead, len(chunks))):
        chunk_copy(c).start()
    for c, (src, dst, r) in enumerate(chunks):
        if c + ahead < len(chunks):
            chunk_copy(c + ahead).start()
        chunk_copy(c).wait()
        dst[pl.ds(r, WEIGHT_CHUNK_ROWS), :] = (
            stage_ref[c % STAGE_SLOTS, :, pl.ds(0, src.shape[1])].astype(jnp.bfloat16))


def _ffn_kernel(x_ref, pre_g_ref, wg_hbm, wu_hbm, wd_hbm, post_g_ref, o_ref,
                wg_ref, wu_ref, wd_ref, stage_ref, sem):
    @pl.when(pl.program_id(0) == 0)
    def _():
        _stage_cast(((wg_hbm, wg_ref), (wu_hbm, wu_ref), (wd_hbm, wd_ref)), stage_ref, sem)

    subs = [slice(i * FFN_SUB_TILE, (i + 1) * FFN_SUB_TILE)
            for i in range(x_ref.shape[0] // FFN_SUB_TILE)]
    xs = [x_ref[rows, :] for rows in subs]
    hs = [_rmsnorm(x, pre_g_ref[...]).astype(jnp.bfloat16) for x in xs]
    gates = [_dot(h, wg_ref[...]) for h in hs]
    ups = [_dot(h, wu_ref[...]) for h in hs]
    acts = [(g * jax.nn.sigmoid(g) * u).astype(jnp.bfloat16) for g, u in zip(gates, ups)]
    fs = [_dot(a, wd_ref[...]) for a in acts]
    for rows, x, f in zip(subs, xs, fs):
        o_ref[rows, :] = x + MACARON_WEIGHT * _rmsnorm(f, post_g_ref[...])


def _ffn(x, pre_g, wg, wu, wd, post_g):
    n, d = x.shape
    d_ff = wg.shape[1]
    assert d % WEIGHT_CHUNK_ROWS == 0 and d_ff % WEIGHT_CHUNK_ROWS == 0
    tile = pl.BlockSpec((TOKEN_TILE, d), lambda i: (i, 0))
    hbm = pl.BlockSpec(memory_space=pl.ANY)
    return pl.pallas_call(
        _ffn_kernel,
        grid=(n // TOKEN_TILE,),
        in_specs=[tile, _const_spec((1, d)), hbm, hbm, hbm, _const_spec((1, d))],
        out_specs=tile,
        out_shape=jax.ShapeDtypeStruct((n, d), jnp.float32),
        scratch_shapes=[
            pltpu.VMEM((d, d_ff), jnp.bfloat16), pltpu.VMEM((d, d_ff), jnp.bfloat16),
            pltpu.VMEM((d_ff, d), jnp.bfloat16),
            pltpu.VMEM((STAGE_SLOTS, WEIGHT_CHUNK_ROWS, max(d, d_ff)), jnp.float32),
            pltpu.SemaphoreType.DMA((STAGE_SLOTS,)),
        ],
        compiler_params=_params("arbitrary"),
        name="ffn",
    )(x, pre_g, wg, wu, wd, post_g)


def _proj_kernel(x_ref, g_ref, wqt_ref, wk_ref, wvt_ref, wp_ref,
                 qt_ref, k_ref, kmean_ref, vt_ref, p_ref):
    blocks = [slice(i * MOBA_BLOCK, (i + 1) * MOBA_BLOCK) for i in range(PROJ_TOKEN_TILE // MOBA_BLOCK)]
    hs = [_rmsnorm(x_ref[0, rows, :], g_ref[...]).astype(jnp.bfloat16) for rows in blocks]
    qts, ks, vts, ps = [], [], [], []
    for h in hs:
        qts.append(lax.dot_general(wqt_ref[...], h, _NT, preferred_element_type=jnp.float32))
        ks.append(_dot(h, wk_ref[...]))
        vts.append(lax.dot_general(wvt_ref[...], h, _NT, preferred_element_type=jnp.float32))
        ps.append(_dot(h, wp_ref[...]))
    for blk, rows in enumerate(blocks):
        qt_ref[0, blk] = (qts[blk] * (ATTN_HEAD_DIM ** -0.5 * LOG2_E)).astype(jnp.bfloat16)
        vt_ref[0, blk] = vts[blk].astype(jnp.bfloat16)
        k_ref[0, blk] = ks[blk].astype(jnp.bfloat16)
        kmean_ref[0, blk] = jnp.sum(ks[blk], axis=0, keepdims=True) * (1.0 / MOBA_BLOCK)
        p_ref[0, rows, :] = ps[blk]


def _proj(x, g, wqt, wk, wvt, wp):
    b, s, d = x.shape
    nb = s // MOBA_BLOCK
    bpt = PROJ_TOKEN_TILE // MOBA_BLOCK
    return pl.pallas_call(
        _proj_kernel,
        grid=(b, s // PROJ_TOKEN_TILE),
        in_specs=[pl.BlockSpec((1, PROJ_TOKEN_TILE, d), lambda bi, t: (bi, t, 0)),
                  _const_spec((1, d)), _const_spec((D_ATTN, d)),
                  _const_spec((d, D_ATTN)), _const_spec((D_ATTN, d)),
                  _const_spec((d, D_POOL))],
        out_specs=[
            pl.BlockSpec((1, bpt, D_ATTN, MOBA_BLOCK), lambda bi, t: (bi, t, 0, 0)),
            pl.BlockSpec((1, bpt, MOBA_BLOCK, D_ATTN), lambda bi, t: (bi, t, 0, 0)),
            pl.BlockSpec((1, bpt, 1, D_ATTN), lambda bi, t: (bi, t, 0, 0)),
            pl.BlockSpec((1, bpt, D_ATTN, MOBA_BLOCK), lambda bi, t: (bi, t, 0, 0)),
            pl.BlockSpec((1, PROJ_TOKEN_TILE, D_POOL), lambda bi, t: (bi, t, 0)),
        ],
        out_shape=[
            jax.ShapeDtypeStruct((b, nb, D_ATTN, MOBA_BLOCK), jnp.bfloat16),
            jax.ShapeDtypeStruct((b, nb, MOBA_BLOCK, D_ATTN), jnp.bfloat16),
            jax.ShapeDtypeStruct((b, nb, 1, D_ATTN), jnp.float32),
            jax.ShapeDtypeStruct((b, nb, D_ATTN, MOBA_BLOCK), jnp.bfloat16),
            jax.ShapeDtypeStruct((b, s, D_POOL), jnp.float32),
        ],
        compiler_params=_params("parallel", "parallel"),
        name="mixer_proj",
    )(x, g, wqt, wk, wvt, wp)


MOBA_TILE = 2 * MOBA_BLOCK
SELECT_UNROLL = 4
ONES_ROWS = 16


def _moba_items(n_tiles):
    tiles, pairs = [], []
    for c in range(n_tiles):
        for pr in range(c, -1, -1):
            tiles.append(c)
            pairs.append(pr)
    return np.asarray(tiles, np.int32), np.asarray(pairs, np.int32)


def _moba_kernel(slopes_ref, tile_tab_ref, pair_tab_ref, qt_ref, k_ref, kmean_ref, vt_ref,
                 o_ref, sel_ref, bias_ref, s_ref, p_ref, acc_ref):
    hp = pl.program_id(1)
    L, T = MOBA_BLOCK, MOBA_TILE
    nb = vt_ref.shape[1]
    n_tiles = nb // 2
    n_items = tile_tab_ref.shape[0]
    heads = range(HEADS_PER_STEP)
    slopes = [slopes_ref[hp * HEADS_PER_STEP + hh] for hh in heads]
    head_rows = [slice(hh * ATTN_HEAD_DIM, (hh + 1) * ATTN_HEAD_DIM) for hh in heads]
    head_row = lax.broadcasted_iota(jnp.int32, (HEADS_PER_STEP * ATTN_HEAD_DIM, T), 0) // ATTN_HEAD_DIM

    def query_tile(c):
        qt = jnp.concatenate([qt_ref[0, 2 * c], qt_ref[0, 2 * c + 1]], axis=1)
        return [jnp.where(head_row == hh, qt, jnp.zeros_like(qt)) for hh in heads]

    key = lax.broadcasted_iota(jnp.int32, (T, T), 0)
    query = lax.broadcasted_iota(jnp.int32, (T, T), 1)
    q_minus_k = (query - key).astype(jnp.float32)
    non_causal = jnp.where(key > query, jnp.inf, 0.0)
    for hh in heads:
        base = slopes[hh] * q_minus_k
        bias_ref[hh, 0] = base
        bias_ref[hh, 1] = base + non_causal

    kmean = kmean_ref[0].astype(jnp.bfloat16)
    blk_id = lax.broadcasted_iota(jnp.int32, (nb, T), 0)
    second_half = lax.broadcasted_iota(jnp.int32, (nb, T), 1) // L

    def select_tile(c):
        qt_heads = query_tile(c)
        own = 2 * c + second_half
        for hh in heads:
            gate = jnp.where(blk_id < own, _dot(kmean, qt_heads[hh]), -jnp.inf)
            sel = blk_id == own
            for _ in range(MOBA_TOPK):
                best = jnp.max(gate, axis=0, keepdims=True)
                first = jnp.min(jnp.where(gate == best, blk_id, nb), axis=0, keepdims=True)
                taken = blk_id == first
                sel = sel | (taken & (blk_id < own))
                gate = jnp.where(taken, -jnp.inf, gate)
            sel_ref[hh, c] = sel.astype(jnp.float32)

    def select(it, _):
        for u in range(SELECT_UNROLL):
            select_tile(SELECT_UNROLL * it + u)
        return 0

    lax.fori_loop(0, n_tiles // SELECT_UNROLL, select, 0)

    blk_rows = [slice(blk * L, (blk + 1) * L) for blk in range(2)]
    ones_rows = jnp.ones((ONES_ROWS, L), jnp.bfloat16)

    def biased_scores(e, slot):
        c, pr = tile_tab_ref[e], pair_tab_ref[e]
        variant = jnp.where(pr == c, 1, 0)
        k_pair = k_ref[0, pl.ds(pl.multiple_of(pr * T, T), T), :]
        col_max = []
        for hh, qt_h in enumerate(query_tile(c)):
            t = _dot(k_pair, qt_h) - bias_ref[hh, variant]
            s_ref[slot, hh] = t
            col_max.append(tuple(jnp.max(t[rows, :], axis=0, keepdims=True) for rows in blk_rows))
        return tuple(col_max)

    def value_product(pr, slot):
        out = []
        for hh in heads:
            lhs = [jnp.concatenate([vt_ref[0, 2 * pr + blk, head_rows[hh], :], ones_rows], axis=0)
                   for blk in range(2)]
            out.append(_dot(lhs[0], p_ref[slot, hh, blk_rows[0], :])
                       + _dot(lhs[1], p_ref[slot, hh, blk_rows[1], :]))
        return out

    def write_tile(c):
        o_t = jnp.concatenate(
            [acc_ref[hh, :ATTN_HEAD_DIM, :] / acc_ref[hh, ATTN_HEAD_DIM:ATTN_HEAD_DIM + 1, :]
             for hh in heads], axis=0)
        o_ref[0, pl.ds(pl.multiple_of(c * T, T), T), :] = o_t.T.astype(o_ref.dtype)

    def item(e, slot, state):
        m, alpha_lag1, alpha_lag2, col_max = state
        other = 1 - slot
        c, pr = tile_tab_ref[e], pair_tab_ref[e]
        e_lag2 = jnp.maximum(e - 2, 0)

        pv_lag2 = value_product(pair_tab_ref[e_lag2], slot)
        col_max_next = biased_scores(jnp.minimum(e + 1, n_items - 1), other)

        opens = pr == c
        m_out, alpha_out = [], []
        for hh in heads:
            m_old = jnp.where(opens, -jnp.inf, m[hh])
            offset = slopes[hh] * ((c - pr) * T).astype(jnp.float32)
            keep = [sel_ref[hh, c, pl.ds(2 * pr + blk, 1), :] > 0.5 for blk in range(2)]
            m_blk = [jnp.where(keep[blk], col_max[hh][blk], -jnp.inf) for blk in range(2)]
            m_new = jnp.maximum(m_old, jnp.maximum(m_blk[0], m_blk[1]) - offset)
            for blk in range(2):
                shift = jnp.where(keep[blk], m_new + offset, jnp.inf)
                p = jnp.exp2(s_ref[slot, hh, blk_rows[blk], :] - shift)
                p_ref[slot, hh, blk_rows[blk], :] = p.astype(jnp.bfloat16)
            m_out.append(m_new)
            alpha_out.append(jnp.exp2(m_old - m_new))

        fold(pv_lag2, alpha_lag2)

        @pl.when((pair_tab_ref[e_lag2] == 0) & (e >= 2))
        def _():
            write_tile(tile_tab_ref[e_lag2])

        return tuple(m_out), tuple(alpha_out), alpha_lag1, col_max_next

    def fold(pv, alpha):
        for hh in heads:
            acc_ref[hh] = alpha[hh] * acc_ref[hh] + pv[hh]

    col_max = biased_scores(0, 0)
    for hh in heads:
        acc_ref[hh] = jnp.zeros(acc_ref.shape[1:], jnp.float32)
    p_ref[...] = jnp.zeros(p_ref.shape, p_ref.dtype)
    row = lambda v: tuple(jnp.full((1, T), v, jnp.float32) for _ in heads)
    state = (row(-jnp.inf), row(0.0), row(0.0), col_max)

    def two_items(it, state):
        state = item(2 * it, 0, state)
        return item(2 * it + 1, 1, state)

    _, alpha_lag1, alpha_lag2, _ = lax.fori_loop(0, n_items // 2, two_items, state)

    for e, alpha in ((n_items - 2, alpha_lag2), (n_items - 1, alpha_lag1)):
        fold(value_product(pair_tab_ref[e], e % 2), alpha)
    write_tile(tile_tab_ref[n_items - 1])


def _moba(slopes, qt, k, kmean, vt):
    b, nb, _, L = qt.shape
    lanes = HEADS_PER_STEP * ATTN_HEAD_DIM
    T = MOBA_TILE
    tile_tab, pair_tab = _moba_items(nb // 2)
    assert len(tile_tab) % 2 == 0
    smem = pl.BlockSpec(memory_space=pltpu.SMEM)
    return pl.pallas_call(
        _moba_kernel,
        grid=(b, N_ATTN_HEADS // HEADS_PER_STEP),
        in_specs=[
            smem, smem, smem,
            pl.BlockSpec((1, nb, lanes, L), lambda bi, hp: (bi, 0, hp, 0)),
            pl.BlockSpec((1, nb * L, lanes), lambda bi, hp: (bi, 0, hp)),
            pl.BlockSpec((1, nb, lanes), lambda bi, hp: (bi, 0, hp)),
            pl.BlockSpec((1, nb, lanes, L), lambda bi, hp: (bi, 0, hp, 0)),
        ],
        out_specs=pl.BlockSpec((1, nb * L, lanes), lambda bi, hp: (bi, 0, hp)),
        out_shape=jax.ShapeDtypeStruct((b, nb * L, D_ATTN), jnp.bfloat16),
        scratch_shapes=[
            pltpu.VMEM((HEADS_PER_STEP, nb // 2, nb, T), jnp.float32),
            pltpu.VMEM((HEADS_PER_STEP, 2, T, T), jnp.float32),
            pltpu.VMEM((2, HEADS_PER_STEP, T, T), jnp.float32),
            pltpu.VMEM((2, HEADS_PER_STEP, T, T), jnp.bfloat16),
            pltpu.VMEM((HEADS_PER_STEP, ATTN_HEAD_DIM + ONES_ROWS, T), jnp.float32),
        ],
        compiler_params=_params("parallel", "parallel"),
        name="moba_attention",
    )(slopes, jnp.asarray(tile_tab), jnp.asarray(pair_tab), qt, k, kmean, vt)


def _memkv_kernel(mem_ref, g_ref, wkv_ref, k_ref, v_ref):
    d = mem_ref.shape[2]
    mem_n = _rmsnorm(mem_ref[0], g_ref[...]).astype(jnp.bfloat16)
    kv = _dot(mem_n, wkv_ref[...])
    k_ref[0] = kv[:, :d].astype(jnp.bfloat16)
    v_ref[0] = kv[:, d:].astype(jnp.bfloat16)


def _memkv(mem, g, wkv):
    b, m, d = mem.shape
    blk = pl.BlockSpec((1, m, d), lambda bi: (bi, 0, 0))
    return pl.pallas_call(
        _memkv_kernel,
        grid=(b,),
        in_specs=[blk, _const_spec((1, d)), _const_spec(wkv.shape)],
        out_specs=[blk, blk],
        out_shape=[jax.ShapeDtypeStruct((b, m, d), jnp.bfloat16)] * 2,
        compiler_params=pltpu.CompilerParams(
            dimension_semantics=("parallel",), vmem_limit_bytes=VMEM_LIMIT_BYTES,
            allow_input_fusion=[False, False, True]),
        name="mem_kv",
    )(mem, g, wkv)


def _pooling_mixer(ext, p, first_pos, pool_w_ref, pool_scale_ref):
    rows = p.shape[0]
    pos = first_pos + lax.broadcasted_iota(jnp.int32, (rows, 1), 0)
    pooled = []
    for g, w in enumerate(POOL_WINDOWS):
        cols = slice(g * POOL_GROUP_DIM, (g + 1) * POOL_GROUP_DIM)
        acc = ext[:, cols]
        n_valid = acc.shape[0]
        span = 1
        while span < w:
            acc = acc[span:, :] + acc[:n_valid - span, :]
            n_valid -= span
            span *= 2
        win_sum = acc[n_valid - rows:, :]
        cnt = jnp.minimum(pos + 1, w).astype(jnp.float32)
        d = win_sum / cnt - p[:, cols]
        y = _dot(d.astype(jnp.bfloat16), pool_w_ref[g])
        pooled.append(y * pool_scale_ref[:, cols])
    return jnp.concatenate(pooled, axis=-1).astype(jnp.bfloat16)


def _mix_xattn_kernel(x_ref, attn_ref, p_ref, halo_ref, pool_w_ref, pool_scale_ref,
                      w_out_ref, mix_g_ref, pre_g_ref, wq_ref, k_ref, v_ref, wo_ref,
                      post_g_ref, o_ref):
    t = pl.program_id(1)
    tm, d = x_ref.shape[1], x_ref.shape[2]
    dh = d // XA_HEADS
    subs = [slice(i * MIX_SUB_TILE, (i + 1) * MIX_SUB_TILE) for i in range(tm // MIX_SUB_TILE)]
    halo = jnp.where(t > 0, halo_ref[0], 0.0)
    ext = jnp.concatenate([halo, p_ref[0]], axis=0)
    pools = [_pooling_mixer(ext[rows.start:rows.stop + POOL_HALO, :], p_ref[0, rows, :],
                            t * tm + rows.start, pool_w_ref, pool_scale_ref) for rows in subs]
    ys = [_dot(jnp.concatenate([attn_ref[0, rows, :], pool], axis=-1), w_out_ref[...])
          for rows, pool in zip(subs, pools)]
    xs = [x_ref[0, rows, :] + _rmsnorm(y, mix_g_ref[...]) for rows, y in zip(subs, ys)]

    hs = [_rmsnorm(x, pre_g_ref[...]).astype(jnp.bfloat16) for x in xs]
    qs = [_dot(h, wq_ref[...]).astype(jnp.bfloat16) for h in hs]
    heads, cs = [[] for _ in subs], []
    head_cols = [slice(hd * dh, (hd + 1) * dh) for hd in range(XA_HEADS)]
    ss = [[lax.dot_general(q[:, cols], k_ref[0, :, cols], _NT,
                           preferred_element_type=jnp.float32) * (dh ** -0.5) for q in qs]
          for cols in head_cols]
    for i in range(len(subs)):
        for hd, cols in enumerate(head_cols):
            s = ss[hd][i]
            e = jnp.exp(s - jnp.max(s, axis=-1, keepdims=True))
            p = e / jnp.sum(e, axis=-1, keepdims=True)
            heads[i].append(_dot(p.astype(jnp.bfloat16), v_ref[0, :, cols]))
        cs.append(_dot(jnp.concatenate(heads[i], axis=-1).astype(jnp.bfloat16), wo_ref[...]))
    for rows, x, c in zip(subs, xs, cs):
        o_ref[0, rows, :] = x + _rmsnorm(c, post_g_ref[...])


def _mix_xattn(x, attn, p_in, pool_w, pool_scale, w_out, mix_g, pre_g, wq, k_mem, v_mem, wo,
               post_g):
    b, s, d = x.shape
    m = k_mem.shape[1]
    halo_blocks = MIX_TOKEN_TILE // POOL_HALO
    tile = lambda width: pl.BlockSpec((1, MIX_TOKEN_TILE, width), lambda bi, t: (bi, t, 0))
    mem_blk = pl.BlockSpec((1, m, d), lambda bi, t: (bi, 0, 0))
    return pl.pallas_call(
        _mix_xattn_kernel,
        grid=(b, s // MIX_TOKEN_TILE),
        in_specs=[
            tile(d), tile(D_ATTN), tile(D_POOL),
            pl.BlockSpec((1, POOL_HALO, D_POOL),
                         lambda bi, t: (bi, jnp.maximum(t * halo_blocks - 1, 0), 0)),
            _const_spec(pool_w.shape), _const_spec((1, D_POOL)),
            _const_spec(w_out.shape), _const_spec((1, d)),
            _const_spec((1, d)), _const_spec(wq.shape), mem_blk, mem_blk,
            _const_spec(wo.shape), _const_spec((1, d)),
        ],
        out_specs=tile(d),
        out_shape=jax.ShapeDtypeStruct((b, s, d), jnp.float32),
        compiler_params=_params("parallel", "parallel"),
        name="mixer_out_mem_xattn",
    )(x, attn, p_in, p_in, pool_w, pool_scale, w_out, mix_g, pre_g, wq, k_mem, v_mem, wo, post_g)


def _alibi_slopes(n_heads):
    return jnp.asarray(2.0 ** (-8.0 * np.arange(1, n_heads + 1) / n_heads), jnp.float32)


def kernel(x, mem, ffn1_pre_g, ffn1_w_gate, ffn1_w_up, ffn1_w_down, ffn1_post_g, mix_pre_g, w_in, pool_w, pool_scale, w_out, mix_post_g, xa_pre_g, mem_g, xa_wq, xa_wkv, xa_wo, xa_post_g, ffn2_pre_g, ffn2_w_gate, ffn2_w_up, ffn2_w_down, ffn2_post_g):
    b, s, d = x.shape
    depth = ffn1_pre_g.shape[0]
    bf = lambda w: w.astype(jnp.bfloat16)
    row = lambda g: g.reshape(1, -1)
    slopes = _alibi_slopes(N_ATTN_HEADS) * LOG2_E
    for l in range(depth):
        x = _ffn(x.reshape(b * s, d), row(ffn1_pre_g[l]), ffn1_w_gate[l], ffn1_w_up[l],
                 ffn1_w_down[l], row(ffn1_post_g[l])).reshape(b, s, d)

        w = bf(w_in[l])
        qt, k, kmean, vt, p_in = _proj(
            x, row(mix_pre_g[l]), w[:, :D_ATTN].T, w[:, D_ATTN:2 * D_ATTN],
            w[:, 2 * D_ATTN:3 * D_ATTN].T, w[:, 3 * D_ATTN:])
        attn = _moba(slopes, qt, k.reshape(b, s, D_ATTN), kmean.reshape(b, -1, D_ATTN), vt)
        k_mem, v_mem = _memkv(mem, row(mem_g[l]), bf(xa_wkv[l]))
        x = _mix_xattn(x, attn, p_in, bf(pool_w[l]), row(pool_scale[l]), bf(w_out[l]),
                       row(mix_post_g[l]), row(xa_pre_g[l]), bf(xa_wq[l]), k_mem, v_mem,
                       bf(xa_wo[l]), row(xa_post_g[l]))

        x = _ffn(x.reshape(b * s, d), row(ffn2_pre_g[l]), ffn2_w_gate[l], ffn2_w_up[l],
                 ffn2_w_down[l], row(ffn2_post_g[l])).reshape(b, s, d)
    return x
```

```python
import numpy as np
import jax
import jax.numpy as jnp
from jax import lax
from jax.experimental import pallas as pl
from jax.experimental.pallas import tpu as pltpu

N_ATTN_HEADS = 8
ATTN_HEAD_DIM = 64
D_ATTN = N_ATTN_HEADS * ATTN_HEAD_DIM
POOL_WINDOWS = (2, 4, 8, 16)
POOL_GROUP_DIM = 128
D_POOL = len(POOL_WINDOWS) * POOL_GROUP_DIM
MOBA_BLOCK = 256
MOBA_TOPK = 3
XA_HEADS = 4
MACARON_WEIGHT = 0.5
RMS_EPS = 1e-6
LOG2_E = 1.4426950408889634

V7X_LANES = 128
V7X_VMEM_BYTES = 64 * 1024 * 1024
VMEM_LIMIT_BYTES = V7X_VMEM_BYTES * 7 // 8

TOKEN_TILE = 512
FFN_SUB_TILE = 256
WEIGHT_CHUNK_ROWS = 128
STAGE_SLOTS = 6
PROJ_TOKEN_TILE = 1024
MIX_TOKEN_TILE = 1024
MIX_SUB_TILE = 256
HEADS_PER_STEP = V7X_LANES // ATTN_HEAD_DIM
POOL_HALO = max(POOL_WINDOWS)

_NT = (((1,), (1,)), ((), ()))


def _rmsnorm(x, g):
    r = lax.rsqrt(jnp.mean(x * x, axis=-1, keepdims=True) + RMS_EPS)
    return (x * r) * g


def _dot(a, b):
    return jnp.dot(a, b, preferred_element_type=jnp.float32)


def _params(*semantics):
    return pltpu.CompilerParams(dimension_semantics=semantics,
                                vmem_limit_bytes=VMEM_LIMIT_BYTES)


def _const_spec(shape):
    return pl.BlockSpec(shape, lambda *_: (0,) * len(shape))


def _stage_cast(pairs, stage_ref, sem):
    chunks = [(src, dst, r) for src, dst in pairs
              for r in range(0, src.shape[0], WEIGHT_CHUNK_ROWS)]

    def chunk_copy(c):
        src, _, r = chunks[c]
        slot = c % STAGE_SLOTS
        return pltpu.make_async_copy(
            src.at[pl.ds(r, WEIGHT_CHUNK_ROWS), :],
            stage_ref.at[slot, :, pl.ds(0, src.shape[1])], sem.at[slot])

    ahead = STAGE_SLOTS - 1
    for c in range(min(ahead, len(chunks))):
        chunk_copy(c).start()
    for c, (src, dst, r) in enumerate(chunks):
        if c + ahead < len(chunks):
            chunk_copy(c + ahead).start()
        chunk_copy(c).wait()
        dst[pl.ds(r, WEIGHT_CHUNK_ROWS), :] = (
            stage_ref[c % STAGE_SLOTS, :, pl.ds(0, src.shape[1])].astype(jnp.bfloat16))


def _ffn_kernel(x_ref, pre_g_ref, wg_hbm, wu_hbm, wd_hbm, post_g_ref, o_ref,
                wg_ref, wu_ref, wd_ref, stage_ref, sem):
    @pl.when(pl.program_id(0) == 0)
    def _():
        _stage_cast(((wg_hbm, wg_ref), (wu_hbm, wu_ref), (wd_hbm, wd_ref)), stage_ref, sem)

    subs = [slice(i * FFN_SUB_TILE, (i + 1) * FFN_SUB_TILE)
            for i in range(x_ref.shape[0] // FFN_SUB_TILE)]
    xs = [x_ref[rows, :] for rows in subs]
    hs = [_rmsnorm(x, pre_g_ref[...]).astype(jnp.bfloat16) for x in xs]
    gates = [_dot(h, wg_ref[...]) for h in hs]
    ups = [_dot(h, wu_ref[...]) for h in hs]
    acts = [(g * jax.nn.sigmoid(g) * u).astype(jnp.bfloat16) for g, u in zip(gates, ups)]
    fs = [_dot(a, wd_ref[...]) for a in acts]
    for rows, x, f in zip(subs, xs, fs):
        o_ref[rows, :] = x + MACARON_WEIGHT * _rmsnorm(f, post_g_ref[...])


def _ffn(x, pre_g, wg, wu, wd, post_g):
    n, d = x.shape
    d_ff = wg.shape[1]
    assert d % WEIGHT_CHUNK_ROWS == 0 and d_ff % WEIGHT_CHUNK_ROWS == 0
    tile = pl.BlockSpec((TOKEN_TILE, d), lambda i: (i, 0))
    hbm = pl.BlockSpec(memory_space=pl.ANY)
    return pl.pallas_call(
        _ffn_kernel,
        grid=(n // TOKEN_TILE,),
        in_specs=[tile, _const_spec((1, d)), hbm, hbm, hbm, _const_spec((1, d))],
        out_specs=tile,
        out_shape=jax.ShapeDtypeStruct((n, d), jnp.float32),
        scratch_shapes=[
            pltpu.VMEM((d, d_ff), jnp.bfloat16), pltpu.VMEM((d, d_ff), jnp.bfloat16),
            pltpu.VMEM((d_ff, d), jnp.bfloat16),
            pltpu.VMEM((STAGE_SLOTS, WEIGHT_CHUNK_ROWS, max(d, d_ff)), jnp.float32),
            pltpu.SemaphoreType.DMA((STAGE_SLOTS,)),
        ],
        compiler_params=_params("arbitrary"),
        name="ffn",
    )(x, pre_g, wg, wu, wd, post_g)


def _proj_kernel(x_ref, g_ref, wqt_ref, wk_ref, wvt_ref, wp_ref,
                 qt_ref, k_ref, kmean_ref, vt_ref, p_ref):
    blocks = [slice(i * MOBA_BLOCK, (i + 1) * MOBA_BLOCK) for i in range(PROJ_TOKEN_TILE // MOBA_BLOCK)]
    hs = [_rmsnorm(x_ref[0, rows, :], g_ref[...]).astype(jnp.bfloat16) for rows in blocks]
    qts, ks, vts, ps = [], [], [], []
    for h in hs:
        qts.append(lax.dot_general(wqt_ref[...], h, _NT, preferred_element_type=jnp.float32))
        ks.append(_dot(h, wk_ref[...]))
        vts.append(lax.dot_general(wvt_ref[...], h, _NT, preferred_element_type=jnp.float32))
        ps.append(_dot(h, wp_ref[...]))
    for blk, rows in enumerate(blocks):
        qt_ref[0, blk] = (qts[blk] * (ATTN_HEAD_DIM ** -0.5 * LOG2_E)).astype(jnp.bfloat16)
        vt_ref[0, blk] = vts[blk].astype(jnp.bfloat16)
        k_ref[0, blk] = ks[blk].astype(jnp.bfloat16)
        kmean_ref[0, blk] = jnp.sum(ks[blk], axis=0, keepdims=True) * (1.0 / MOBA_BLOCK)
        p_ref[0, rows, :] = ps[blk]


def _proj(x, g, wqt, wk, wvt, wp):
    b, s, d = x.shape
    nb = s // MOBA_BLOCK
    bpt = PROJ_TOKEN_TILE // MOBA_BLOCK
    return pl.pallas_call(
        _proj_kernel,
        grid=(b, s // PROJ_TOKEN_TILE),
        in_specs=[pl.BlockSpec((1, PROJ_TOKEN_TILE, d), lambda bi, t: (bi, t, 0)),
                  _const_spec((1, d)), _const_spec((D_ATTN, d)),
                  _const_spec((d, D_ATTN)), _const_spec((D_ATTN, d)),
                  _const_spec((d, D_POOL))],
        out_specs=[
            pl.BlockSpec((1, bpt, D_ATTN, MOBA_BLOCK), lambda bi, t: (bi, t, 0, 0)),
            pl.BlockSpec((1, bpt, MOBA_BLOCK, D_ATTN), lambda bi, t: (bi, t, 0, 0)),
            pl.BlockSpec((1, bpt, 1, D_ATTN), lambda bi, t: (bi, t, 0, 0)),
            pl.BlockSpec((1, bpt, D_ATTN, MOBA_BLOCK), lambda bi, t: (bi, t, 0, 0)),
            pl.BlockSpec((1, PROJ_TOKEN_TILE, D_POOL), lambda bi, t: (bi, t, 0)),
        ],
        out_shape=[
            jax.ShapeDtypeStruct((b, nb, D_ATTN, MOBA_BLOCK), jnp.bfloat16),
            jax.ShapeDtypeStruct((b, nb, MOBA_BLOCK, D_ATTN), jnp.bfloat16),
            jax.ShapeDtypeStruct((b, nb, 1, D_ATTN), jnp.float32),
            jax.ShapeDtypeStruct((b, nb, D_ATTN, MOBA_BLOCK), jnp.bfloat16),
            jax.ShapeDtypeStruct((b, s, D_POOL), jnp.float32),
        ],
        compiler_params=_params("parallel", "parallel"),
        name="mixer_proj",
    )(x, g, wqt, wk, wvt, wp)


MOBA_TILE = 2 * MOBA_BLOCK
SELECT_UNROLL = 4
ONES_ROWS = 16


def _moba_items(n_tiles):
    tiles, pairs = [], []
    for c in range(n_tiles):
        for pr in range(c, -1, -1):
            tiles.append(c)
            pairs.append(pr)
    return np.asarray(tiles, np.int32), np.asarray(pairs, np.int32)


def _moba_kernel(slopes_ref, tile_tab_ref, pair_tab_ref, qt_ref, k_ref, kmean_ref, vt_ref,
                 o_ref, sel_ref, bias_ref, s_ref, p_ref, acc_ref):
    hp = pl.program_id(1)
    L, T = MOBA_BLOCK, MOBA_TILE
    nb = vt_ref.shape[1]
    n_tiles = nb // 2
    n_items = tile_tab_ref.shape[0]
    heads = range(HEADS_PER_STEP)
    slopes = [slopes_ref[hp * HEADS_PER_STEP + hh] for hh in heads]
    head_rows = [slice(hh * ATTN_HEAD_DIM, (hh + 1) * ATTN_HEAD_DIM) for hh in heads]
    head_row = lax.broadcasted_iota(jnp.int32, (HEADS_PER_STEP * ATTN_HEAD_DIM, T), 0) // ATTN_HEAD_DIM

    def query_tile(c):
        qt = jnp.concatenate([qt_ref[0, 2 * c], qt_ref[0, 2 * c + 1]], axis=1)
        return [jnp.where(head_row == hh, qt, jnp.zeros_like(qt)) for hh in heads]

    key = lax.broadcasted_iota(jnp.int32, (T, T), 0)
    query = lax.broadcasted_iota(jnp.int32, (T, T), 1)
    q_minus_k = (query - key).astype(jnp.float32)
    non_causal = jnp.where(key > query, jnp.inf, 0.0)
    for hh in heads:
        base = slopes[hh] * q_minus_k
        bias_ref[hh, 0] = base
        bias_ref[hh, 1] = base + non_causal

    kmean = kmean_ref[0].astype(jnp.bfloat16)
    blk_id = lax.broadcasted_iota(jnp.int32, (nb, T), 0)
    second_half = lax.broadcasted_iota(jnp.int32, (nb, T), 1) // L

    def select_tile(c):
        qt_heads = query_tile(c)
        own = 2 * c + second_half
        for hh in heads:
            gate = jnp.where(blk_id < own, _dot(kmean, qt_heads[hh]), -jnp.inf)
            sel = blk_id == own
            for _ in range(MOBA_TOPK):
                best = jnp.max(gate, axis=0, keepdims=True)
                first = jnp.min(jnp.where(gate == best, blk_id, nb), axis=0, keepdims=True)
                taken = blk_id == first
                sel = sel | (taken & (blk_id < own))
                gate = jnp.where(taken, -jnp.inf, gate)
            sel_ref[hh, c] = sel.astype(jnp.float32)

    def select(it, _):
        for u in range(SELECT_UNROLL):
            select_tile(SELECT_UNROLL * it + u)
        return 0

    lax.fori_loop(0, n_tiles // SELECT_UNROLL, select, 0)

    blk_rows = [slice(blk * L, (blk + 1) * L) for blk in range(2)]
    ones_rows = jnp.ones((ONES_ROWS, L), jnp.bfloat16)

    def biased_scores(e, slot):
        c, pr = tile_tab_ref[e], pair_tab_ref[e]
        variant = jnp.where(pr == c, 1, 0)
        k_pair = k_ref[0, pl.ds(pl.multiple_of(pr * T, T), T), :]
        col_max = []
        for hh, qt_h in enumerate(query_tile(c)):
            t = _dot(k_pair, qt_h) - bias_ref[hh, variant]
            s_ref[slot, hh] = t
            col_max.append(tuple(jnp.max(t[rows, :], axis=0, keepdims=True) for rows in blk_rows))
        return tuple(col_max)

    def value_product(pr, slot):
        out = []
        for hh in heads:
            lhs = [jnp.concatenate([vt_ref[0, 2 * pr + blk, head_rows[hh], :], ones_rows], axis=0)
                   for blk in range(2)]
            out.append(_dot(lhs[0], p_ref[slot, hh, blk_rows[0], :])
                       + _dot(lhs[1], p_ref[slot, hh, blk_rows[1], :]))
        return out

    def write_tile(c):
        o_t = jnp.concatenate(
            [acc_ref[hh, :ATTN_HEAD_DIM, :] / acc_ref[hh, ATTN_HEAD_DIM:ATTN_HEAD_DIM + 1, :]
             for hh in heads], axis=0)
        o_ref[0, pl.ds(pl.multiple_of(c * T, T), T), :] = o_t.T.astype(o_ref.dtype)

    def item(e, slot, state):
        m, alpha_lag1, alpha_lag2, col_max = state
        other = 1 - slot
        c, pr = tile_tab_ref[e], pair_tab_ref[e]
        e_lag2 = jnp.maximum(e - 2, 0)

        pv_lag2 = value_product(pair_tab_ref[e_lag2], slot)
        col_max_next = biased_scores(jnp.minimum(e + 1, n_items - 1), other)

        opens = pr == c
        m_out, alpha_out = [], []
        for hh in heads:
            m_old = jnp.where(opens, -jnp.inf, m[hh])
            offset = slopes[hh] * ((c - pr) * T).astype(jnp.float32)
            keep = [sel_ref[hh, c, pl.ds(2 * pr + blk, 1), :] > 0.5 for blk in range(2)]
            m_blk = [jnp.where(keep[blk], col_max[hh][blk], -jnp.inf) for blk in range(2)]
            m_new = jnp.maximum(m_old, jnp.maximum(m_blk[0], m_blk[1]) - offset)
            for blk in range(2):
                shift = jnp.where(keep[blk], m_new + offset, jnp.inf)
                p = jnp.exp2(s_ref[slot, hh, blk_rows[blk], :] - shift)
                p_ref[slot, hh, blk_rows[blk], :] = p.astype(jnp.bfloat16)
            m_out.append(m_new)
            alpha_out.append(jnp.exp2(m_old - m_new))

        fold(pv_lag2, alpha_lag2)

        @pl.when((pair_tab_ref[e_lag2] == 0) & (e >= 2))
        def _():
            write_tile(tile_tab_ref[e_lag2])

        return tuple(m_out), tuple(alpha_out), alpha_lag1, col_max_next

    def fold(pv, alpha):
        for hh in heads:
            acc_ref[hh] = alpha[hh] * acc_ref[hh] + pv[hh]

    col_max = biased_scores(0, 0)
    for hh in heads:
        acc_ref[hh] = jnp.zeros(acc_ref.shape[1:], jnp.float32)
    p_ref[...] = jnp.zeros(p_ref.shape, p_ref.dtype)
    row = lambda v: tuple(jnp.full((1, T), v, jnp.float32) for _ in heads)
    state = (row(-jnp.inf), row(0.0), row(0.0), col_max)

    def two_items(it, state):
        state = item(2 * it, 0, state)
        return item(2 * it + 1, 1, state)

    _, alpha_lag1, alpha_lag2, _ = lax.fori_loop(0, n_items // 2, two_items, state)

    for e, alpha in ((n_items - 2, alpha_lag2), (n_items - 1, alpha_lag1)):
        fold(value_product(pair_tab_ref[e], e % 2), alpha)
    write_tile(tile_tab_ref[n_items - 1])


def _moba(slopes, qt, k, kmean, vt):
    b, nb, _, L = qt.shape
    lanes = HEADS_PER_STEP * ATTN_HEAD_DIM
    T = MOBA_TILE
    tile_tab, pair_tab = _moba_items(nb // 2)
    assert len(tile_tab) % 2 == 0
    smem = pl.BlockSpec(memory_space=pltpu.SMEM)
    return pl.pallas_call(
        _moba_kernel,
        grid=(b, N_ATTN_HEADS // HEADS_PER_STEP),
        in_specs=[
            smem, smem, smem,
            pl.BlockSpec((1, nb, lanes, L), lambda bi, hp: (bi, 0, hp, 0)),
            pl.BlockSpec((1, nb * L, lanes), lambda bi, hp: (bi, 0, hp)),
            pl.BlockSpec((1, nb, lanes), lambda bi, hp: (bi, 0, hp)),
            pl.BlockSpec((1, nb, lanes, L), lambda bi, hp: (bi, 0, hp, 0)),
        ],
        out_specs=pl.BlockSpec((1, nb * L, lanes), lambda bi, hp: (bi, 0, hp)),
        out_shape=jax.ShapeDtypeStruct((b, nb * L, D_ATTN), jnp.bfloat16),
        scratch_shapes=[
            pltpu.VMEM((HEADS_PER_STEP, nb // 2, nb, T), jnp.float32),
            pltpu.VMEM((HEADS_PER_STEP, 2, T, T), jnp.float32),
            pltpu.VMEM((2, HEADS_PER_STEP, T, T), jnp.float32),
            pltpu.VMEM((2, HEADS_PER_STEP, T, T), jnp.bfloat16),
            pltpu.VMEM((HEADS_PER_STEP, ATTN_HEAD_DIM + ONES_ROWS, T), jnp.float32),
        ],
        compiler_params=_params("parallel", "parallel"),
        name="moba_attention",
    )(slopes, jnp.asarray(tile_tab), jnp.asarray(pair_tab), qt, k, kmean, vt)


def _memkv_kernel(mem_ref, g_ref, wkv_ref, k_ref, v_ref):
    d = mem_ref.shape[2]
    mem_n = _rmsnorm(mem_ref[0], g_ref[...]).astype(jnp.bfloat16)
    kv = _dot(mem_n, wkv_ref[...])
    k_ref[0] = kv[:, :d].astype(jnp.bfloat16)
    v_ref[0] = kv[:, d:].astype(jnp.bfloat16)


def _memkv(mem, g, wkv):
    b, m, d = mem.shape
    blk = pl.BlockSpec((1, m, d), lambda bi: (bi, 0, 0))
    return pl.pallas_call(
        _memkv_kernel,
        grid=(b,),
        in_specs=[blk, _const_spec((1, d)), _const_spec(wkv.shape)],
        out_specs=[blk, blk],
        out_shape=[jax.ShapeDtypeStruct((b, m, d), jnp.bfloat16)] * 2,
        compiler_params=_params("parallel"),
        name="mem_kv",
    )(mem, g, wkv)


def _pooling_mixer(ext, p, first_pos, pool_w_ref, pool_scale_ref):
    rows = p.shape[0]
    pos = first_pos + lax.broadcasted_iota(jnp.int32, (rows, 1), 0)
    pooled = []
    for g, w in enumerate(POOL_WINDOWS):
        cols = slice(g * POOL_GROUP_DIM, (g + 1) * POOL_GROUP_DIM)
        acc = ext[:, cols]
        n_valid = acc.shape[0]
        span = 1
        while span < w:
            acc = acc[span:, :] + acc[:n_valid - span, :]
            n_valid -= span
            span *= 2
        win_sum = acc[n_valid - rows:, :]
        cnt = jnp.minimum(pos + 1, w).astype(jnp.float32)
        d = win_sum / cnt - p[:, cols]
        y = _dot(d.astype(jnp.bfloat16), pool_w_ref[g])
        pooled.append(y * pool_scale_ref[:, cols])
    return jnp.concatenate(pooled, axis=-1).astype(jnp.bfloat16)


def _mix_xattn_kernel(x_ref, attn_ref, p_ref, halo_ref, pool_w_ref, pool_scale_ref,
                      w_out_ref, mix_g_ref, pre_g_ref, wq_ref, k_ref, v_ref, wo_ref,
                      post_g_ref, o_ref):
    t = pl.program_id(1)
    tm, d = x_ref.shape[1], x_ref.shape[2]
    dh = d // XA_HEADS
    subs = [slice(i * MIX_SUB_TILE, (i + 1) * MIX_SUB_TILE) for i in range(tm // MIX_SUB_TILE)]
    halo = jnp.where(t > 0, halo_ref[0], 0.0)
    ext = jnp.concatenate([halo, p_ref[0]], axis=0)
    pools = [_pooling_mixer(ext[rows.start:rows.stop + POOL_HALO, :], p_ref[0, rows, :],
                            t * tm + rows.start, pool_w_ref, pool_scale_ref) for rows in subs]
    ys = [_dot(jnp.concatenate([attn_ref[0, rows, :], pool], axis=-1), w_out_ref[...])
          for rows, pool in zip(subs, pools)]
    xs = [x_ref[0, rows, :] + _rmsnorm(y, mix_g_ref[...]) for rows, y in zip(subs, ys)]

    hs = [_rmsnorm(x, pre_g_ref[...]).astype(jnp.bfloat16) for x in xs]
    qs = [_dot(h, wq_ref[...]).astype(jnp.bfloat16) for h in hs]
    heads, cs = [[] for _ in subs], []
    head_cols = [slice(hd * dh, (hd + 1) * dh) for hd in range(XA_HEADS)]
    ss = [[lax.dot_general(q[:, cols], k_ref[0, :, cols], _NT,
                           preferred_element_type=jnp.float32) * (dh ** -0.5) for q in qs]
          for cols in head_cols]
    for i in range(len(subs)):
        for hd, cols in enumerate(head_cols):
            s = ss[hd][i]
            e = jnp.exp(s - jnp.max(s, axis=-1, keepdims=True))
            p = e / jnp.sum(e, axis=-1, keepdims=True)
            heads[i].append(_dot(p.astype(jnp.bfloat16), v_ref[0, :, cols]))
        cs.append(_dot(jnp.concatenate(heads[i], axis=-1).astype(jnp.bfloat16), wo_ref[...]))
    for rows, x, c in zip(subs, xs, cs):
        o_ref[0, rows, :] = x + _rmsnorm(c, post_g_ref[...])


def _mix_xattn(x, attn, p_in, pool_w, pool_scale, w_out, mix_g, pre_g, wq, k_mem, v_mem, wo,
               post_g):
    b, s, d = x.shape
    m = k_mem.shape[1]
    halo_blocks = MIX_TOKEN_TILE // POOL_HALO
    tile = lambda width: pl.BlockSpec((1, MIX_TOKEN_TILE, width), lambda bi, t: (bi, t, 0))
    mem_blk = pl.BlockSpec((1, m, d), lambda bi, t: (bi, 0, 0))
    return pl.pallas_call(
        _mix_xattn_kernel,
        grid=(b, s // MIX_TOKEN_TILE),
        in_specs=[
            tile(d), tile(D_ATTN), tile(D_POOL),
            pl.BlockSpec((1, POOL_HALO, D_POOL),
                         lambda bi, t: (bi, jnp.maximum(t * halo_blocks - 1, 0), 0)),
            _const_spec(pool_w.shape), _const_spec((1, D_POOL)),
            _const_spec(w_out.shape), _const_spec((1, d)),
            _const_spec((1, d)), _const_spec(wq.shape), mem_blk, mem_blk,
            _const_spec(wo.shape), _const_spec((1, d)),
        ],
        out_specs=tile(d),
        out_shape=jax.ShapeDtypeStruct((b, s, d), jnp.float32),
        compiler_params=_params("parallel", "parallel"),
        name="mixer_out_mem_xattn",
    )(x, attn, p_in, p_in, pool_w, pool_scale, w_out, mix_g, pre_g, wq, k_mem, v_mem, wo, post_g)


def _alibi_slopes(n_heads):
    return jnp.asarray(2.0 ** (-8.0 * np.arange(1, n_heads + 1) / n_heads), jnp.float32)


def kernel(x, mem, ffn1_pre_g, ffn1_w_gate, ffn1_w_up, ffn1_w_down, ffn1_post_g, mix_pre_g, w_in, pool_w, pool_scale, w_out, mix_post_g, xa_pre_g, mem_g, xa_wq, xa_wkv, xa_wo, xa_post_g, ffn2_pre_g, ffn2_w_gate, ffn2_w_up, ffn2_w_down, ffn2_post_g):
    b, s, d = x.shape
    depth = ffn1_pre_g.shape[0]
    bf = lambda w: w.astype(jnp.bfloat16)
    row = lambda g: g.reshape(1, -1)
    slopes = _alibi_slopes(N_ATTN_HEADS) * LOG2_E
    for l in range(depth):
        x = _ffn(x.reshape(b * s, d), row(ffn1_pre_g[l]), ffn1_w_gate[l], ffn1_w_up[l],
                 ffn1_w_down[l], row(ffn1_post_g[l])).reshape(b, s, d)

        w = bf(w_in[l])
        qt, k, kmean, vt, p_in = _proj(
            x, row(mix_pre_g[l]), w[:, :D_ATTN].T, w[:, D_ATTN:2 * D_ATTN],
            w[:, 2 * D_ATTN:3 * D_ATTN].T, w[:, 3 * D_ATTN:])
        attn = _moba(slopes, qt, k.reshape(b, s, D_ATTN), kmean.reshape(b, -1, D_ATTN), vt)
        k_mem, v_mem = _memkv(mem, row(mem_g[l]), bf(xa_wkv[l]))
        x = _mix_xattn(x, attn, p_in, bf(pool_w[l]), row(pool_scale[l]), bf(w_out[l]),
                       row(mix_post_g[l]), row(xa_pre_g[l]), bf(xa_wq[l]), k_mem, v_mem,
                       bf(xa_wo[l]), row(xa_post_g[l]))

        x = _ffn(x.reshape(b * s, d), row(ffn2_pre_g[l]), ffn2_w_gate[l], ffn2_w_up[l],
                 ffn2_w_down[l], row(ffn2_post_g[l])).reshape(b, s, d)
    return x
```

```python
import functools

import numpy as np
import jax
import jax.numpy as jnp
from jax import lax
from jax.experimental import pallas as pl
from jax.experimental.pallas import tpu as pltpu

N_ATTN_HEADS = 8
ATTN_HEAD_DIM = 64
D_ATTN = N_ATTN_HEADS * ATTN_HEAD_DIM
POOL_WINDOWS = (2, 4, 8, 16)
POOL_GROUP_DIM = 128
D_POOL = len(POOL_WINDOWS) * POOL_GROUP_DIM
MOBA_BLOCK = 256
MOBA_TOPK = 3
XA_HEADS = 4
MACARON_WEIGHT = 0.5
RMS_EPS = 1e-6
LOG2_E = 1.4426950408889634

V7X_LANES = 128
V7X_VMEM_BYTES = 64 * 1024 * 1024
VMEM_LIMIT_BYTES = V7X_VMEM_BYTES * 7 // 8

TOKEN_TILE = 512
FFN_SUB_TILE = 256
WEIGHT_CHUNK_ROWS = 128
STAGE_SLOTS = 6
MIX_TOKEN_TILE = 1024
MIX_SUB_TILE = 256
HEADS_PER_STEP = V7X_LANES // ATTN_HEAD_DIM
POOL_HALO = max(POOL_WINDOWS)

_NT = (((1,), (1,)), ((), ()))


def _rmsnorm(x, g):
    r = lax.rsqrt(jnp.mean(x * x, axis=-1, keepdims=True) + RMS_EPS)
    return (x * r) * g


def _dot(a, b):
    return jnp.dot(a, b, preferred_element_type=jnp.float32)


def _params(*semantics):
    return pltpu.CompilerParams(dimension_semantics=semantics,
                                vmem_limit_bytes=VMEM_LIMIT_BYTES)


def _const_spec(shape):
    return pl.BlockSpec(shape, lambda *_: (0,) * len(shape))


def _stage_cast(pairs, stage_ref, sem):
    chunks = [(src, dst, r) for src, dst in pairs
              for r in range(0, src.shape[0], WEIGHT_CHUNK_ROWS)]

    def chunk_copy(c):
        src, _, r = chunks[c]
        slot = c % STAGE_SLOTS
        return pltpu.make_async_copy(
            src.at[pl.ds(r, WEIGHT_CHUNK_ROWS), :],
            stage_ref.at[slot, :, pl.ds(0, src.shape[1])], sem.at[slot])

    ahead = STAGE_SLOTS - 1
    for c in range(min(ahead, len(chunks))):
        chunk_copy(c).start()
    for c, (src, dst, r) in enumerate(chunks):
        if c + ahead < len(chunks):
            chunk_copy(c + ahead).start()
        chunk_copy(c).wait()
        dst[pl.ds(r, WEIGHT_CHUNK_ROWS), :] = (
            stage_ref[c % STAGE_SLOTS, :, pl.ds(0, src.shape[1])].astype(jnp.bfloat16))


def _ffn_kernel(*refs, with_proj):
    n_in, n_out = (11, 6) if with_proj else (6, 1)
    x_ref, pre_g_ref, wg_hbm, wu_hbm, wd_hbm, post_g_ref = refs[:6]
    o_ref = refs[n_in]
    wg_ref, wu_ref, wd_ref, stage_ref, sem = refs[n_in + n_out:]

    @pl.when(pl.program_id(0) == 0)
    def _():
        _stage_cast(((wg_hbm, wg_ref), (wu_hbm, wu_ref), (wd_hbm, wd_ref)), stage_ref, sem)

    subs = [slice(i * FFN_SUB_TILE, (i + 1) * FFN_SUB_TILE)
            for i in range(x_ref.shape[0] // FFN_SUB_TILE)]
    xs = [x_ref[rows, :] for rows in subs]
    hs = [_rmsnorm(x, pre_g_ref[...]).astype(jnp.bfloat16) for x in xs]
    gates = [_dot(h, wg_ref[...]) for h in hs]
    ups = [_dot(h, wu_ref[...]) for h in hs]
    acts = [(g * jax.nn.sigmoid(g) * u).astype(jnp.bfloat16) for g, u in zip(gates, ups)]
    fs = [_dot(a, wd_ref[...]) for a in acts]
    outs = [x + MACARON_WEIGHT * _rmsnorm(f, post_g_ref[...]) for x, f in zip(xs, fs)]
    for rows, out in zip(subs, outs):
        o_ref[rows, :] = out
    if with_proj:
        _mixer_projection(outs, subs, *refs[6:n_in], *refs[n_in + 1:n_in + n_out])


def _mixer_projection(xs, subs, g_ref, wqt_ref, wk_ref, wvt_ref, wp_ref,
                      qt_ref, k_ref, kmean_ref, vt_ref, p_ref):
    hs = [_rmsnorm(x, g_ref[...]).astype(jnp.bfloat16) for x in xs]
    for blk, (rows, h) in enumerate(zip(subs, hs)):
        qt = lax.dot_general(wqt_ref[...], h, _NT, preferred_element_type=jnp.float32)
        k = _dot(h, wk_ref[...])
        vt = lax.dot_general(wvt_ref[...], h, _NT, preferred_element_type=jnp.float32)
        p_ref[rows, :] = _dot(h, wp_ref[...])
        qt_ref[blk] = (qt * (ATTN_HEAD_DIM ** -0.5 * LOG2_E)).astype(jnp.bfloat16)
        vt_ref[blk] = vt.astype(jnp.bfloat16)
        k_ref[blk] = k.astype(jnp.bfloat16)
        kmean_ref[blk] = jnp.sum(k, axis=0, keepdims=True) * (1.0 / MOBA_BLOCK)


def _ffn(x, pre_g, wg, wu, wd, post_g, proj=None):
    n, d = x.shape
    d_ff = wg.shape[1]
    assert d % WEIGHT_CHUNK_ROWS == 0 and d_ff % WEIGHT_CHUNK_ROWS == 0
    assert FFN_SUB_TILE == MOBA_BLOCK
    tile = pl.BlockSpec((TOKEN_TILE, d), lambda i: (i, 0))
    hbm = pl.BlockSpec(memory_space=pl.ANY)
    in_specs = [tile, _const_spec((1, d)), hbm, hbm, hbm, _const_spec((1, d))]
    out_specs = [tile]
    out_shape = [jax.ShapeDtypeStruct((n, d), jnp.float32)]
    operands = [x, pre_g, wg, wu, wd, post_g]
    if proj is not None:
        bpt = TOKEN_TILE // MOBA_BLOCK
        nblk = n // MOBA_BLOCK
        in_specs += [_const_spec(w.shape) for w in proj]
        operands += list(proj)
        blocked = lambda r, c: pl.BlockSpec((bpt, r, c), lambda i: (i, 0, 0))
        out_specs += [blocked(D_ATTN, MOBA_BLOCK), blocked(MOBA_BLOCK, D_ATTN), blocked(1, D_ATTN),
                      blocked(D_ATTN, MOBA_BLOCK), pl.BlockSpec((TOKEN_TILE, D_POOL), lambda i: (i, 0))]
        out_shape += [jax.ShapeDtypeStruct((nblk, D_ATTN, MOBA_BLOCK), jnp.bfloat16),
                      jax.ShapeDtypeStruct((nblk, MOBA_BLOCK, D_ATTN), jnp.bfloat16),
                      jax.ShapeDtypeStruct((nblk, 1, D_ATTN), jnp.float32),
                      jax.ShapeDtypeStruct((nblk, D_ATTN, MOBA_BLOCK), jnp.bfloat16),
                      jax.ShapeDtypeStruct((n, D_POOL), jnp.float32)]
    return pl.pallas_call(
        functools.partial(_ffn_kernel, with_proj=proj is not None),
        grid=(n // TOKEN_TILE,),
        in_specs=in_specs,
        out_specs=out_specs,
        out_shape=out_shape,
        scratch_shapes=[
            pltpu.VMEM((d, d_ff), jnp.bfloat16), pltpu.VMEM((d, d_ff), jnp.bfloat16),
            pltpu.VMEM((d_ff, d), jnp.bfloat16),
            pltpu.VMEM((STAGE_SLOTS, WEIGHT_CHUNK_ROWS, max(d, d_ff)), jnp.float32),
            pltpu.SemaphoreType.DMA((STAGE_SLOTS,)),
        ],
        compiler_params=_params("arbitrary"),
        name="ffn_mixer_proj" if proj is not None else "ffn",
    )(*operands)


MOBA_TILE = 2 * MOBA_BLOCK
SELECT_UNROLL = 4
ONES_ROWS = 16


def _moba_items(n_tiles):
    tiles, pairs = [], []
    for c in range(n_tiles):
        for pr in range(c, -1, -1):
            tiles.append(c)
            pairs.append(pr)
    return np.asarray(tiles, np.int32), np.asarray(pairs, np.int32)


def _moba_kernel(slopes_ref, tile_tab_ref, pair_tab_ref, qt_ref, k_ref, kmean_ref, vt_ref,
                 o_ref, sel_ref, bias_ref, s_ref, p_ref, acc_ref):
    hp = pl.program_id(1)
    L, T = MOBA_BLOCK, MOBA_TILE
    nb = vt_ref.shape[1]
    n_tiles = nb // 2
    n_items = tile_tab_ref.shape[0]
    heads = range(HEADS_PER_STEP)
    slopes = [slopes_ref[hp * HEADS_PER_STEP + hh] for hh in heads]
    head_rows = [slice(hh * ATTN_HEAD_DIM, (hh + 1) * ATTN_HEAD_DIM) for hh in heads]
    head_row = lax.broadcasted_iota(jnp.int32, (HEADS_PER_STEP * ATTN_HEAD_DIM, T), 0) // ATTN_HEAD_DIM

    def query_tile(c):
        qt = jnp.concatenate([qt_ref[0, 2 * c], qt_ref[0, 2 * c + 1]], axis=1)
        return [jnp.where(head_row == hh, qt, jnp.zeros_like(qt)) for hh in heads]

    key = lax.broadcasted_iota(jnp.int32, (T, T), 0)
    query = lax.broadcasted_iota(jnp.int32, (T, T), 1)
    q_minus_k = (query - key).astype(jnp.float32)
    non_causal = jnp.where(key > query, jnp.inf, 0.0)
    for hh in heads:
        base = slopes[hh] * q_minus_k
        bias_ref[hh, 0] = base
        bias_ref[hh, 1] = base + non_causal

    kmean = kmean_ref[0].astype(jnp.bfloat16)
    blk_id = lax.broadcasted_iota(jnp.int32, (nb, T), 0)
    second_half = lax.broadcasted_iota(jnp.int32, (nb, T), 1) // L

    def select_tile(c):
        qt_heads = query_tile(c)
        own = 2 * c + second_half
        for hh in heads:
            gate = jnp.where(blk_id < own, _dot(kmean, qt_heads[hh]), -jnp.inf)
            sel = blk_id == own
            for _ in range(MOBA_TOPK):
                best = jnp.max(gate, axis=0, keepdims=True)
                first = jnp.min(jnp.where(gate == best, blk_id, nb), axis=0, keepdims=True)
                taken = blk_id == first
                sel = sel | (taken & (blk_id < own))
                gate = jnp.where(taken, -jnp.inf, gate)
            sel_ref[hh, c] = sel.astype(jnp.float32)

    def select(it, _):
        for u in range(SELECT_UNROLL):
            select_tile(SELECT_UNROLL * it + u)
        return 0

    lax.fori_loop(0, n_tiles // SELECT_UNROLL, select, 0)

    blk_rows = [slice(blk * L, (blk + 1) * L) for blk in range(2)]
    ones_rows = jnp.ones((ONES_ROWS, L), jnp.bfloat16)

    def biased_scores(e, slot):
        c, pr = tile_tab_ref[e], pair_tab_ref[e]
        variant = jnp.where(pr == c, 1, 0)
        k_pair = k_ref[0, pl.ds(pl.multiple_of(pr * T, T), T), :]
        col_max = []
        for hh, qt_h in enumerate(query_tile(c)):
            t = _dot(k_pair, qt_h) - bias_ref[hh, variant]
            s_ref[slot, hh] = t
            col_max.append(tuple(jnp.max(t[rows, :], axis=0, keepdims=True) for rows in blk_rows))
        return tuple(col_max)

    def value_product(pr, slot):
        out = []
        for hh in heads:
            lhs = [jnp.concatenate([vt_ref[0, 2 * pr + blk, head_rows[hh], :], ones_rows], axis=0)
                   for blk in range(2)]
            out.append(_dot(lhs[0], p_ref[slot, hh, blk_rows[0], :])
                       + _dot(lhs[1], p_ref[slot, hh, blk_rows[1], :]))
        return out

    def write_tile(c):
        o_t = jnp.concatenate(
            [acc_ref[hh, :ATTN_HEAD_DIM, :] / acc_ref[hh, ATTN_HEAD_DIM:ATTN_HEAD_DIM + 1, :]
             for hh in heads], axis=0)
        o_ref[0, pl.ds(pl.multiple_of(c * T, T), T), :] = o_t.T.astype(o_ref.dtype)

    def item(e, slot, state):
        m, alpha_lag1, alpha_lag2, col_max = state
        other = 1 - slot
        c, pr = tile_tab_ref[e], pair_tab_ref[e]
        e_lag2 = jnp.maximum(e - 2, 0)

        pv_lag2 = value_product(pair_tab_ref[e_lag2], slot)
        col_max_next = biased_scores(jnp.minimum(e + 1, n_items - 1), other)

        opens = pr == c
        m_out, alpha_out = [], []
        for hh in heads:
            m_old = jnp.where(opens, -jnp.inf, m[hh])
            offset = slopes[hh] * ((c - pr) * T).astype(jnp.float32)
            keep = [sel_ref[hh, c, pl.ds(2 * pr + blk, 1), :] > 0.5 for blk in range(2)]
            m_blk = [jnp.where(keep[blk], col_max[hh][blk], -jnp.inf) for blk in range(2)]
            m_new = jnp.maximum(m_old, jnp.maximum(m_blk[0], m_blk[1]) - offset)
            for blk in range(2):
                shift = jnp.where(keep[blk], m_new + offset, jnp.inf)
                p = jnp.exp2(s_ref[slot, hh, blk_rows[blk], :] - shift)
                p_ref[slot, hh, blk_rows[blk], :] = p.astype(jnp.bfloat16)
            m_out.append(m_new)
            alpha_out.append(jnp.exp2(m_old - m_new))

        fold(pv_lag2, alpha_lag2)

        @pl.when((pair_tab_ref[e_lag2] == 0) & (e >= 2))
        def _():
            write_tile(tile_tab_ref[e_lag2])

        return tuple(m_out), tuple(alpha_out), alpha_lag1, col_max_next

    def fold(pv, alpha):
        for hh in heads:
            acc_ref[hh] = alpha[hh] * acc_ref[hh] + pv[hh]

    col_max = biased_scores(0, 0)
    for hh in heads:
        acc_ref[hh] = jnp.zeros(acc_ref.shape[1:], jnp.float32)
    p_ref[...] = jnp.zeros(p_ref.shape, p_ref.dtype)
    row = lambda v: tuple(jnp.full((1, T), v, jnp.float32) for _ in heads)
    state = (row(-jnp.inf), row(0.0), row(0.0), col_max)

    def two_items(it, state):
        state = item(2 * it, 0, state)
        return item(2 * it + 1, 1, state)

    _, alpha_lag1, alpha_lag2, _ = lax.fori_loop(0, n_items // 2, two_items, state)

    for e, alpha in ((n_items - 2, alpha_lag2), (n_items - 1, alpha_lag1)):
        fold(value_product(pair_tab_ref[e], e % 2), alpha)
    write_tile(tile_tab_ref[n_items - 1])


def _moba(slopes, qt, k, kmean, vt):
    b, nb, _, L = qt.shape
    lanes = HEADS_PER_STEP * ATTN_HEAD_DIM
    T = MOBA_TILE
    tile_tab, pair_tab = _moba_items(nb // 2)
    assert len(tile_tab) % 2 == 0
    smem = pl.BlockSpec(memory_space=pltpu.SMEM)
    return pl.pallas_call(
        _moba_kernel,
        grid=(b, N_ATTN_HEADS // HEADS_PER_STEP),
        in_specs=[
            smem, smem, smem,
            pl.BlockSpec((1, nb, lanes, L), lambda bi, hp: (bi, 0, hp, 0)),
            pl.BlockSpec((1, nb * L, lanes), lambda bi, hp: (bi, 0, hp)),
            pl.BlockSpec((1, nb, lanes), lambda bi, hp: (bi, 0, hp)),
            pl.BlockSpec((1, nb, lanes, L), lambda bi, hp: (bi, 0, hp, 0)),
        ],
        out_specs=pl.BlockSpec((1, nb * L, lanes), lambda bi, hp: (bi, 0, hp)),
        out_shape=jax.ShapeDtypeStruct((b, nb * L, D_ATTN), jnp.bfloat16),
        scratch_shapes=[
            pltpu.VMEM((HEADS_PER_STEP, nb // 2, nb, T), jnp.float32),
            pltpu.VMEM((HEADS_PER_STEP, 2, T, T), jnp.float32),
            pltpu.VMEM((2, HEADS_PER_STEP, T, T), jnp.float32),
            pltpu.VMEM((2, HEADS_PER_STEP, T, T), jnp.bfloat16),
            pltpu.VMEM((HEADS_PER_STEP, ATTN_HEAD_DIM + ONES_ROWS, T), jnp.float32),
        ],
        compiler_params=_params("parallel", "parallel"),
        name="moba_attention",
    )(slopes, jnp.asarray(tile_tab), jnp.asarray(pair_tab), qt, k, kmean, vt)


def _memkv_kernel(mem_ref, g_ref, wkv_ref, k_ref, v_ref):
    d = mem_ref.shape[2]
    mem_n = _rmsnorm(mem_ref[0], g_ref[...]).astype(jnp.bfloat16)
    kv = _dot(mem_n, wkv_ref[...])
    k_ref[0] = kv[:, :d].astype(jnp.bfloat16)
    v_ref[0] = kv[:, d:].astype(jnp.bfloat16)


def _memkv(mem, g, wkv):
    b, m, d = mem.shape
    blk = pl.BlockSpec((1, m, d), lambda bi: (bi, 0, 0))
    return pl.pallas_call(
        _memkv_kernel,
        grid=(b,),
        in_specs=[blk, _const_spec((1, d)), _const_spec(wkv.shape)],
        out_specs=[blk, blk],
        out_shape=[jax.ShapeDtypeStruct((b, m, d), jnp.bfloat16)] * 2,
        compiler_params=_params("parallel"),
        name="mem_kv",
    )(mem, g, wkv)


def _pooling_mixer(ext, p, first_pos, pool_w_ref, pool_scale_ref):
    rows = p.shape[0]
    pos = first_pos + lax.broadcasted_iota(jnp.int32, (rows, 1), 0)
    pooled = []
    for g, w in enumerate(POOL_WINDOWS):
        cols = slice(g * POOL_GROUP_DIM, (g + 1) * POOL_GROUP_DIM)
        acc = ext[:, cols]
        n_valid = acc.shape[0]
        span = 1
        while span < w:
            acc = acc[span:, :] + acc[:n_valid - span, :]
            n_valid -= span
            span *= 2
        win_sum = acc[n_valid - rows:, :]
        cnt = jnp.minimum(pos + 1, w).astype(jnp.float32)
        d = win_sum / cnt - p[:, cols]
        y = _dot(d.astype(jnp.bfloat16), pool_w_ref[g])
        pooled.append(y * pool_scale_ref[:, cols])
    return jnp.concatenate(pooled, axis=-1).astype(jnp.bfloat16)


def _mix_xattn_kernel(x_ref, attn_ref, p_ref, halo_ref, pool_w_ref, pool_scale_ref,
                      w_out_ref, mix_g_ref, pre_g_ref, wq_ref, k_ref, v_ref, wo_ref,
                      post_g_ref, o_ref):
    t = pl.program_id(1)
    tm, d = x_ref.shape[1], x_ref.shape[2]
    dh = d // XA_HEADS
    subs = [slice(i * MIX_SUB_TILE, (i + 1) * MIX_SUB_TILE) for i in range(tm // MIX_SUB_TILE)]
    halo = jnp.where(t > 0, halo_ref[0], 0.0)
    ext = jnp.concatenate([halo, p_ref[0]], axis=0)
    pools = [_pooling_mixer(ext[rows.start:rows.stop + POOL_HALO, :], p_ref[0, rows, :],
                            t * tm + rows.start, pool_w_ref, pool_scale_ref) for rows in subs]
    ys = [_dot(jnp.concatenate([attn_ref[0, rows, :], pool], axis=-1), w_out_ref[...])
          for rows, pool in zip(subs, pools)]
    xs = [x_ref[0, rows, :] + _rmsnorm(y, mix_g_ref[...]) for rows, y in zip(subs, ys)]

    hs = [_rmsnorm(x, pre_g_ref[...]).astype(jnp.bfloat16) for x in xs]
    qs = [_dot(h, wq_ref[...]).astype(jnp.bfloat16) for h in hs]
    heads, cs = [[] for _ in subs], []
    head_cols = [slice(hd * dh, (hd + 1) * dh) for hd in range(XA_HEADS)]
    ss = [[lax.dot_general(q[:, cols], k_ref[0, :, cols], _NT,
                           preferred_element_type=jnp.float32) * (dh ** -0.5) for q in qs]
          for cols in head_cols]
    for i in range(len(subs)):
        for hd, cols in enumerate(head_cols):
            s = ss[hd][i]
            e = jnp.exp(s - jnp.max(s, axis=-1, keepdims=True))
            p = e / jnp.sum(e, axis=-1, keepdims=True)
            heads[i].append(_dot(p.astype(jnp.bfloat16), v_ref[0, :, cols]))
        cs.append(_dot(jnp.concatenate(heads[i], axis=-1).astype(jnp.bfloat16), wo_ref[...]))
    for rows, x, c in zip(subs, xs, cs):
        o_ref[0, rows, :] = x + _rmsnorm(c, post_g_ref[...])


def _mix_xattn(x, attn, p_in, pool_w, pool_scale, w_out, mix_g, pre_g, wq, k_mem, v_mem, wo,
               post_g):
    b, s, d = x.shape
    m = k_mem.shape[1]
    halo_blocks = MIX_TOKEN_TILE // POOL_HALO
    tile = lambda width: pl.BlockSpec((1, MIX_TOKEN_TILE, width), lambda bi, t: (bi, t, 0))
    mem_blk = pl.BlockSpec((1, m, d), lambda bi, t: (bi, 0, 0))
    return pl.pallas_call(
        _mix_xattn_kernel,
        grid=(b, s // MIX_TOKEN_TILE),
        in_specs=[
            tile(d), tile(D_ATTN), tile(D_POOL),
            pl.BlockSpec((1, POOL_HALO, D_POOL),
                         lambda bi, t: (bi, jnp.maximum(t * halo_blocks - 1, 0), 0)),
            _const_spec(pool_w.shape), _const_spec((1, D_POOL)),
            _const_spec(w_out.shape), _const_spec((1, d)),
            _const_spec((1, d)), _const_spec(wq.shape), mem_blk, mem_blk,
            _const_spec(wo.shape), _const_spec((1, d)),
        ],
        out_specs=tile(d),
        out_shape=jax.ShapeDtypeStruct((b, s, d), jnp.float32),
        compiler_params=_params("parallel", "parallel"),
        name="mixer_out_mem_xattn",
    )(x, attn, p_in, p_in, pool_w, pool_scale, w_out, mix_g, pre_g, wq, k_mem, v_mem, wo, post_g)


def _alibi_slopes(n_heads):
    return jnp.asarray(2.0 ** (-8.0 * np.arange(1, n_heads + 1) / n_heads), jnp.float32)


def kernel(x, mem, ffn1_pre_g, ffn1_w_gate, ffn1_w_up, ffn1_w_down, ffn1_post_g, mix_pre_g, w_in, pool_w, pool_scale, w_out, mix_post_g, xa_pre_g, mem_g, xa_wq, xa_wkv, xa_wo, xa_post_g, ffn2_pre_g, ffn2_w_gate, ffn2_w_up, ffn2_w_down, ffn2_post_g):
    b, s, d = x.shape
    depth = ffn1_pre_g.shape[0]
    bf = lambda w: w.astype(jnp.bfloat16)
    row = lambda g: g.reshape(1, -1)
    slopes = _alibi_slopes(N_ATTN_HEADS) * LOG2_E
    for l in range(depth):
        w = bf(w_in[l])
        x, qt, k, kmean, vt, p_in = _ffn(
            x.reshape(b * s, d), row(ffn1_pre_g[l]), ffn1_w_gate[l], ffn1_w_up[l], ffn1_w_down[l],
            row(ffn1_post_g[l]),
            proj=(row(mix_pre_g[l]), w[:, :D_ATTN].T, w[:, D_ATTN:2 * D_ATTN],
                  w[:, 2 * D_ATTN:3 * D_ATTN].T, w[:, 3 * D_ATTN:]))
        nb = s // MOBA_BLOCK
        x, p_in = x.reshape(b, s, d), p_in.reshape(b, s, D_POOL)
        qt, vt = (t.reshape(b, nb, D_ATTN, MOBA_BLOCK) for t in (qt, vt))
        attn = _moba(slopes, qt, k.reshape(b, s, D_ATTN), kmean.reshape(b, -1, D_ATTN), vt)
        k_mem, v_mem = _memkv(mem, row(mem_g[l]), bf(xa_wkv[l]))
        x = _mix_xattn(x, attn, p_in, bf(pool_w[l]), row(pool_scale[l]), bf(w_out[l]),
                       row(mix_post_g[l]), row(xa_pre_g[l]), bf(xa_wq[l]), k_mem, v_mem,
                       bf(xa_wo[l]), row(xa_post_g[l]))

        x = _ffn(x.reshape(b * s, d), row(ffn2_pre_g[l]), ffn2_w_gate[l], ffn2_w_up[l],
                 ffn2_w_down[l], row(ffn2_post_g[l]))[0].reshape(b, s, d)
    return x
```

```python
import functools

import numpy as np
import jax
import jax.numpy as jnp
from jax import lax
from jax.experimental import pallas as pl
from jax.experimental.pallas import tpu as pltpu

N_ATTN_HEADS = 8
ATTN_HEAD_DIM = 64
D_ATTN = N_ATTN_HEADS * ATTN_HEAD_DIM
POOL_WINDOWS = (2, 4, 8, 16)
POOL_GROUP_DIM = 128
D_POOL = len(POOL_WINDOWS) * POOL_GROUP_DIM
MOBA_BLOCK = 256
MOBA_TOPK = 3
XA_HEADS = 4
MACARON_WEIGHT = 0.5
RMS_EPS = 1e-6
LOG2_E = 1.4426950408889634

V7X_LANES = 128
V7X_VMEM_BYTES = 64 * 1024 * 1024
VMEM_LIMIT_BYTES = V7X_VMEM_BYTES * 7 // 8

TOKEN_TILE = 512
FFN_SUB_TILE = 256
WEIGHT_CHUNK_ROWS = 128
STAGE_SLOTS = 6
MIX_TOKEN_TILE = 1024
MIX_SUB_TILE = 256
HEADS_PER_STEP = V7X_LANES // ATTN_HEAD_DIM
POOL_HALO = max(POOL_WINDOWS)

_NT = (((1,), (1,)), ((), ()))


def _rmsnorm(x, g):
    r = lax.rsqrt(jnp.mean(x * x, axis=-1, keepdims=True) + RMS_EPS)
    return (x * r) * g


def _dot(a, b):
    return jnp.dot(a, b, preferred_element_type=jnp.float32)


def _params(*semantics):
    return pltpu.CompilerParams(dimension_semantics=semantics,
                                vmem_limit_bytes=VMEM_LIMIT_BYTES)


def _const_spec(shape):
    return pl.BlockSpec(shape, lambda *_: (0,) * len(shape))


def _stage_cast(pairs, stage_ref, sem):
    chunks = [(src, dst, r) for src, dst in pairs
              for r in range(0, src.shape[0], WEIGHT_CHUNK_ROWS)]

    def chunk_copy(c):
        src, _, r = chunks[c]
        slot = c % STAGE_SLOTS
        return pltpu.make_async_copy(
            src.at[pl.ds(r, WEIGHT_CHUNK_ROWS), :],
            stage_ref.at[slot, :, pl.ds(0, src.shape[1])], sem.at[slot])

    ahead = STAGE_SLOTS - 1
    for c in range(min(ahead, len(chunks))):
        chunk_copy(c).start()
    for c, (src, dst, r) in enumerate(chunks):
        if c + ahead < len(chunks):
            chunk_copy(c + ahead).start()
        chunk_copy(c).wait()
        dst[pl.ds(r, WEIGHT_CHUNK_ROWS), :] = (
            stage_ref[c % STAGE_SLOTS, :, pl.ds(0, src.shape[1])].astype(jnp.bfloat16))


def _ffn_kernel(*refs, with_proj):
    n_in, n_out = (11, 6) if with_proj else (6, 1)
    x_ref, pre_g_ref, wg_hbm, wu_hbm, wd_hbm, post_g_ref = refs[:6]
    o_ref = refs[n_in]
    wg_ref, wu_ref, wd_ref, stage_ref, sem = refs[n_in + n_out:]

    @pl.when(pl.program_id(0) == 0)
    def _():
        _stage_cast(((wg_hbm, wg_ref), (wu_hbm, wu_ref), (wd_hbm, wd_ref)), stage_ref, sem)

    subs = [slice(i * FFN_SUB_TILE, (i + 1) * FFN_SUB_TILE)
            for i in range(x_ref.shape[0] // FFN_SUB_TILE)]
    xs = [x_ref[rows, :] for rows in subs]
    hs = [_rmsnorm(x, pre_g_ref[...]).astype(jnp.bfloat16) for x in xs]
    gates = [_dot(h, wg_ref[...]) for h in hs]
    ups = [_dot(h, wu_ref[...]) for h in hs]
    acts = [(g * jax.nn.sigmoid(g) * u).astype(jnp.bfloat16) for g, u in zip(gates, ups)]
    fs = [_dot(a, wd_ref[...]) for a in acts]
    outs = [x + MACARON_WEIGHT * _rmsnorm(f, post_g_ref[...]) for x, f in zip(xs, fs)]
    for rows, out in zip(subs, outs):
        o_ref[rows, :] = out
    if with_proj:
        _mixer_projection(outs, subs, *refs[6:n_in], *refs[n_in + 1:n_in + n_out])


def _mixer_projection(xs, subs, g_ref, wqt_ref, wk_ref, wvt_ref, wp_ref,
                      qt_ref, k_ref, kmean_ref, vt_ref, p_ref):
    hs = [_rmsnorm(x, g_ref[...]).astype(jnp.bfloat16) for x in xs]
    for blk, (rows, h) in enumerate(zip(subs, hs)):
        qt = lax.dot_general(wqt_ref[...], h, _NT, preferred_element_type=jnp.float32)
        k = _dot(h, wk_ref[...])
        vt = lax.dot_general(wvt_ref[...], h, _NT, preferred_element_type=jnp.float32)
        p_ref[rows, :] = _dot(h, wp_ref[...])
        qt_ref[blk] = (qt * (ATTN_HEAD_DIM ** -0.5 * LOG2_E)).astype(jnp.bfloat16)
        vt_ref[blk] = vt.astype(jnp.bfloat16)
        k_ref[blk] = k.astype(jnp.bfloat16)
        kmean_ref[blk] = jnp.sum(k, axis=0, keepdims=True) * (1.0 / MOBA_BLOCK)


def _ffn(x, pre_g, wg, wu, wd, post_g, proj=None):
    n, d = x.shape
    d_ff = wg.shape[1]
    assert d % WEIGHT_CHUNK_ROWS == 0 and d_ff % WEIGHT_CHUNK_ROWS == 0
    assert FFN_SUB_TILE == MOBA_BLOCK
    tile = pl.BlockSpec((TOKEN_TILE, d), lambda i: (i, 0))
    hbm = pl.BlockSpec(memory_space=pl.ANY)
    in_specs = [tile, _const_spec((1, d)), hbm, hbm, hbm, _const_spec((1, d))]
    out_specs = [tile]
    out_shape = [jax.ShapeDtypeStruct((n, d), jnp.float32)]
    operands = [x, pre_g, wg, wu, wd, post_g]
    if proj is not None:
        bpt = TOKEN_TILE // MOBA_BLOCK
        nblk = n // MOBA_BLOCK
        in_specs += [_const_spec(w.shape) for w in proj]
        operands += list(proj)
        blocked = lambda r, c: pl.BlockSpec((bpt, r, c), lambda i: (i, 0, 0))
        out_specs += [blocked(D_ATTN, MOBA_BLOCK), blocked(MOBA_BLOCK, D_ATTN), blocked(1, D_ATTN),
                      blocked(D_ATTN, MOBA_BLOCK), pl.BlockSpec((TOKEN_TILE, D_POOL), lambda i: (i, 0))]
        out_shape += [jax.ShapeDtypeStruct((nblk, D_ATTN, MOBA_BLOCK), jnp.bfloat16),
                      jax.ShapeDtypeStruct((nblk, MOBA_BLOCK, D_ATTN), jnp.bfloat16),
                      jax.ShapeDtypeStruct((nblk, 1, D_ATTN), jnp.float32),
                      jax.ShapeDtypeStruct((nblk, D_ATTN, MOBA_BLOCK), jnp.bfloat16),
                      jax.ShapeDtypeStruct((n, D_POOL), jnp.float32)]
    return pl.pallas_call(
        functools.partial(_ffn_kernel, with_proj=proj is not None),
        grid=(n // TOKEN_TILE,),
        in_specs=in_specs,
        out_specs=out_specs,
        out_shape=out_shape,
        scratch_shapes=[
            pltpu.VMEM((d, d_ff), jnp.bfloat16), pltpu.VMEM((d, d_ff), jnp.bfloat16),
            pltpu.VMEM((d_ff, d), jnp.bfloat16),
            pltpu.VMEM((STAGE_SLOTS, WEIGHT_CHUNK_ROWS, max(d, d_ff)), jnp.float32),
            pltpu.SemaphoreType.DMA((STAGE_SLOTS,)),
        ],
        compiler_params=_params("arbitrary"),
        name="ffn_mixer_proj" if proj is not None else "ffn",
    )(*operands)


MOBA_TILE = 2 * MOBA_BLOCK
SELECT_UNROLL = 4
ONES_ROWS = 16


def _moba_items(n_tiles):
    tiles, pairs = [], []
    for c in range(n_tiles):
        for pr in range(c, -1, -1):
            tiles.append(c)
            pairs.append(pr)
    return np.asarray(tiles, np.int32), np.asarray(pairs, np.int32)


def _moba_kernel(slopes_ref, tile_tab_ref, pair_tab_ref, qt_ref, k_ref, kmean_ref, vt_ref,
                 o_ref, sel_ref, bias_ref, s_ref, p_ref, acc_ref):
    hp = pl.program_id(1)
    L, T = MOBA_BLOCK, MOBA_TILE
    nb = vt_ref.shape[1]
    n_tiles = nb // 2
    n_items = tile_tab_ref.shape[0]
    heads = range(HEADS_PER_STEP)
    slopes = [slopes_ref[hp * HEADS_PER_STEP + hh] for hh in heads]
    head_rows = [slice(hh * ATTN_HEAD_DIM, (hh + 1) * ATTN_HEAD_DIM) for hh in heads]
    head_row = lax.broadcasted_iota(jnp.int32, (HEADS_PER_STEP * ATTN_HEAD_DIM, T), 0) // ATTN_HEAD_DIM

    def query_tile(c):
        qt = jnp.concatenate([qt_ref[0, 2 * c], qt_ref[0, 2 * c + 1]], axis=1)
        return [jnp.where(head_row == hh, qt, jnp.zeros_like(qt)) for hh in heads]

    key = lax.broadcasted_iota(jnp.int32, (T, T), 0)
    query = lax.broadcasted_iota(jnp.int32, (T, T), 1)
    q_minus_k = (query - key).astype(jnp.float32)
    non_causal = jnp.where(key > query, jnp.inf, 0.0)
    for hh in heads:
        base = slopes[hh] * q_minus_k
        bias_ref[hh, 0] = base
        bias_ref[hh, 1] = base + non_causal

    kmean = kmean_ref[0].astype(jnp.bfloat16)
    blk_id = lax.broadcasted_iota(jnp.int32, (nb, T), 0)
    second_half = lax.broadcasted_iota(jnp.int32, (nb, T), 1) // L

    def select_tile(c):
        qt_heads = query_tile(c)
        own = 2 * c + second_half
        for hh in heads:
            gate = jnp.where(blk_id < own, _dot(kmean, qt_heads[hh]), -jnp.inf)
            sel = blk_id == own
            for _ in range(MOBA_TOPK):
                best = jnp.max(gate, axis=0, keepdims=True)
                first = jnp.min(jnp.where(gate == best, blk_id, nb), axis=0, keepdims=True)
                taken = blk_id == first
                sel = sel | (taken & (blk_id < own))
                gate = jnp.where(taken, -jnp.inf, gate)
            sel_ref[hh, c] = sel.astype(jnp.float32)

    def select(it, _):
        for u in range(SELECT_UNROLL):
            select_tile(SELECT_UNROLL * it + u)
        return 0

    lax.fori_loop(0, n_tiles // SELECT_UNROLL, select, 0)

    blk_rows = [slice(blk * L, (blk + 1) * L) for blk in range(2)]
    ones_rows = jnp.ones((ONES_ROWS, L), jnp.bfloat16)

    def biased_scores(e, slot):
        c, pr = tile_tab_ref[e], pair_tab_ref[e]
        variant = jnp.where(pr == c, 1, 0)
        k_pair = k_ref[0, pl.ds(pl.multiple_of(pr * T, T), T), :]
        col_max = []
        for hh, qt_h in enumerate(query_tile(c)):
            t = _dot(k_pair, qt_h) - bias_ref[hh, variant]
            s_ref[slot, hh] = t
            col_max.append(tuple(jnp.max(t[rows, :], axis=0, keepdims=True) for rows in blk_rows))
        return tuple(col_max)

    def value_product(pr, slot):
        out = []
        for hh in heads:
            lhs = [jnp.concatenate([vt_ref[0, 2 * pr + blk, head_rows[hh], :], ones_rows], axis=0)
                   for blk in range(2)]
            out.append(_dot(lhs[0], p_ref[slot, hh, blk_rows[0], :])
                       + _dot(lhs[1], p_ref[slot, hh, blk_rows[1], :]))
        return out

    def write_tile(c):
        o_t = jnp.concatenate(
            [acc_ref[hh, :ATTN_HEAD_DIM, :] / acc_ref[hh, ATTN_HEAD_DIM:ATTN_HEAD_DIM + 1, :]
             for hh in heads], axis=0)
        o_ref[0, pl.ds(pl.multiple_of(c * T, T), T), :] = o_t.T.astype(o_ref.dtype)

    def item(e, slot, state):
        m, alpha_lag1, alpha_lag2, col_max = state
        other = 1 - slot
        c, pr = tile_tab_ref[e], pair_tab_ref[e]
        e_lag2 = jnp.maximum(e - 2, 0)

        pv_lag2 = value_product(pair_tab_ref[e_lag2], slot)
        col_max_next = biased_scores(jnp.minimum(e + 1, n_items - 1), other)

        opens = pr == c
        m_out, alpha_out = [], []
        for hh in heads:
            m_old = jnp.where(opens, -jnp.inf, m[hh])
            offset = slopes[hh] * ((c - pr) * T).astype(jnp.float32)
            keep = [sel_ref[hh, c, pl.ds(2 * pr + blk, 1), :] > 0.5 for blk in range(2)]
            m_blk = [jnp.where(keep[blk], col_max[hh][blk], -jnp.inf) for blk in range(2)]
            m_new = jnp.maximum(m_old, jnp.maximum(m_blk[0], m_blk[1]) - offset)
            for blk in range(2):
                shift = jnp.where(keep[blk], m_new + offset, jnp.inf)
                p = jnp.exp2(s_ref[slot, hh, blk_rows[blk], :] - shift)
                p_ref[slot, hh, blk_rows[blk], :] = p.astype(jnp.bfloat16)
            m_out.append(m_new)
            alpha_out.append(jnp.exp2(m_old - m_new))

        fold(pv_lag2, alpha_lag2)

        @pl.when((pair_tab_ref[e_lag2] == 0) & (e >= 2))
        def _():
            write_tile(tile_tab_ref[e_lag2])

        return tuple(m_out), tuple(alpha_out), alpha_lag1, col_max_next

    def fold(pv, alpha):
        for hh in heads:
            acc_ref[hh] = alpha[hh] * acc_ref[hh] + pv[hh]

    col_max = biased_scores(0, 0)
    for hh in heads:
        acc_ref[hh] = jnp.zeros(acc_ref.shape[1:], jnp.float32)
    p_ref[...] = jnp.zeros(p_ref.shape, p_ref.dtype)
    row = lambda v: tuple(jnp.full((1, T), v, jnp.float32) for _ in heads)
    state = (row(-jnp.inf), row(0.0), row(0.0), col_max)

    def two_items(it, state):
        state = item(2 * it, 0, state)
        return item(2 * it + 1, 1, state)

    _, alpha_lag1, alpha_lag2, _ = lax.fori_loop(0, n_items // 2, two_items, state)

    for e, alpha in ((n_items - 2, alpha_lag2), (n_items - 1, alpha_lag1)):
        fold(value_product(pair_tab_ref[e], e % 2), alpha)
    write_tile(tile_tab_ref[n_items - 1])


def _moba(slopes, qt, k, kmean, vt):
    b, nb, _, L = qt.shape
    lanes = HEADS_PER_STEP * ATTN_HEAD_DIM
    T = MOBA_TILE
    tile_tab, pair_tab = _moba_items(nb // 2)
    assert len(tile_tab) % 2 == 0
    smem = pl.BlockSpec(memory_space=pltpu.SMEM)
    return pl.pallas_call(
        _moba_kernel,
        grid=(b, N_ATTN_HEADS // HEADS_PER_STEP),
        in_specs=[
            smem, smem, smem,
            pl.BlockSpec((1, nb, lanes, L), lambda bi, hp: (bi, 0, hp, 0)),
            pl.BlockSpec((1, nb * L, lanes), lambda bi, hp: (bi, 0, hp)),
            pl.BlockSpec((1, nb, lanes), lambda bi, hp: (bi, 0, hp)),
            pl.BlockSpec((1, nb, lanes, L), lambda bi, hp: (bi, 0, hp, 0)),
        ],
        out_specs=pl.BlockSpec((1, nb * L, lanes), lambda bi, hp: (bi, 0, hp)),
        out_shape=jax.ShapeDtypeStruct((b, nb * L, D_ATTN), jnp.bfloat16),
        scratch_shapes=[
            pltpu.VMEM((HEADS_PER_STEP, nb // 2, nb, T), jnp.float32),
            pltpu.VMEM((HEADS_PER_STEP, 2, T, T), jnp.float32),
            pltpu.VMEM((2, HEADS_PER_STEP, T, T), jnp.float32),
            pltpu.VMEM((2, HEADS_PER_STEP, T, T), jnp.bfloat16),
            pltpu.VMEM((HEADS_PER_STEP, ATTN_HEAD_DIM + ONES_ROWS, T), jnp.float32),
        ],
        compiler_params=_params("parallel", "parallel"),
        name="moba_attention",
    )(slopes, jnp.asarray(tile_tab), jnp.asarray(pair_tab), qt, k, kmean, vt)


def _pooling_mixer(ext, p, first_pos, pool_w_ref, pool_scale_ref):
    rows = p.shape[0]
    pos = first_pos + lax.broadcasted_iota(jnp.int32, (rows, 1), 0)
    pooled = []
    for g, w in enumerate(POOL_WINDOWS):
        cols = slice(g * POOL_GROUP_DIM, (g + 1) * POOL_GROUP_DIM)
        acc = ext[:, cols]
        n_valid = acc.shape[0]
        span = 1
        while span < w:
            acc = acc[span:, :] + acc[:n_valid - span, :]
            n_valid -= span
            span *= 2
        win_sum = acc[n_valid - rows:, :]
        cnt = jnp.minimum(pos + 1, w).astype(jnp.float32)
        d = win_sum / cnt - p[:, cols]
        y = _dot(d.astype(jnp.bfloat16), pool_w_ref[g])
        pooled.append(y * pool_scale_ref[:, cols])
    return jnp.concatenate(pooled, axis=-1).astype(jnp.bfloat16)


def _mix_xattn_kernel(x_ref, attn_ref, p_ref, halo_ref, pool_w_ref, pool_scale_ref,
                      w_out_ref, mix_g_ref, pre_g_ref, wq_ref, mem_ref, mem_g_ref, wkv_ref,
                      wo_ref, post_g_ref, o_ref, k_ref, v_ref):
    t = pl.program_id(1)
    tm, d = x_ref.shape[1], x_ref.shape[2]
    dh = d // XA_HEADS

    @pl.when(t == 0)
    def _():
        mem_n = _rmsnorm(mem_ref[0], mem_g_ref[...]).astype(jnp.bfloat16)
        kv = _dot(mem_n, wkv_ref[...])
        k_ref[...] = kv[:, :d].astype(jnp.bfloat16)
        v_ref[...] = kv[:, d:].astype(jnp.bfloat16)

    subs = [slice(i * MIX_SUB_TILE, (i + 1) * MIX_SUB_TILE) for i in range(tm // MIX_SUB_TILE)]
    halo = jnp.where(t > 0, halo_ref[0], 0.0)
    ext = jnp.concatenate([halo, p_ref[0]], axis=0)
    pools = [_pooling_mixer(ext[rows.start:rows.stop + POOL_HALO, :], p_ref[0, rows, :],
                            t * tm + rows.start, pool_w_ref, pool_scale_ref) for rows in subs]
    ys = [_dot(jnp.concatenate([attn_ref[0, rows, :], pool], axis=-1), w_out_ref[...])
          for rows, pool in zip(subs, pools)]
    xs = [x_ref[0, rows, :] + _rmsnorm(y, mix_g_ref[...]) for rows, y in zip(subs, ys)]

    hs = [_rmsnorm(x, pre_g_ref[...]).astype(jnp.bfloat16) for x in xs]
    qs = [_dot(h, wq_ref[...]).astype(jnp.bfloat16) for h in hs]
    heads, cs = [[] for _ in subs], []
    head_cols = [slice(hd * dh, (hd + 1) * dh) for hd in range(XA_HEADS)]
    ss = [[lax.dot_general(q[:, cols], k_ref[:, cols], _NT,
                           preferred_element_type=jnp.float32) * (dh ** -0.5) for q in qs]
          for cols in head_cols]
    for i in range(len(subs)):
        for hd, cols in enumerate(head_cols):
            s = ss[hd][i]
            e = jnp.exp(s - jnp.max(s, axis=-1, keepdims=True))
            p = e / jnp.sum(e, axis=-1, keepdims=True)
            heads[i].append(_dot(p.astype(jnp.bfloat16), v_ref[:, cols]))
        cs.append(_dot(jnp.concatenate(heads[i], axis=-1).astype(jnp.bfloat16), wo_ref[...]))
    for rows, x, c in zip(subs, xs, cs):
        o_ref[0, rows, :] = x + _rmsnorm(c, post_g_ref[...])


def _mix_xattn(x, attn, p_in, pool_w, pool_scale, w_out, mix_g, pre_g, wq, mem, mem_g, wkv, wo,
               post_g):
    b, s, d = x.shape
    m = mem.shape[1]
    halo_blocks = MIX_TOKEN_TILE // POOL_HALO
    tile = lambda width: pl.BlockSpec((1, MIX_TOKEN_TILE, width), lambda bi, t: (bi, t, 0))
    mem_blk = pl.BlockSpec((1, m, d), lambda bi, t: (bi, 0, 0))
    return pl.pallas_call(
        _mix_xattn_kernel,
        grid=(b, s // MIX_TOKEN_TILE),
        in_specs=[
            tile(d), tile(D_ATTN), tile(D_POOL),
            pl.BlockSpec((1, POOL_HALO, D_POOL),
                         lambda bi, t: (bi, jnp.maximum(t * halo_blocks - 1, 0), 0)),
            _const_spec(pool_w.shape), _const_spec((1, D_POOL)),
            _const_spec(w_out.shape), _const_spec((1, d)),
            _const_spec((1, d)), _const_spec(wq.shape),
            mem_blk, _const_spec((1, d)), _const_spec(wkv.shape),
            _const_spec(wo.shape), _const_spec((1, d)),
        ],
        out_specs=tile(d),
        out_shape=jax.ShapeDtypeStruct((b, s, d), jnp.float32),
        scratch_shapes=[pltpu.VMEM((m, d), jnp.bfloat16), pltpu.VMEM((m, d), jnp.bfloat16)],
        compiler_params=_params("parallel", "arbitrary"),
        name="mixer_out_mem_xattn",
    )(x, attn, p_in, p_in, pool_w, pool_scale, w_out, mix_g, pre_g, wq, mem, mem_g, wkv, wo, post_g)


def _alibi_slopes(n_heads):
    return jnp.asarray(2.0 ** (-8.0 * np.arange(1, n_heads + 1) / n_heads), jnp.float32)


def kernel(x, mem, ffn1_pre_g, ffn1_w_gate, ffn1_w_up, ffn1_w_down, ffn1_post_g, mix_pre_g, w_in, pool_w, pool_scale, w_out, mix_post_g, xa_pre_g, mem_g, xa_wq, xa_wkv, xa_wo, xa_post_g, ffn2_pre_g, ffn2_w_gate, ffn2_w_up, ffn2_w_down, ffn2_post_g):
    b, s, d = x.shape
    depth = ffn1_pre_g.shape[0]
    bf = lambda w: w.astype(jnp.bfloat16)
    row = lambda g: g.reshape(1, -1)
    slopes = _alibi_slopes(N_ATTN_HEADS) * LOG2_E
    for l in range(depth):
        w = bf(w_in[l])
        x, qt, k, kmean, vt, p_in = _ffn(
            x.reshape(b * s, d), row(ffn1_pre_g[l]), ffn1_w_gate[l], ffn1_w_up[l], ffn1_w_down[l],
            row(ffn1_post_g[l]),
            proj=(row(mix_pre_g[l]), w[:, :D_ATTN].T, w[:, D_ATTN:2 * D_ATTN],
                  w[:, 2 * D_ATTN:3 * D_ATTN].T, w[:, 3 * D_ATTN:]))
        nb = s // MOBA_BLOCK
        x, p_in = x.reshape(b, s, d), p_in.reshape(b, s, D_POOL)
        qt, vt = (t.reshape(b, nb, D_ATTN, MOBA_BLOCK) for t in (qt, vt))
        attn = _moba(slopes, qt, k.reshape(b, s, D_ATTN), kmean.reshape(b, -1, D_ATTN), vt)
        x = _mix_xattn(x, attn, p_in, bf(pool_w[l]), row(pool_scale[l]), bf(w_out[l]),
                       row(mix_post_g[l]), row(xa_pre_g[l]), bf(xa_wq[l]), mem, row(mem_g[l]),
                       bf(xa_wkv[l]), bf(xa_wo[l]), row(xa_post_g[l]))

        x = _ffn(x.reshape(b * s, d), row(ffn2_pre_g[l]), ffn2_w_gate[l], ffn2_w_up[l],
                 ffn2_w_down[l], row(ffn2_post_g[l]))[0].reshape(b, s, d)
    return x
```

```python
import functools

import numpy as np
import jax
import jax.numpy as jnp
from jax import lax
from jax.experimental import pallas as pl
from jax.experimental.pallas import tpu as pltpu

N_ATTN_HEADS = 8
ATTN_HEAD_DIM = 64
D_ATTN = N_ATTN_HEADS * ATTN_HEAD_DIM
POOL_WINDOWS = (2, 4, 8, 16)
POOL_GROUP_DIM = 128
D_POOL = len(POOL_WINDOWS) * POOL_GROUP_DIM
MOBA_BLOCK = 256
MOBA_TOPK = 3
XA_HEADS = 4
MACARON_WEIGHT = 0.5
RMS_EPS = 1e-6
LOG2_E = 1.4426950408889634

V7X_LANES = 128
V7X_VMEM_BYTES = 64 * 1024 * 1024
VMEM_LIMIT_BYTES = V7X_VMEM_BYTES * 7 // 8

TOKEN_TILE = 512
FFN_SUB_TILE = 256
WEIGHT_CHUNK_ROWS = 128
STAGE_SLOTS = 6
HEADS_PER_STEP = V7X_LANES // ATTN_HEAD_DIM
POOL_HALO = max(POOL_WINDOWS)

_NT = (((1,), (1,)), ((), ()))


def _rmsnorm(x, g):
    r = lax.rsqrt(jnp.mean(x * x, axis=-1, keepdims=True) + RMS_EPS)
    return (x * r) * g


def _dot(a, b):
    return jnp.dot(a, b, preferred_element_type=jnp.float32)


def _params(*semantics):
    return pltpu.CompilerParams(dimension_semantics=semantics,
                                vmem_limit_bytes=VMEM_LIMIT_BYTES)


def _const_spec(shape):
    return pl.BlockSpec(shape, lambda *_: (0,) * len(shape))


def _stage_cast(pairs, stage_ref, sem):
    chunks = [(src, dst, r) for src, dst in pairs
              for r in range(0, src.shape[0], WEIGHT_CHUNK_ROWS)]

    def chunk_copy(c):
        src, _, r = chunks[c]
        slot = c % STAGE_SLOTS
        return pltpu.make_async_copy(
            src.at[pl.ds(r, WEIGHT_CHUNK_ROWS), :],
            stage_ref.at[slot, :, pl.ds(0, src.shape[1])], sem.at[slot])

    ahead = STAGE_SLOTS - 1
    for c in range(min(ahead, len(chunks))):
        chunk_copy(c).start()
    for c, (src, dst, r) in enumerate(chunks):
        if c + ahead < len(chunks):
            chunk_copy(c + ahead).start()
        chunk_copy(c).wait()
        dst[pl.ds(r, WEIGHT_CHUNK_ROWS), :] = (
            stage_ref[c % STAGE_SLOTS, :, pl.ds(0, src.shape[1])].astype(jnp.bfloat16))


def _ffn_kernel(*refs, with_proj, with_mix, seq_steps):
    n_in, n_out = (11, 6) if with_proj else (6 + N_MIX_INPUTS, 1) if with_mix else (6, 1)
    x_ref, pre_g_ref, wg_hbm, wu_hbm, wd_hbm, post_g_ref = refs[:6]
    o_ref = refs[n_in]
    wg_ref, wu_ref, wd_ref, stage_ref, sem = refs[n_in + n_out:]

    @pl.when(pl.program_id(0) == 0)
    def _():
        _stage_cast(((wg_hbm, wg_ref), (wu_hbm, wu_ref), (wd_hbm, wd_ref)), stage_ref, sem)

    subs = [slice(i * FFN_SUB_TILE, (i + 1) * FFN_SUB_TILE)
            for i in range(x_ref.shape[0] // FFN_SUB_TILE)]
    xs = [x_ref[rows, :] for rows in subs]
    if with_mix:
        xs = _mixer_out_xattn(xs, subs, pl.program_id(0) % seq_steps, *refs[6:n_in])
    hs = [_rmsnorm(x, pre_g_ref[...]).astype(jnp.bfloat16) for x in xs]
    gates = [_dot(h, wg_ref[...]) for h in hs]
    ups = [_dot(h, wu_ref[...]) for h in hs]
    acts = [(g * jax.nn.sigmoid(g) * u).astype(jnp.bfloat16) for g, u in zip(gates, ups)]
    fs = [_dot(a, wd_ref[...]) for a in acts]
    outs = [x + MACARON_WEIGHT * _rmsnorm(f, post_g_ref[...]) for x, f in zip(xs, fs)]
    for rows, out in zip(subs, outs):
        o_ref[rows, :] = out
    if with_proj:
        _mixer_projection(outs, subs, *refs[6:n_in], *refs[n_in + 1:n_in + n_out])


def _mixer_projection(xs, subs, g_ref, wqt_ref, wk_ref, wvt_ref, wp_ref,
                      qt_ref, k_ref, kmean_ref, vt_ref, p_ref):
    hs = [_rmsnorm(x, g_ref[...]).astype(jnp.bfloat16) for x in xs]
    for blk, (rows, h) in enumerate(zip(subs, hs)):
        qt = lax.dot_general(wqt_ref[...], h, _NT, preferred_element_type=jnp.float32)
        k = _dot(h, wk_ref[...])
        vt = lax.dot_general(wvt_ref[...], h, _NT, preferred_element_type=jnp.float32)
        p_ref[rows, :] = _dot(h, wp_ref[...])
        qt_ref[blk] = (qt * (ATTN_HEAD_DIM ** -0.5 * LOG2_E)).astype(jnp.bfloat16)
        vt_ref[blk] = vt.astype(jnp.bfloat16)
        k_ref[blk] = k.astype(jnp.bfloat16)
        kmean_ref[blk] = jnp.sum(k, axis=0, keepdims=True) * (1.0 / MOBA_BLOCK)


def _ffn(x, pre_g, wg, wu, wd, post_g, proj=None, mix=None, seq_len=None):
    n, d = x.shape
    d_ff = wg.shape[1]
    assert d % WEIGHT_CHUNK_ROWS == 0 and d_ff % WEIGHT_CHUNK_ROWS == 0
    assert FFN_SUB_TILE == MOBA_BLOCK
    tile = pl.BlockSpec((TOKEN_TILE, d), lambda i: (i, 0))
    hbm = pl.BlockSpec(memory_space=pl.ANY)
    in_specs = [tile, _const_spec((1, d)), hbm, hbm, hbm, _const_spec((1, d))]
    out_specs = [tile]
    out_shape = [jax.ShapeDtypeStruct((n, d), jnp.float32)]
    operands = [x, pre_g, wg, wu, wd, post_g]
    if proj is not None:
        bpt = TOKEN_TILE // MOBA_BLOCK
        nblk = n // MOBA_BLOCK
        in_specs += [_const_spec(w.shape) for w in proj]
        operands += list(proj)
        blocked = lambda r, c: pl.BlockSpec((bpt, r, c), lambda i: (i, 0, 0))
        out_specs += [blocked(D_ATTN, MOBA_BLOCK), blocked(MOBA_BLOCK, D_ATTN), blocked(1, D_ATTN),
                      blocked(D_ATTN, MOBA_BLOCK), pl.BlockSpec((TOKEN_TILE, D_POOL), lambda i: (i, 0))]
        out_shape += [jax.ShapeDtypeStruct((nblk, D_ATTN, MOBA_BLOCK), jnp.bfloat16),
                      jax.ShapeDtypeStruct((nblk, MOBA_BLOCK, D_ATTN), jnp.bfloat16),
                      jax.ShapeDtypeStruct((nblk, 1, D_ATTN), jnp.float32),
                      jax.ShapeDtypeStruct((nblk, D_ATTN, MOBA_BLOCK), jnp.bfloat16),
                      jax.ShapeDtypeStruct((n, D_POOL), jnp.float32)]
    seq_steps = None
    if mix is not None:
        attn, p_in, pool_w, pool_scale, w_out, mix_g, xa_pre_g, wq, k_mem, v_mem, wo, xa_post_g = mix
        seq_steps = seq_len // TOKEN_TILE
        halo_blocks = TOKEN_TILE // POOL_HALO
        m = k_mem.shape[1]
        mem_blk = pl.BlockSpec((1, m, d), lambda i: (i // seq_steps, 0, 0))
        in_specs += [
            pl.BlockSpec((TOKEN_TILE, D_ATTN), lambda i: (i, 0)),
            pl.BlockSpec((TOKEN_TILE, D_POOL), lambda i: (i, 0)),
            pl.BlockSpec((POOL_HALO, D_POOL), lambda i: (jnp.maximum(i * halo_blocks - 1, 0), 0)),
            _const_spec(pool_w.shape), _const_spec((1, D_POOL)), _const_spec(w_out.shape),
            _const_spec((1, d)), _const_spec((1, d)), _const_spec(wq.shape), mem_blk, mem_blk,
            _const_spec(wo.shape), _const_spec((1, d))]
        operands += [attn, p_in, p_in, pool_w, pool_scale, w_out, mix_g, xa_pre_g, wq, k_mem, v_mem,
                     wo, xa_post_g]
        assert len(operands) == 6 + N_MIX_INPUTS
    return pl.pallas_call(
        functools.partial(_ffn_kernel, with_proj=proj is not None, with_mix=mix is not None,
                          seq_steps=seq_steps),
        grid=(n // TOKEN_TILE,),
        in_specs=in_specs,
        out_specs=out_specs,
        out_shape=out_shape,
        scratch_shapes=[
            pltpu.VMEM((d, d_ff), jnp.bfloat16), pltpu.VMEM((d, d_ff), jnp.bfloat16),
            pltpu.VMEM((d_ff, d), jnp.bfloat16),
            pltpu.VMEM((STAGE_SLOTS, WEIGHT_CHUNK_ROWS, max(d, d_ff)), jnp.float32),
            pltpu.SemaphoreType.DMA((STAGE_SLOTS,)),
        ],
        compiler_params=_params("arbitrary"),
        name="ffn_mixer_proj" if proj is not None else "mixer_out_xattn_ffn" if mix is not None else "ffn",
    )(*operands)


MOBA_TILE = 2 * MOBA_BLOCK
SELECT_UNROLL = 4
ONES_ROWS = 16


def _moba_items(n_tiles):
    tiles, pairs = [], []
    for c in range(n_tiles):
        for pr in range(c, -1, -1):
            tiles.append(c)
            pairs.append(pr)
    return np.asarray(tiles, np.int32), np.asarray(pairs, np.int32)


def _moba_kernel(slopes_ref, tile_tab_ref, pair_tab_ref, qt_ref, k_ref, kmean_ref, vt_ref,
                 o_ref, sel_ref, bias_ref, s_ref, p_ref, acc_ref):
    hp = pl.program_id(1)
    L, T = MOBA_BLOCK, MOBA_TILE
    nb = vt_ref.shape[1]
    n_tiles = nb // 2
    n_items = tile_tab_ref.shape[0]
    heads = range(HEADS_PER_STEP)
    slopes = [slopes_ref[hp * HEADS_PER_STEP + hh] for hh in heads]
    head_rows = [slice(hh * ATTN_HEAD_DIM, (hh + 1) * ATTN_HEAD_DIM) for hh in heads]
    head_row = lax.broadcasted_iota(jnp.int32, (HEADS_PER_STEP * ATTN_HEAD_DIM, T), 0) // ATTN_HEAD_DIM

    def query_tile(c):
        qt = jnp.concatenate([qt_ref[0, 2 * c], qt_ref[0, 2 * c + 1]], axis=1)
        return [jnp.where(head_row == hh, qt, jnp.zeros_like(qt)) for hh in heads]

    key = lax.broadcasted_iota(jnp.int32, (T, T), 0)
    query = lax.broadcasted_iota(jnp.int32, (T, T), 1)
    q_minus_k = (query - key).astype(jnp.float32)
    non_causal = jnp.where(key > query, jnp.inf, 0.0)
    for hh in heads:
        base = slopes[hh] * q_minus_k
        bias_ref[hh, 0] = base
        bias_ref[hh, 1] = base + non_causal

    kmean = kmean_ref[0].astype(jnp.bfloat16)
    blk_id = lax.broadcasted_iota(jnp.int32, (nb, T), 0)
    second_half = lax.broadcasted_iota(jnp.int32, (nb, T), 1) // L

    def select_tile(c):
        qt_heads = query_tile(c)
        own = 2 * c + second_half
        for hh in heads:
            gate = jnp.where(blk_id < own, _dot(kmean, qt_heads[hh]), -jnp.inf)
            sel = blk_id == own
            for _ in range(MOBA_TOPK):
                best = jnp.max(gate, axis=0, keepdims=True)
                first = jnp.min(jnp.where(gate == best, blk_id, nb), axis=0, keepdims=True)
                taken = blk_id == first
                sel = sel | (taken & (blk_id < own))
                gate = jnp.where(taken, -jnp.inf, gate)
            sel_ref[hh, c] = sel.astype(jnp.float32)

    def select(it, _):
        for u in range(SELECT_UNROLL):
            select_tile(SELECT_UNROLL * it + u)
        return 0

    lax.fori_loop(0, n_tiles // SELECT_UNROLL, select, 0)

    blk_rows = [slice(blk * L, (blk + 1) * L) for blk in range(2)]
    ones_rows = jnp.ones((ONES_ROWS, L), jnp.bfloat16)

    def biased_scores(e, slot):
        c, pr = tile_tab_ref[e], pair_tab_ref[e]
        variant = jnp.where(pr == c, 1, 0)
        k_pair = k_ref[0, pl.ds(pl.multiple_of(pr * T, T), T), :]
        col_max = []
        for hh, qt_h in enumerate(query_tile(c)):
            t = _dot(k_pair, qt_h) - bias_ref[hh, variant]
            s_ref[slot, hh] = t
            col_max.append(tuple(jnp.max(t[rows, :], axis=0, keepdims=True) for rows in blk_rows))
        return tuple(col_max)

    def value_product(pr, slot):
        out = []
        for hh in heads:
            lhs = [jnp.concatenate([vt_ref[0, 2 * pr + blk, head_rows[hh], :], ones_rows], axis=0)
                   for blk in range(2)]
            out.append(_dot(lhs[0], p_ref[slot, hh, blk_rows[0], :])
                       + _dot(lhs[1], p_ref[slot, hh, blk_rows[1], :]))
        return out

    def write_tile(c):
        o_t = jnp.concatenate(
            [acc_ref[hh, :ATTN_HEAD_DIM, :] / acc_ref[hh, ATTN_HEAD_DIM:ATTN_HEAD_DIM + 1, :]
             for hh in heads], axis=0)
        o_ref[0, pl.ds(pl.multiple_of(c * T, T), T), :] = o_t.T.astype(o_ref.dtype)

    def item(e, slot, state):
        m, alpha_lag1, alpha_lag2, col_max = state
        other = 1 - slot
        c, pr = tile_tab_ref[e], pair_tab_ref[e]
        e_lag2 = jnp.maximum(e - 2, 0)

        pv_lag2 = value_product(pair_tab_ref[e_lag2], slot)
        col_max_next = biased_scores(jnp.minimum(e + 1, n_items - 1), other)

        opens = pr == c
        m_out, alpha_out = [], []
        for hh in heads:
            m_old = jnp.where(opens, -jnp.inf, m[hh])
            offset = slopes[hh] * ((c - pr) * T).astype(jnp.float32)
            keep = [sel_ref[hh, c, pl.ds(2 * pr + blk, 1), :] > 0.5 for blk in range(2)]
            m_blk = [jnp.where(keep[blk], col_max[hh][blk], -jnp.inf) for blk in range(2)]
            m_new = jnp.maximum(m_old, jnp.maximum(m_blk[0], m_blk[1]) - offset)
            for blk in range(2):
                shift = jnp.where(keep[blk], m_new + offset, jnp.inf)
                p = jnp.exp2(s_ref[slot, hh, blk_rows[blk], :] - shift)
                p_ref[slot, hh, blk_rows[blk], :] = p.astype(jnp.bfloat16)
            m_out.append(m_new)
            alpha_out.append(jnp.exp2(m_old - m_new))

        fold(pv_lag2, alpha_lag2)

        @pl.when((pair_tab_ref[e_lag2] == 0) & (e >= 2))
        def _():
            write_tile(tile_tab_ref[e_lag2])

        return tuple(m_out), tuple(alpha_out), alpha_lag1, col_max_next

    def fold(pv, alpha):
        for hh in heads:
            acc_ref[hh] = alpha[hh] * acc_ref[hh] + pv[hh]

    col_max = biased_scores(0, 0)
    for hh in heads:
        acc_ref[hh] = jnp.zeros(acc_ref.shape[1:], jnp.float32)
    p_ref[...] = jnp.zeros(p_ref.shape, p_ref.dtype)
    row = lambda v: tuple(jnp.full((1, T), v, jnp.float32) for _ in heads)
    state = (row(-jnp.inf), row(0.0), row(0.0), col_max)

    def two_items(it, state):
        state = item(2 * it, 0, state)
        return item(2 * it + 1, 1, state)

    _, alpha_lag1, alpha_lag2, _ = lax.fori_loop(0, n_items // 2, two_items, state)

    for e, alpha in ((n_items - 2, alpha_lag2), (n_items - 1, alpha_lag1)):
        fold(value_product(pair_tab_ref[e], e % 2), alpha)
    write_tile(tile_tab_ref[n_items - 1])


def _moba(slopes, qt, k, kmean, vt):
    b, nb, _, L = qt.shape
    lanes = HEADS_PER_STEP * ATTN_HEAD_DIM
    T = MOBA_TILE
    tile_tab, pair_tab = _moba_items(nb // 2)
    assert len(tile_tab) % 2 == 0
    smem = pl.BlockSpec(memory_space=pltpu.SMEM)
    return pl.pallas_call(
        _moba_kernel,
        grid=(b, N_ATTN_HEADS // HEADS_PER_STEP),
        in_specs=[
            smem, smem, smem,
            pl.BlockSpec((1, nb, lanes, L), lambda bi, hp: (bi, 0, hp, 0)),
            pl.BlockSpec((1, nb * L, lanes), lambda bi, hp: (bi, 0, hp)),
            pl.BlockSpec((1, nb, lanes), lambda bi, hp: (bi, 0, hp)),
            pl.BlockSpec((1, nb, lanes, L), lambda bi, hp: (bi, 0, hp, 0)),
        ],
        out_specs=pl.BlockSpec((1, nb * L, lanes), lambda bi, hp: (bi, 0, hp)),
        out_shape=jax.ShapeDtypeStruct((b, nb * L, D_ATTN), jnp.bfloat16),
        scratch_shapes=[
            pltpu.VMEM((HEADS_PER_STEP, nb // 2, nb, T), jnp.float32),
            pltpu.VMEM((HEADS_PER_STEP, 2, T, T), jnp.float32),
            pltpu.VMEM((2, HEADS_PER_STEP, T, T), jnp.float32),
            pltpu.VMEM((2, HEADS_PER_STEP, T, T), jnp.bfloat16),
            pltpu.VMEM((HEADS_PER_STEP, ATTN_HEAD_DIM + ONES_ROWS, T), jnp.float32),
        ],
        compiler_params=_params("parallel", "parallel"),
        name="moba_attention",
    )(slopes, jnp.asarray(tile_tab), jnp.asarray(pair_tab), qt, k, kmean, vt)


def _memkv_kernel(mem_ref, g_ref, wkv_ref, k_ref, v_ref):
    d = mem_ref.shape[2]
    mem_n = _rmsnorm(mem_ref[0], g_ref[...]).astype(jnp.bfloat16)
    kv = _dot(mem_n, wkv_ref[...])
    k_ref[0] = kv[:, :d].astype(jnp.bfloat16)
    v_ref[0] = kv[:, d:].astype(jnp.bfloat16)


def _memkv(mem, g, wkv):
    b, m, d = mem.shape
    blk = pl.BlockSpec((1, m, d), lambda bi: (bi, 0, 0))
    return pl.pallas_call(
        _memkv_kernel,
        grid=(b,),
        in_specs=[blk, _const_spec((1, d)), _const_spec(wkv.shape)],
        out_specs=[blk, blk],
        out_shape=[jax.ShapeDtypeStruct((b, m, d), jnp.bfloat16)] * 2,
        compiler_params=_params("parallel"),
        name="mem_kv",
    )(mem, g, wkv)


def _pooling_mixer(ext, p, first_pos, pool_w_ref, pool_scale_ref):
    rows = p.shape[0]
    pos = first_pos + lax.broadcasted_iota(jnp.int32, (rows, 1), 0)
    pooled = []
    for g, w in enumerate(POOL_WINDOWS):
        cols = slice(g * POOL_GROUP_DIM, (g + 1) * POOL_GROUP_DIM)
        acc = ext[:, cols]
        n_valid = acc.shape[0]
        span = 1
        while span < w:
            acc = acc[span:, :] + acc[:n_valid - span, :]
            n_valid -= span
            span *= 2
        win_sum = acc[n_valid - rows:, :]
        cnt = jnp.minimum(pos + 1, w).astype(jnp.float32)
        d = win_sum / cnt - p[:, cols]
        y = _dot(d.astype(jnp.bfloat16), pool_w_ref[g])
        pooled.append(y * pool_scale_ref[:, cols])
    return jnp.concatenate(pooled, axis=-1).astype(jnp.bfloat16)


N_MIX_INPUTS = 13


def _mixer_out_xattn(xs, subs, t, attn_ref, p_ref, halo_ref, pool_w_ref, pool_scale_ref,
                     w_out_ref, mix_g_ref, pre_g_ref, wq_ref, k_ref, v_ref, wo_ref, post_g_ref):
    tm = p_ref.shape[0]
    d = xs[0].shape[1]
    dh = d // XA_HEADS
    halo = jnp.where(t > 0, halo_ref[...], 0.0)
    ext = jnp.concatenate([halo, p_ref[...]], axis=0)
    pools = [_pooling_mixer(ext[rows.start:rows.stop + POOL_HALO, :], p_ref[rows, :],
                            t * tm + rows.start, pool_w_ref, pool_scale_ref) for rows in subs]
    ys = [_dot(jnp.concatenate([attn_ref[rows, :], pool], axis=-1), w_out_ref[...])
          for rows, pool in zip(subs, pools)]
    xs = [x + _rmsnorm(y, mix_g_ref[...]) for x, y in zip(xs, ys)]

    hs = [_rmsnorm(x, pre_g_ref[...]).astype(jnp.bfloat16) for x in xs]
    qs = [_dot(h, wq_ref[...]).astype(jnp.bfloat16) for h in hs]
    heads, cs = [[] for _ in subs], []
    head_cols = [slice(hd * dh, (hd + 1) * dh) for hd in range(XA_HEADS)]
    ss = [[lax.dot_general(q[:, cols], k_ref[0, :, cols], _NT,
                           preferred_element_type=jnp.float32) * (dh ** -0.5) for q in qs]
          for cols in head_cols]
    for i in range(len(subs)):
        for hd, cols in enumerate(head_cols):
            s = ss[hd][i]
            e = jnp.exp(s - jnp.max(s, axis=-1, keepdims=True))
            p = e / jnp.sum(e, axis=-1, keepdims=True)
            heads[i].append(_dot(p.astype(jnp.bfloat16), v_ref[0, :, cols]))
        cs.append(_dot(jnp.concatenate(heads[i], axis=-1).astype(jnp.bfloat16), wo_ref[...]))
    return [x + _rmsnorm(c, post_g_ref[...]) for x, c in zip(xs, cs)]


def _alibi_slopes(n_heads):
    return jnp.asarray(2.0 ** (-8.0 * np.arange(1, n_heads + 1) / n_heads), jnp.float32)


def kernel(x, mem, ffn1_pre_g, ffn1_w_gate, ffn1_w_up, ffn1_w_down, ffn1_post_g, mix_pre_g, w_in, pool_w, pool_scale, w_out, mix_post_g, xa_pre_g, mem_g, xa_wq, xa_wkv, xa_wo, xa_post_g, ffn2_pre_g, ffn2_w_gate, ffn2_w_up, ffn2_w_down, ffn2_post_g):
    b, s, d = x.shape
    depth = ffn1_pre_g.shape[0]
    bf = lambda w: w.astype(jnp.bfloat16)
    row = lambda g: g.reshape(1, -1)
    slopes = _alibi_slopes(N_ATTN_HEADS) * LOG2_E
    for l in range(depth):
        w = bf(w_in[l])
        x, qt, k, kmean, vt, p_in = _ffn(
            x.reshape(b * s, d), row(ffn1_pre_g[l]), ffn1_w_gate[l], ffn1_w_up[l], ffn1_w_down[l],
            row(ffn1_post_g[l]),
            proj=(row(mix_pre_g[l]), w[:, :D_ATTN].T, w[:, D_ATTN:2 * D_ATTN],
                  w[:, 2 * D_ATTN:3 * D_ATTN].T, w[:, 3 * D_ATTN:]))
        nb = s // MOBA_BLOCK
        x, p_in = x.reshape(b, s, d), p_in.reshape(b, s, D_POOL)
        qt, vt = (t.reshape(b, nb, D_ATTN, MOBA_BLOCK) for t in (qt, vt))
        attn = _moba(slopes, qt, k.reshape(b, s, D_ATTN), kmean.reshape(b, -1, D_ATTN), vt)
        k_mem, v_mem = _memkv(mem, row(mem_g[l]), bf(xa_wkv[l]))
        mix = (attn.reshape(b * s, D_ATTN), p_in.reshape(b * s, D_POOL), bf(pool_w[l]),
               row(pool_scale[l]), bf(w_out[l]), row(mix_post_g[l]), row(xa_pre_g[l]), bf(xa_wq[l]),
               k_mem, v_mem, bf(xa_wo[l]), row(xa_post_g[l]))
        x = _ffn(x.reshape(b * s, d), row(ffn2_pre_g[l]), ffn2_w_gate[l], ffn2_w_up[l],
                 ffn2_w_down[l], row(ffn2_post_g[l]), mix=mix, seq_len=s)[0].reshape(b, s, d)
    return x
```

```python
import functools

import numpy as np
import jax
import jax.numpy as jnp
from jax import lax
from jax.experimental import pallas as pl
from jax.experimental.pallas import tpu as pltpu

N_ATTN_HEADS = 8
ATTN_HEAD_DIM = 64
D_ATTN = N_ATTN_HEADS * ATTN_HEAD_DIM
POOL_WINDOWS = (2, 4, 8, 16)
POOL_GROUP_DIM = 128
D_POOL = len(POOL_WINDOWS) * POOL_GROUP_DIM
MOBA_BLOCK = 256
MOBA_TOPK = 3
XA_HEADS = 4
MACARON_WEIGHT = 0.5
RMS_EPS = 1e-6
LOG2_E = 1.4426950408889634

V7X_LANES = 128
V7X_VMEM_BYTES = 64 * 1024 * 1024
VMEM_LIMIT_BYTES = V7X_VMEM_BYTES * 7 // 8

TOKEN_TILE = 512
FFN_SUB_TILE = 256
WEIGHT_CHUNK_ROWS = 128
STAGE_SLOTS = 6
MIX_TOKEN_TILE = 1024
MIX_SUB_TILE = 256
HEADS_PER_STEP = V7X_LANES // ATTN_HEAD_DIM
POOL_HALO = max(POOL_WINDOWS)

_NT = (((1,), (1,)), ((), ()))


def _rmsnorm(x, g):
    r = lax.rsqrt(jnp.mean(x * x, axis=-1, keepdims=True) + RMS_EPS)
    return (x * r) * g


def _dot(a, b):
    return jnp.dot(a, b, preferred_element_type=jnp.float32)


def _params(*semantics):
    return pltpu.CompilerParams(dimension_semantics=semantics,
                                vmem_limit_bytes=VMEM_LIMIT_BYTES)


def _const_spec(shape):
    return pl.BlockSpec(shape, lambda *_: (0,) * len(shape))


def _stage_cast(pairs, stage_ref, sem):
    chunks = [(src, dst, r) for src, dst in pairs
              for r in range(0, src.shape[0], WEIGHT_CHUNK_ROWS)]

    def chunk_copy(c):
        src, _, r = chunks[c]
        slot = c % STAGE_SLOTS
        return pltpu.make_async_copy(
            src.at[pl.ds(r, WEIGHT_CHUNK_ROWS), :],
            stage_ref.at[slot, :, pl.ds(0, src.shape[1])], sem.at[slot])

    ahead = STAGE_SLOTS - 1
    for c in range(min(ahead, len(chunks))):
        chunk_copy(c).start()
    for c, (src, dst, r) in enumerate(chunks):
        if c + ahead < len(chunks):
            chunk_copy(c + ahead).start()
        chunk_copy(c).wait()
        dst[pl.ds(r, WEIGHT_CHUNK_ROWS), :] = (
            stage_ref[c % STAGE_SLOTS, :, pl.ds(0, src.shape[1])].astype(jnp.bfloat16))


def _ffn_kernel(*refs, with_proj):
    n_in, n_out = (11, 6) if with_proj else (6, 1)
    x_ref, pre_g_ref, wg_hbm, wu_hbm, wd_hbm, post_g_ref = refs[:6]
    o_ref = refs[n_in]
    wg_ref, wu_ref, wd_ref, stage_ref, sem = refs[n_in + n_out:]

    @pl.when(pl.program_id(0) == 0)
    def _():
        _stage_cast(((wg_hbm, wg_ref), (wu_hbm, wu_ref), (wd_hbm, wd_ref)), stage_ref, sem)

    subs = [slice(i * FFN_SUB_TILE, (i + 1) * FFN_SUB_TILE)
            for i in range(x_ref.shape[0] // FFN_SUB_TILE)]
    xs = [x_ref[rows, :] for rows in subs]
    hs = [_rmsnorm(x, pre_g_ref[...]).astype(jnp.bfloat16) for x in xs]
    gates = [_dot(h, wg_ref[...]) for h in hs]
    ups = [_dot(h, wu_ref[...]) for h in hs]
    acts = [(g * jax.nn.sigmoid(g) * u).astype(jnp.bfloat16) for g, u in zip(gates, ups)]
    fs = [_dot(a, wd_ref[...]) for a in acts]
    outs = [x + MACARON_WEIGHT * _rmsnorm(f, post_g_ref[...]) for x, f in zip(xs, fs)]
    for rows, out in zip(subs, outs):
        o_ref[rows, :] = out
    if with_proj:
        _mixer_projection(outs, subs, *refs[6:n_in], *refs[n_in + 1:n_in + n_out])


def _mixer_projection(xs, subs, g_ref, wqt_ref, wk_ref, wvt_ref, wp_ref,
                      qt_ref, k_ref, kmean_ref, vt_ref, p_ref):
    hs = [_rmsnorm(x, g_ref[...]).astype(jnp.bfloat16) for x in xs]
    for blk, (rows, h) in enumerate(zip(subs, hs)):
        qt = lax.dot_general(wqt_ref[...], h, _NT, preferred_element_type=jnp.float32)
        k = _dot(h, wk_ref[...])
        vt = lax.dot_general(wvt_ref[...], h, _NT, preferred_element_type=jnp.float32)
        p_ref[rows, :] = _dot(h, wp_ref[...])
        qt_ref[blk] = (qt * (ATTN_HEAD_DIM ** -0.5 * LOG2_E)).astype(jnp.bfloat16)
        vt_ref[blk] = vt.astype(jnp.bfloat16)
        k_ref[blk] = k.astype(jnp.bfloat16)
        kmean_ref[blk] = jnp.sum(k, axis=0, keepdims=True) * (1.0 / MOBA_BLOCK)


def _ffn(x, pre_g, wg, wu, wd, post_g, proj=None):
    n, d = x.shape
    d_ff = wg.shape[1]
    assert d % WEIGHT_CHUNK_ROWS == 0 and d_ff % WEIGHT_CHUNK_ROWS == 0
    assert FFN_SUB_TILE == MOBA_BLOCK
    tile = pl.BlockSpec((TOKEN_TILE, d), lambda i: (i, 0))
    hbm = pl.BlockSpec(memory_space=pl.ANY)
    in_specs = [tile, _const_spec((1, d)), hbm, hbm, hbm, _const_spec((1, d))]
    out_specs = [tile]
    out_shape = [jax.ShapeDtypeStruct((n, d), jnp.float32)]
    operands = [x, pre_g, wg, wu, wd, post_g]
    if proj is not None:
        bpt = TOKEN_TILE // MOBA_BLOCK
        nblk = n // MOBA_BLOCK
        in_specs += [_const_spec(w.shape) for w in proj]
        operands += list(proj)
        blocked = lambda r, c: pl.BlockSpec((bpt, r, c), lambda i: (i, 0, 0))
        out_specs += [blocked(D_ATTN, MOBA_BLOCK), blocked(MOBA_BLOCK, D_ATTN), blocked(1, D_ATTN),
                      blocked(D_ATTN, MOBA_BLOCK), pl.BlockSpec((TOKEN_TILE, D_POOL), lambda i: (i, 0))]
        out_shape += [jax.ShapeDtypeStruct((nblk, D_ATTN, MOBA_BLOCK), jnp.bfloat16),
                      jax.ShapeDtypeStruct((nblk, MOBA_BLOCK, D_ATTN), jnp.bfloat16),
                      jax.ShapeDtypeStruct((nblk, 1, D_ATTN), jnp.float32),
                      jax.ShapeDtypeStruct((nblk, D_ATTN, MOBA_BLOCK), jnp.bfloat16),
                      jax.ShapeDtypeStruct((n, D_POOL), jnp.float32)]
    return pl.pallas_call(
        functools.partial(_ffn_kernel, with_proj=proj is not None),
        grid=(n // TOKEN_TILE,),
        in_specs=in_specs,
        out_specs=out_specs,
        out_shape=out_shape,
        scratch_shapes=[
            pltpu.VMEM((d, d_ff), jnp.bfloat16), pltpu.VMEM((d, d_ff), jnp.bfloat16),
            pltpu.VMEM((d_ff, d), jnp.bfloat16),
            pltpu.VMEM((STAGE_SLOTS, WEIGHT_CHUNK_ROWS, max(d, d_ff)), jnp.float32),
            pltpu.SemaphoreType.DMA((STAGE_SLOTS,)),
        ],
        compiler_params=_params("arbitrary"),
        name="ffn_mixer_proj" if proj is not None else "ffn",
    )(*operands)


MOBA_TILE = 2 * MOBA_BLOCK
SELECT_UNROLL = 4
ONES_ROWS = 16


def _moba_items(n_tiles):
    tiles, pairs = [], []
    for c in range(n_tiles):
        for pr in range(c, -1, -1):
            tiles.append(c)
            pairs.append(pr)
    return np.asarray(tiles, np.int32), np.asarray(pairs, np.int32)


def _moba_kernel(slopes_ref, tile_tab_ref, pair_tab_ref, qt_ref, k_ref, kmean_ref, vt_ref,
                 o_ref, sel_ref, bias_ref, s_ref, p_ref, acc_ref):
    hp = pl.program_id(1)
    L, T = MOBA_BLOCK, MOBA_TILE
    nb = vt_ref.shape[1]
    n_tiles = nb // 2
    n_items = tile_tab_ref.shape[0]
    heads = range(HEADS_PER_STEP)
    slopes = [slopes_ref[hp * HEADS_PER_STEP + hh] for hh in heads]
    head_rows = [slice(hh * ATTN_HEAD_DIM, (hh + 1) * ATTN_HEAD_DIM) for hh in heads]
    head_row = lax.broadcasted_iota(jnp.int32, (HEADS_PER_STEP * ATTN_HEAD_DIM, T), 0) // ATTN_HEAD_DIM

    def query_tile(c):
        qt = jnp.concatenate([qt_ref[0, 2 * c], qt_ref[0, 2 * c + 1]], axis=1)
        return [jnp.where(head_row == hh, qt, jnp.zeros_like(qt)) for hh in heads]

    key = lax.broadcasted_iota(jnp.int32, (T, T), 0)
    query = lax.broadcasted_iota(jnp.int32, (T, T), 1)
    q_minus_k = (query - key).astype(jnp.float32)
    non_causal = jnp.where(key > query, jnp.inf, 0.0)
    for hh in heads:
        base = slopes[hh] * q_minus_k
        bias_ref[hh, 0] = base
        bias_ref[hh, 1] = base + non_causal

    kmean = kmean_ref[0].astype(jnp.bfloat16)
    blk_id = lax.broadcasted_iota(jnp.int32, (nb, T), 0)
    second_half = lax.broadcasted_iota(jnp.int32, (nb, T), 1) // L

    def select_tile(c):
        qt_heads = query_tile(c)
        own = 2 * c + second_half
        for hh in heads:
            gate = jnp.where(blk_id < own, _dot(kmean, qt_heads[hh]), -jnp.inf)
            sel = blk_id == own
            for _ in range(MOBA_TOPK):
                best = jnp.max(gate, axis=0, keepdims=True)
                first = jnp.min(jnp.where(gate == best, blk_id, nb), axis=0, keepdims=True)
                taken = blk_id == first
                sel = sel | (taken & (blk_id < own))
                gate = jnp.where(taken, -jnp.inf, gate)
            sel_ref[hh, c] = sel.astype(jnp.float32)

    def select(it, _):
        for u in range(SELECT_UNROLL):
            select_tile(SELECT_UNROLL * it + u)
        return 0

    lax.fori_loop(0, n_tiles // SELECT_UNROLL, select, 0)

    blk_rows = [slice(blk * L, (blk + 1) * L) for blk in range(2)]
    ones_rows = jnp.ones((ONES_ROWS, L), jnp.bfloat16)

    def biased_scores(e, slot):
        c, pr = tile_tab_ref[e], pair_tab_ref[e]
        variant = jnp.where(pr == c, 1, 0)
        k_pair = k_ref[0, pl.ds(pl.multiple_of(pr * T, T), T), :]
        col_max = []
        for hh, qt_h in enumerate(query_tile(c)):
            t = _dot(k_pair, qt_h) - bias_ref[hh, variant]
            s_ref[slot, hh] = t
            col_max.append(tuple(jnp.max(t[rows, :], axis=0, keepdims=True) for rows in blk_rows))
        return tuple(col_max)

    def value_product(pr, slot):
        out = []
        for hh in heads:
            lhs = [jnp.concatenate([vt_ref[0, 2 * pr + blk, head_rows[hh], :], ones_rows], axis=0)
                   for blk in range(2)]
            out.append(_dot(lhs[0], p_ref[slot, hh, blk_rows[0], :])
                       + _dot(lhs[1], p_ref[slot, hh, blk_rows[1], :]))
        return out

    def write_tile(c):
        o_t = jnp.concatenate(
            [acc_ref[hh, :ATTN_HEAD_DIM, :] / acc_ref[hh, ATTN_HEAD_DIM:ATTN_HEAD_DIM + 1, :]
             for hh in heads], axis=0)
        o_ref[0, pl.ds(pl.multiple_of(c * T, T), T), :] = o_t.T.astype(o_ref.dtype)

    def item(e, slot, state):
        m, alpha_lag1, alpha_lag2, col_max = state
        other = 1 - slot
        c, pr = tile_tab_ref[e], pair_tab_ref[e]
        e_lag2 = jnp.maximum(e - 2, 0)

        pv_lag2 = value_product(pair_tab_ref[e_lag2], slot)
        col_max_next = biased_scores(jnp.minimum(e + 1, n_items - 1), other)

        opens = pr == c
        m_out, alpha_out = [], []
        for hh in heads:
            m_old = jnp.where(opens, -jnp.inf, m[hh])
            offset = slopes[hh] * ((c - pr) * T).astype(jnp.float32)
            keep = [sel_ref[hh, c, pl.ds(2 * pr + blk, 1), :] > 0.5 for blk in range(2)]
            m_blk = [jnp.where(keep[blk], col_max[hh][blk], -jnp.inf) for blk in range(2)]
            m_new = jnp.maximum(m_old, jnp.maximum(m_blk[0], m_blk[1]) - offset)
            for blk in range(2):
                shift = jnp.where(keep[blk], m_new + offset, jnp.inf)
                p = jnp.exp2(s_ref[slot, hh, blk_rows[blk], :] - shift)
                p_ref[slot, hh, blk_rows[blk], :] = p.astype(jnp.bfloat16)
            m_out.append(m_new)
            alpha_out.append(jnp.exp2(m_old - m_new))

        fold(pv_lag2, alpha_lag2)

        @pl.when((pair_tab_ref[e_lag2] == 0) & (e >= 2))
        def _():
            write_tile(tile_tab_ref[e_lag2])

        return tuple(m_out), tuple(alpha_out), alpha_lag1, col_max_next

    def fold(pv, alpha):
        for hh in heads:
            acc_ref[hh] = alpha[hh] * acc_ref[hh] + pv[hh]

    col_max = biased_scores(0, 0)
    for hh in heads:
        acc_ref[hh] = jnp.zeros(acc_ref.shape[1:], jnp.float32)
    p_ref[...] = jnp.zeros(p_ref.shape, p_ref.dtype)
    row = lambda v: tuple(jnp.full((1, T), v, jnp.float32) for _ in heads)
    state = (row(-jnp.inf), row(0.0), row(0.0), col_max)

    def two_items(it, state):
        state = item(2 * it, 0, state)
        return item(2 * it + 1, 1, state)

    _, alpha_lag1, alpha_lag2, _ = lax.fori_loop(0, n_items // 2, two_items, state)

    for e, alpha in ((n_items - 2, alpha_lag2), (n_items - 1, alpha_lag1)):
        fold(value_product(pair_tab_ref[e], e % 2), alpha)
    write_tile(tile_tab_ref[n_items - 1])


def _moba(slopes, qt, k, kmean, vt):
    b, nb, _, L = qt.shape
    lanes = HEADS_PER_STEP * ATTN_HEAD_DIM
    T = MOBA_TILE
    tile_tab, pair_tab = _moba_items(nb // 2)
    assert len(tile_tab) % 2 == 0
    smem = pl.BlockSpec(memory_space=pltpu.SMEM)
    return pl.pallas_call(
        _moba_kernel,
        grid=(b, N_ATTN_HEADS // HEADS_PER_STEP),
        in_specs=[
            smem, smem, smem,
            pl.BlockSpec((1, nb, lanes, L), lambda bi, hp: (bi, 0, hp, 0)),
            pl.BlockSpec((1, nb * L, lanes), lambda bi, hp: (bi, 0, hp)),
            pl.BlockSpec((1, nb, lanes), lambda bi, hp: (bi, 0, hp)),
            pl.BlockSpec((1, nb, lanes, L), lambda bi, hp: (bi, 0, hp, 0)),
        ],
        out_specs=pl.BlockSpec((1, nb * L, lanes), lambda bi, hp: (bi, 0, hp)),
        out_shape=jax.ShapeDtypeStruct((b, nb * L, D_ATTN), jnp.bfloat16),
        scratch_shapes=[
            pltpu.VMEM((HEADS_PER_STEP, nb // 2, nb, T), jnp.float32),
            pltpu.VMEM((HEADS_PER_STEP, 2, T, T), jnp.float32),
            pltpu.VMEM((2, HEADS_PER_STEP, T, T), jnp.float32),
            pltpu.VMEM((2, HEADS_PER_STEP, T, T), jnp.bfloat16),
            pltpu.VMEM((HEADS_PER_STEP, ATTN_HEAD_DIM + ONES_ROWS, T), jnp.float32),
        ],
        compiler_params=_params("parallel", "parallel"),
        name="moba_attention",
    )(slopes, jnp.asarray(tile_tab), jnp.asarray(pair_tab), qt, k, kmean, vt)


def _memkv_kernel(mem_ref, g_ref, wkv_ref, k_ref, v_ref):
    d = mem_ref.shape[2]
    mem_n = _rmsnorm(mem_ref[0], g_ref[...]).astype(jnp.bfloat16)
    kv = _dot(mem_n, wkv_ref[...])
    k_ref[0] = kv[:, :d].astype(jnp.bfloat16)
    v_ref[0] = kv[:, d:].astype(jnp.bfloat16)


def _memkv(mem, g, wkv):
    b, m, d = mem.shape
    blk = pl.BlockSpec((1, m, d), lambda bi: (bi, 0, 0))
    return pl.pallas_call(
        _memkv_kernel,
        grid=(b,),
        in_specs=[blk, _const_spec((1, d)), _const_spec(wkv.shape)],
        out_specs=[blk, blk],
        out_shape=[jax.ShapeDtypeStruct((b, m, d), jnp.bfloat16)] * 2,
        compiler_params=_params("parallel"),
        name="mem_kv",
    )(mem, g, wkv)


def _pooling_mixer(ext, p, first_pos, pool_w_ref, pool_scale_ref):
    rows = p.shape[0]
    pos = first_pos + lax.broadcasted_iota(jnp.int32, (rows, 1), 0)
    pooled = []
    for g, w in enumerate(POOL_WINDOWS):
        cols = slice(g * POOL_GROUP_DIM, (g + 1) * POOL_GROUP_DIM)
        acc = ext[:, cols]
        n_valid = acc.shape[0]
        span = 1
        while span < w:
            acc = acc[span:, :] + acc[:n_valid - span, :]
            n_valid -= span
            span *= 2
        win_sum = acc[n_valid - rows:, :]
        cnt = jnp.minimum(pos + 1, w).astype(jnp.float32)
        d = win_sum / cnt - p[:, cols]
        y = _dot(d.astype(jnp.bfloat16), pool_w_ref[g])
        pooled.append(y * pool_scale_ref[:, cols])
    return jnp.concatenate(pooled, axis=-1).astype(jnp.bfloat16)


def _mix_xattn_kernel(x_ref, attn_ref, p_ref, halo_ref, pool_w_ref, pool_scale_ref,
                      w_out_ref, mix_g_ref, pre_g_ref, wq_ref, k_ref, v_ref, wo_ref,
                      post_g_ref, o_ref, w_out_bf, wq_bf, wo_bf):
    t = pl.program_id(1)
    tm, d = x_ref.shape[1], x_ref.shape[2]

    @pl.when((pl.program_id(0) == 0) & (t == 0))
    def _():
        w_out_bf[...] = w_out_ref[...].astype(jnp.bfloat16)
        wq_bf[...] = wq_ref[...].astype(jnp.bfloat16)
        wo_bf[...] = wo_ref[...].astype(jnp.bfloat16)
    w_out_ref, wq_ref, wo_ref = w_out_bf, wq_bf, wo_bf
    dh = d // XA_HEADS
    subs = [slice(i * MIX_SUB_TILE, (i + 1) * MIX_SUB_TILE) for i in range(tm // MIX_SUB_TILE)]
    halo = jnp.where(t > 0, halo_ref[0], 0.0)
    ext = jnp.concatenate([halo, p_ref[0]], axis=0)
    pools = [_pooling_mixer(ext[rows.start:rows.stop + POOL_HALO, :], p_ref[0, rows, :],
                            t * tm + rows.start, pool_w_ref, pool_scale_ref) for rows in subs]
    ys = [_dot(jnp.concatenate([attn_ref[0, rows, :], pool], axis=-1), w_out_ref[...])
          for rows, pool in zip(subs, pools)]
    xs = [x_ref[0, rows, :] + _rmsnorm(y, mix_g_ref[...]) for rows, y in zip(subs, ys)]

    hs = [_rmsnorm(x, pre_g_ref[...]).astype(jnp.bfloat16) for x in xs]
    qs = [_dot(h, wq_ref[...]).astype(jnp.bfloat16) for h in hs]
    heads, cs = [[] for _ in subs], []
    head_cols = [slice(hd * dh, (hd + 1) * dh) for hd in range(XA_HEADS)]
    ss = [[lax.dot_general(q[:, cols], k_ref[0, :, cols], _NT,
                           preferred_element_type=jnp.float32) * (dh ** -0.5) for q in qs]
          for cols in head_cols]
    for i in range(len(subs)):
        for hd, cols in enumerate(head_cols):
            s = ss[hd][i]
            e = jnp.exp(s - jnp.max(s, axis=-1, keepdims=True))
            p = e / jnp.sum(e, axis=-1, keepdims=True)
            heads[i].append(_dot(p.astype(jnp.bfloat16), v_ref[0, :, cols]))
        cs.append(_dot(jnp.concatenate(heads[i], axis=-1).astype(jnp.bfloat16), wo_ref[...]))
    for rows, x, c in zip(subs, xs, cs):
        o_ref[0, rows, :] = x + _rmsnorm(c, post_g_ref[...])


def _mix_xattn(x, attn, p_in, pool_w, pool_scale, w_out, mix_g, pre_g, wq, k_mem, v_mem, wo,
               post_g):
    b, s, d = x.shape
    m = k_mem.shape[1]
    halo_blocks = MIX_TOKEN_TILE // POOL_HALO
    tile = lambda width: pl.BlockSpec((1, MIX_TOKEN_TILE, width), lambda bi, t: (bi, t, 0))
    mem_blk = pl.BlockSpec((1, m, d), lambda bi, t: (bi, 0, 0))
    return pl.pallas_call(
        _mix_xattn_kernel,
        grid=(b, s // MIX_TOKEN_TILE),
        in_specs=[
            tile(d), tile(D_ATTN), tile(D_POOL),
            pl.BlockSpec((1, POOL_HALO, D_POOL),
                         lambda bi, t: (bi, jnp.maximum(t * halo_blocks - 1, 0), 0)),
            _const_spec(pool_w.shape), _const_spec((1, D_POOL)),
            _const_spec(w_out.shape), _const_spec((1, d)),
            _const_spec((1, d)), _const_spec(wq.shape), mem_blk, mem_blk,
            _const_spec(wo.shape), _const_spec((1, d)),
        ],
        out_specs=tile(d),
        out_shape=jax.ShapeDtypeStruct((b, s, d), jnp.float32),
        scratch_shapes=[pltpu.VMEM(w.shape, jnp.bfloat16) for w in (w_out, wq, wo)],
        compiler_params=_params("arbitrary", "arbitrary"),
        name="mixer_out_mem_xattn",
    )(x, attn, p_in, p_in, pool_w, pool_scale, w_out, mix_g, pre_g, wq, k_mem, v_mem, wo, post_g)


def _alibi_slopes(n_heads):
    return jnp.asarray(2.0 ** (-8.0 * np.arange(1, n_heads + 1) / n_heads), jnp.float32)


def kernel(x, mem, ffn1_pre_g, ffn1_w_gate, ffn1_w_up, ffn1_w_down, ffn1_post_g, mix_pre_g, w_in, pool_w, pool_scale, w_out, mix_post_g, xa_pre_g, mem_g, xa_wq, xa_wkv, xa_wo, xa_post_g, ffn2_pre_g, ffn2_w_gate, ffn2_w_up, ffn2_w_down, ffn2_post_g):
    b, s, d = x.shape
    depth = ffn1_pre_g.shape[0]
    bf = lambda w: w.astype(jnp.bfloat16)
    row = lambda g: g.reshape(1, -1)
    slopes = _alibi_slopes(N_ATTN_HEADS) * LOG2_E
    for l in range(depth):
        w = bf(w_in[l])
        x, qt, k, kmean, vt, p_in = _ffn(
            x.reshape(b * s, d), row(ffn1_pre_g[l]), ffn1_w_gate[l], ffn1_w_up[l], ffn1_w_down[l],
            row(ffn1_post_g[l]),
            proj=(row(mix_pre_g[l]), w[:, :D_ATTN].T, w[:, D_ATTN:2 * D_ATTN],
                  w[:, 2 * D_ATTN:3 * D_ATTN].T, w[:, 3 * D_ATTN:]))
        nb = s // MOBA_BLOCK
        x, p_in = x.reshape(b, s, d), p_in.reshape(b, s, D_POOL)
        qt, vt = (t.reshape(b, nb, D_ATTN, MOBA_BLOCK) for t in (qt, vt))
        attn = _moba(slopes, qt, k.reshape(b, s, D_ATTN), kmean.reshape(b, -1, D_ATTN), vt)
        k_mem, v_mem = _memkv(mem, row(mem_g[l]), bf(xa_wkv[l]))
        x = _mix_xattn(x, attn, p_in, bf(pool_w[l]), row(pool_scale[l]), w_out[l],
                       row(mix_post_g[l]), row(xa_pre_g[l]), xa_wq[l], k_mem, v_mem,
                       xa_wo[l], row(xa_post_g[l]))

        x = _ffn(x.reshape(b * s, d), row(ffn2_pre_g[l]), ffn2_w_gate[l], ffn2_w_up[l],
                 ffn2_w_down[l], row(ffn2_post_g[l]))[0].reshape(b, s, d)
    return x
```
